```python
import jax
import jax.numpy as jnp
from jax import lax
import numpy as np

D_MODEL = 1024
BATCH = 8
SEQ = 2048
DEPTH = 2
DEC_BATCH = 128
DEC_SEQ = 8
PAST_LEN = 2048
PAGE_SIZE = 128

N_EVEN = (DEPTH + 1) // 2
N_ODD = DEPTH // 2
MIX_HALF = D_MODEL // 2
POOL_WINDOWS = (2, 4, 8, 16)
N_POOL_GROUPS = len(POOL_WINDOWS)
POOL_GROUP = MIX_HALF // N_POOL_GROUPS
POOL_BUF = max(POOL_WINDOWS) - 1
HEAD_DIM = 64
N_HEADS = MIX_HALF // HEAD_DIM
FORGET_BIAS = 3.0
Q_BLOCK = 128
CONV_WIDTH = 31
CONV_BUF = CONV_WIDTH - 1
N_MEM = 256
X_HEADS = 4
X_HEAD_DIM = 128
X_WIDTH = X_HEADS * X_HEAD_DIM
D_FF = 2816
N_EXPERTS = 8
TOP_K = 2
D_FF_EXPERT = 3584
EPS = 1e-6
IN_EVEN = 4 * MIX_HALF + N_HEADS
IN_ODD = 5 * MIX_HALF

kernel_name = 'hybrid_pool_fox_conformer_stickbreak_decode_step'


def rmsnorm(x, g):
    xf = x.astype(jnp.float32)
    y = xf * lax.rsqrt(jnp.mean(xf * xf, axis=-1, keepdims=True) + EPS)
    return (y * g.astype(jnp.float32)).astype(x.dtype)


def layernorm(x, g, b):
    xf = x.astype(jnp.float32)
    mu = jnp.mean(xf, axis=-1, keepdims=True)
    var = jnp.mean(jnp.square(xf - mu), axis=-1, keepdims=True)
    y = (xf - mu) * lax.rsqrt(var + EPS)
    return (y * g.astype(jnp.float32) + b.astype(jnp.float32)).astype(x.dtype)


def gather_pages(pool, page_table):
    rows = pool[page_table]
    return rows.reshape(page_table.shape[0], -1, *pool.shape[2:])


def sweep_query_blocks(block_fn, qs, q_pos):
    L = q_pos.shape[0]
    qb = min(Q_BLOCK, L)
    nb = -(-L // qb)
    pad = nb * qb - L

    def prep(a):
        a = jnp.pad(a, [(0, 0), (0, pad)] + [(0, 0)] * (a.ndim - 2))
        a = a.reshape(a.shape[0], nb, qb, *a.shape[2:])
        return jnp.moveaxis(a, 1, 0)

    qs_b = tuple(prep(a) for a in qs)
    pos_b = jnp.pad(q_pos, (0, pad), mode='edge').reshape(nb, qb)
    out = lax.map(lambda args: block_fn(*args), (*qs_b, pos_b))
    out = jnp.moveaxis(out, 0, 1)
    out = out.reshape(out.shape[0], nb * qb, *out.shape[3:])
    return out[:, :L]


def fox_attention(q, k, v, Fq, Fk, q_pos, k_pos):
    scale = HEAD_DIM ** -0.5
    FkT = jnp.swapaxes(Fk, 1, 2)[:, :, None, :]

    def block(qb, fqb, pb):
        s = jnp.einsum('bqhd,bkhd->bhqk', qb, k, preferred_element_type=jnp.float32) * scale
        s = s + jnp.swapaxes(fqb, 1, 2)[..., None] - FkT
        mask = k_pos[None, :] <= pb[:, None]
        p = jax.nn.softmax(jnp.where(mask, s, -jnp.inf), axis=-1)
        return jnp.einsum('bhqk,bkhd->bqhd', p.astype(v.dtype), v)

    return sweep_query_blocks(block, (q, Fq), q_pos)


def stick_breaking_attention(q, k, v, q_pos, k_pos):
    scale = HEAD_DIM ** -0.5

    def block(qb, pb):
        z = jnp.einsum('bqhd,bkhd->bhqk', qb, k, preferred_element_type=jnp.float32) * scale
        mask = k_pos[None, :] < pb[:, None]
        log_not = jnp.where(mask, jax.nn.log_sigmoid(-z), 0.0)
        csum = jnp.cumsum(log_not, axis=-1)
        log_a = jax.nn.log_sigmoid(z) + csum[..., -1:] - csum
        a = jnp.where(mask, jnp.exp(log_a), 0.0)
        return jnp.einsum('bhqk,bkhd->bqhd', a.astype(v.dtype), v)

    return sweep_query_blocks(block, (q,), q_pos)


def pool_mixer(p, prefix, start_pos, w_grp, scale):
    B, L, _ = p.shape
    full = jnp.concatenate([prefix, p], axis=1)
    c = jnp.pad(jnp.cumsum(full.astype(jnp.float32), axis=1), ((0, 0), (1, 0), (0, 0)))
    pos = start_pos + jnp.arange(L)
    means = []
    for g, w in enumerate(POOL_WINDOWS):
        sl = slice(g * POOL_GROUP, (g + 1) * POOL_GROUP)
        wsum = c[:, POOL_BUF + 1:, sl] - c[:, POOL_BUF + 1 - w:POOL_BUF + 1 - w + L, sl]
        count = jnp.minimum(pos + 1, w).astype(jnp.float32)
        means.append(wsum / count[None, :, None])
    mean = jnp.concatenate(means, axis=-1).astype(p.dtype)
    d = (mean - p).reshape(B, L, N_POOL_GROUPS, POOL_GROUP)
    y = jnp.einsum('blgc,gcd->blgd', d, w_grp).reshape(B, L, MIX_HALF)
    return y * scale, full[:, -POOL_BUF:]


def conv_module(a, gate, prefix, conv_w, conv_b, ln_g, ln_b):
    u = a * jax.nn.sigmoid(gate)
    full = jnp.concatenate([prefix, u], axis=1)
    y = lax.conv_general_dilated(full, conv_w[:, None, :], window_strides=(1,), padding='VALID',
                                 dimension_numbers=('NWC', 'WIO', 'NWC'),
                                 feature_group_count=MIX_HALF) + conv_b
    y = jax.nn.silu(layernorm(y, ln_g, ln_b))
    return y, full[:, -CONV_BUF:]


def even_mixer(h, past_k, past_v, past_logf, pool_prefix, w_in, b_f, w_pool_grp, pool_scale, g_fq, g_fk, w_out):
    B, L, _ = h.shape
    P = past_k.shape[1]
    z = h @ w_in
    p, q, k, v, fl = jnp.split(z, [MIX_HALF, 2 * MIX_HALF, 3 * MIX_HALF, 4 * MIX_HALF], axis=-1)
    y_a, pool_state = pool_mixer(p, pool_prefix, P, w_pool_grp, pool_scale)
    q = rmsnorm(q.reshape(B, L, N_HEADS, HEAD_DIM), g_fq)
    k = rmsnorm(k.reshape(B, L, N_HEADS, HEAD_DIM), g_fk)
    v = v.reshape(B, L, N_HEADS, HEAD_DIM)
    logf = jax.nn.log_sigmoid(fl.astype(jnp.float32) + b_f.astype(jnp.float32))
    k_all = jnp.concatenate([past_k, k], axis=1)
    v_all = jnp.concatenate([past_v, v], axis=1)
    F = jnp.cumsum(jnp.concatenate([past_logf.astype(jnp.float32), logf], axis=1), axis=1)
    y_b = fox_attention(q, k_all, v_all, F[:, P:], F, P + jnp.arange(L), jnp.arange(P + L))
    y = jnp.concatenate([y_a, y_b.reshape(B, L, MIX_HALF)], axis=-1) @ w_out
    return y, (k, v, logf.astype(h.dtype), pool_state)


def odd_mixer(h, past_k, past_v, conv_prefix, w_in, conv_w, conv_b, ln_g, ln_b, w_out):
    B, L, _ = h.shape
    P = past_k.shape[1]
    z = h @ w_in
    a, g, q, k, v = jnp.split(z, [MIX_HALF, 2 * MIX_HALF, 3 * MIX_HALF, 4 * MIX_HALF], axis=-1)
    y_c, conv_state = conv_module(a, g, conv_prefix, conv_w, conv_b, ln_g, ln_b)
    q = q.reshape(B, L, N_HEADS, HEAD_DIM)
    k = k.reshape(B, L, N_HEADS, HEAD_DIM)
    v = v.reshape(B, L, N_HEADS, HEAD_DIM)
    k_all = jnp.concatenate([past_k, k], axis=1)
    v_all = jnp.concatenate([past_v, v], axis=1)
    y_d = stick_breaking_attention(q, k_all, v_all, P + jnp.arange(L), jnp.arange(P + L))
    y = jnp.concatenate([y_c, y_d.reshape(B, L, MIX_HALF)], axis=-1) @ w_out
    return y, (k, v, conv_state)


def mem_keys_values(mem, w_ck, w_cv, g_ck):
    B, M, _ = mem.shape
    k = rmsnorm((mem @ w_ck).reshape(B, M, X_HEADS, X_HEAD_DIM), g_ck)
    v = (mem @ w_cv).reshape(B, M, X_HEADS, X_HEAD_DIM)
    return k, v


def cross_attention(h, mem_k, mem_v, w_cq, g_cq, w_co):
    B, L, _ = h.shape
    q = rmsnorm((h @ w_cq).reshape(B, L, X_HEADS, X_HEAD_DIM), g_cq)
    s = jnp.einsum('blhd,bmhd->bhlm', q, mem_k, preferred_element_type=jnp.float32) * X_HEAD_DIM ** -0.5
    p = jax.nn.softmax(s, axis=-1)
    o = jnp.einsum('bhlm,bmhd->blhd', p.astype(mem_v.dtype), mem_v).reshape(B, L, X_WIDTH)
    return o @ w_co


def swiglu(h, w_gate, w_up, w_down):
    return (jax.nn.silu(h @ w_gate) * (h @ w_up)) @ w_down


def moe_swiglu(h, w_router, b_router, we_gate, we_up, we_down):
    logits = (h @ w_router).astype(jnp.float32) + b_router.astype(jnp.float32)
    top_val, top_idx = lax.top_k(logits, TOP_K)
    gates = jax.nn.softmax(top_val, axis=-1)
    combine = jnp.sum(jax.nn.one_hot(top_idx, N_EXPERTS, dtype=jnp.float32) * gates[..., None], axis=-2)
    combine = combine.astype(h.dtype)
    out = jnp.zeros_like(h)
    for e in range(N_EXPERTS):
        out = out + combine[..., e:e + 1] * swiglu(h, we_gate[e], we_up[e], we_down[e])
    return out


def decoder_stack(x, mem_kv, fox_past, sb_past, pool_prefix, conv_prefix, W):
    new = {name: [] for name in ('fox_k', 'fox_v', 'fox_logf', 'sb_k', 'sb_v', 'pool', 'conv')}
    for layer in range(DEPTH):
        i = layer // 2
        h = rmsnorm(x, W['norm_mix'][layer])
        if layer % 2 == 0:
            y, (k, v, lf, ps) = even_mixer(h, *fox_past[i], pool_prefix[i], W['w_in_e'][i], W['b_f'][i],
                                           W['w_pool_grp'][i], W['pool_scale'][i], W['g_fq'][i],
                                           W['g_fk'][i], W['w_out_e'][i])
            new['fox_k'].append(k)
            new['fox_v'].append(v)
            new['fox_logf'].append(lf)
            new['pool'].append(ps)
        else:
            y, (k, v, cs) = odd_mixer(h, *sb_past[i], conv_prefix[i], W['w_in_o'][i], W['conv_w'][i],
                                      W['conv_b'][i], W['ln_g'][i], W['ln_b'][i], W['w_out_o'][i])
            new['sb_k'].append(k)
            new['sb_v'].append(v)
            new['conv'].append(cs)
        x = x + y
        x = x + cross_attention(rmsnorm(x, W['norm_cross'][layer]), *mem_kv[layer],
                                W['w_cq'][layer], W['g_cq'][layer], W['w_co'][layer])
        h = rmsnorm(x, W['norm_ffn'][layer])
        if layer % 2 == 0:
            x = x + swiglu(h, W['w_gate'][i], W['w_up'][i], W['w_down'][i])
        else:
            x = x + moe_swiglu(h, W['w_router'][i], W['b_router'][i], W['we_gate'][i],
                               W['we_up'][i], W['we_down'][i])
    return x, new


def setup_inputs(seed: int = 0) -> dict:
    key = jax.random.key(seed)
    keys = jax.random.split(key, 48)
    idx = iter(range(48))

    def nrm(shape, scale=1.0):
        return scale * jax.random.normal(keys[next(idx)], shape, jnp.float32)

    def gain(shape):
        return 1.0 + nrm(shape, 0.05)

    n_pages = PAST_LEN // PAGE_SIZE
    used = DEC_BATCH * n_pages
    n_pool = used + max(1, used // 4)
    D = D_MODEL
    x_prompt = nrm((BATCH, SEQ, D))
    x_sample = nrm((DEC_BATCH, DEC_SEQ, D))
    cache_fox_k = nrm((N_EVEN, n_pool, PAGE_SIZE, N_HEADS, HEAD_DIM))
    cache_fox_v = nrm((N_EVEN, n_pool, PAGE_SIZE, N_HEADS, HEAD_DIM))
    cache_fox_logf = jax.nn.log_sigmoid(FORGET_BIAS + nrm((N_EVEN, n_pool, PAGE_SIZE, N_HEADS)))
    cache_sb_k = nrm((N_ODD, n_pool, PAGE_SIZE, N_HEADS, HEAD_DIM))
    cache_sb_v = nrm((N_ODD, n_pool, PAGE_SIZE, N_HEADS, HEAD_DIM))
    cache_mem_k = nrm((DEPTH, DEC_BATCH, N_MEM, X_HEADS, X_HEAD_DIM))
    cache_mem_v = nrm((DEPTH, DEC_BATCH, N_MEM, X_HEADS, X_HEAD_DIM))
    state_pool = nrm((N_EVEN, DEC_BATCH, POOL_BUF, MIX_HALF))
    state_conv = nrm((N_ODD, DEC_BATCH, CONV_BUF, MIX_HALF), 0.5)
    page_table = jax.random.permutation(keys[next(idx)], n_pool)[:used].reshape(DEC_BATCH, n_pages).astype(jnp.int32)
    mem_prompt = nrm((BATCH, N_MEM, D))
    return {
        'x_prompt': x_prompt, 'x_sample': x_sample,
        'cache_fox_k': cache_fox_k, 'cache_fox_v': cache_fox_v, 'cache_fox_logf': cache_fox_logf,
        'cache_sb_k': cache_sb_k, 'cache_sb_v': cache_sb_v,
        'cache_mem_k': cache_mem_k, 'cache_mem_v': cache_mem_v,
        'state_pool': state_pool, 'state_conv': state_conv,
        'page_table': page_table, 'mem_prompt': mem_prompt,
        'norm_mix': gain((DEPTH, D)), 'norm_cross': gain((DEPTH, D)), 'norm_ffn': gain((DEPTH, D)),
        'w_cq': nrm((DEPTH, D, X_WIDTH), D ** -0.5), 'w_ck': nrm((DEPTH, D, X_WIDTH), D ** -0.5),
        'w_cv': nrm((DEPTH, D, X_WIDTH), D ** -0.5), 'w_co': nrm((DEPTH, X_WIDTH, D), X_WIDTH ** -0.5),
        'g_cq': gain((DEPTH, X_HEAD_DIM)), 'g_ck': gain((DEPTH, X_HEAD_DIM)),
        'w_in_e': nrm((N_EVEN, D, IN_EVEN), D ** -0.5), 'b_f': FORGET_BIAS + nrm((N_EVEN, N_HEADS), 0.1),
        'w_pool_grp': nrm((N_EVEN, N_POOL_GROUPS, POOL_GROUP, POOL_GROUP), POOL_GROUP ** -0.5),
        'pool_scale': gain((N_EVEN, MIX_HALF)),
        'g_fq': gain((N_EVEN, HEAD_DIM)), 'g_fk': gain((N_EVEN, HEAD_DIM)),
        'w_out_e': nrm((N_EVEN, 2 * MIX_HALF, D), (2 * MIX_HALF) ** -0.5),
        'w_gate': nrm((N_EVEN, D, D_FF), D ** -0.5), 'w_up': nrm((N_EVEN, D, D_FF), D ** -0.5),
        'w_down': nrm((N_EVEN, D_FF, D), D_FF ** -0.5),
        'w_in_o': nrm((N_ODD, D, IN_ODD), D ** -0.5),
        'conv_w': nrm((N_ODD, CONV_WIDTH, MIX_HALF), CONV_WIDTH ** -0.5),
        'conv_b': nrm((N_ODD, MIX_HALF), 0.01),
        'ln_g': gain((N_ODD, MIX_HALF)), 'ln_b': nrm((N_ODD, MIX_HALF), 0.01),
        'w_out_o': nrm((N_ODD, 2 * MIX_HALF, D), (2 * MIX_HALF) ** -0.5),
        'w_router': nrm((N_ODD, D, N_EXPERTS), D ** -0.5), 'b_router': nrm((N_ODD, N_EXPERTS), 0.01),
        'we_gate': nrm((N_ODD, N_EXPERTS, D, D_FF_EXPERT), D ** -0.5),
        'we_up': nrm((N_ODD, N_EXPERTS, D, D_FF_EXPERT), D ** -0.5),
        'we_down': nrm((N_ODD, N_EXPERTS, D_FF_EXPERT, D), D_FF_EXPERT ** -0.5),
    }


def reference(x_prompt, x_sample, cache_fox_k, cache_fox_v, cache_fox_logf, cache_sb_k, cache_sb_v,
              cache_mem_k, cache_mem_v, state_pool, state_conv, page_table, mem_prompt,
              norm_mix, norm_cross, norm_ffn, w_cq, w_ck, w_cv, w_co, g_cq, g_ck,
              w_in_e, b_f, w_pool_grp, pool_scale, g_fq, g_fk, w_out_e, w_gate, w_up, w_down,
              w_in_o, conv_w, conv_b, ln_g, ln_b, w_out_o, w_router, b_router, we_gate, we_up, we_down):
    W = dict(norm_mix=norm_mix, norm_cross=norm_cross, norm_ffn=norm_ffn, w_cq=w_cq, w_co=w_co,
             g_cq=g_cq, w_in_e=w_in_e, b_f=b_f, w_pool_grp=w_pool_grp, pool_scale=pool_scale,
             g_fq=g_fq, g_fk=g_fk, w_out_e=w_out_e, w_gate=w_gate, w_up=w_up, w_down=w_down,
             w_in_o=w_in_o, conv_w=conv_w, conv_b=conv_b, ln_g=ln_g, ln_b=ln_b, w_out_o=w_out_o,
             w_router=w_router, b_router=b_router, we_gate=we_gate, we_up=we_up, we_down=we_down)

    B = x_prompt.shape[0]
    dt = x_prompt.dtype
    empty_kv = jnp.zeros((B, 0, N_HEADS, HEAD_DIM), dt)
    empty_lf = jnp.zeros((B, 0, N_HEADS), dt)
    mem_kv_p = [mem_keys_values(mem_prompt, w_ck[l], w_cv[l], g_ck[l]) for l in range(DEPTH)]
    fox_past_p = [(empty_kv, empty_kv, empty_lf) for _ in range(N_EVEN)]
    sb_past_p = [(empty_kv, empty_kv) for _ in range(N_ODD)]
    pool_prefix_p = [jnp.zeros((B, POOL_BUF, MIX_HALF), dt) for _ in range(N_EVEN)]
    conv_prefix_p = [jnp.zeros((B, CONV_BUF, MIX_HALF), dt) for _ in range(N_ODD)]
    y_prompt, sp = decoder_stack(x_prompt, mem_kv_p, fox_past_p, sb_past_p, pool_prefix_p, conv_prefix_p, W)

    mem_kv_s = [(cache_mem_k[l], cache_mem_v[l]) for l in range(DEPTH)]
    fox_past_s = [(gather_pages(cache_fox_k[i], page_table), gather_pages(cache_fox_v[i], page_table),
                   gather_pages(cache_fox_logf[i], page_table)) for i in range(N_EVEN)]
    sb_past_s = [(gather_pages(cache_sb_k[i], page_table), gather_pages(cache_sb_v[i], page_table))
                 for i in range(N_ODD)]
    pool_prefix_s = [state_pool[i] for i in range(N_EVEN)]
    conv_prefix_s = [state_conv[i] for i in range(N_ODD)]
    y_sample, ss = decoder_stack(x_sample, mem_kv_s, fox_past_s, sb_past_s, pool_prefix_s, conv_prefix_s, W)

    return (y_prompt, y_sample,
            jnp.stack(sp['fox_k']), jnp.stack(sp['fox_v']), jnp.stack(sp['fox_logf']),
            jnp.stack(sp['sb_k']), jnp.stack(sp['sb_v']),
            jnp.stack(sp['pool']), jnp.stack(sp['conv']),
            jnp.stack([kv[0] for kv in mem_kv_p]), jnp.stack([kv[1] for kv in mem_kv_p]),
            jnp.stack(ss['fox_k']), jnp.stack(ss['fox_v']), jnp.stack(ss['fox_logf']),
            jnp.stack(ss['sb_k']), jnp.stack(ss['sb_v']),
            jnp.stack(ss['pool']), jnp.stack(ss['conv']))
```

```python
import functools

import jax
import jax.numpy as jnp
from jax import lax
from jax.experimental import pallas as pl
from jax.experimental.pallas import tpu as pltpu

F32 = jnp.float32
BF16 = jnp.bfloat16

D_MODEL = 1024
MIX_HALF = 512
HEAD_DIM = 64
N_HEADS = 8
POOL_WINDOWS = (2, 4, 8, 16)
POOL_GROUP = 128
POOL_BUF = 15
CONV_WIDTH = 31
CONV_BUF = 30
N_MEM = 256
X_HEADS = 4
X_HEAD_DIM = 128
X_WIDTH = 512
N_EXPERTS = 8
PAGE = 128
EPS = 1e-6
ATT_SCALE = HEAD_DIM ** -0.5
NEG_BIG = -1e30

LANES = 128
SUBLANES = 8
MIB = 1024 * 1024


def _params(sem, vmem_mib):
    return pltpu.CompilerParams(dimension_semantics=sem, vmem_limit_bytes=vmem_mib * MIB)


def _full(shape):
    return pl.BlockSpec(shape, lambda *_: (0,) * len(shape))


def _rms(x, g):
    return x * lax.rsqrt(jnp.mean(x * x, axis=-1, keepdims=True) + EPS) * g


def _split2(x):
    hi = x.astype(BF16)
    lo = (x - hi.astype(F32)).astype(BF16)
    return hi, lo


def _split3(x):
    hi = x.astype(BF16)
    r = x - hi.astype(F32)
    mid = r.astype(BF16)
    lo = (r - mid.astype(F32)).astype(BF16)
    return hi, mid, lo


def _dot(a, b):
    return jnp.dot(a, b, preferred_element_type=F32)


def _dot_nt(a, b):
    return lax.dot_general(a, b, (((1,), (1,)), ((), ())), preferred_element_type=F32)


def _softplus(z):
    return jnp.maximum(z, 0.0) + jnp.log1p(jnp.exp(-jnp.abs(z)))


def _head_rms(a, g, bd, head_dim):
    hi, lo = _split2(a * a)
    ssq = _dot(hi, bd) + _dot(lo, bd)
    return a * lax.rsqrt(ssq * (1.0 / head_dim) + EPS) * g


def _in_even_kernel(x_ref, g_ref, w_ref, wf_ref, wft_ref, bf_ref, bft_ref, gq_ref, gk_ref, bd_ref,
                    p_ref, q_ref, k32_ref, kb_ref, v32_ref, vb_ref, lf_ref, lft_ref):
    h = _rms(x_ref[...], g_ref[...]).astype(BF16)
    bd = bd_ref[...]
    p_ref[...] = _dot(h, w_ref[:, 0:MIX_HALF])
    q = _head_rms(_dot(h, w_ref[:, MIX_HALF:2 * MIX_HALF]), gq_ref[...], bd, HEAD_DIM)
    q_ref[...] = q
    k = _head_rms(_dot(h, w_ref[:, 2 * MIX_HALF:3 * MIX_HALF]), gk_ref[...], bd, HEAD_DIM)
    k32_ref[...] = k
    kb_ref[...] = k.astype(BF16)
    v = _dot(h, w_ref[:, 3 * MIX_HALF:4 * MIX_HALF])
    v32_ref[...] = v
    vb_ref[...] = v.astype(BF16)
    fl = _dot(h, wf_ref[...]) + bf_ref[...]
    lf_ref[...] = (-_softplus(-fl))[:, 0:N_HEADS]
    flt = _dot_nt(wft_ref[...], h) + bft_ref[...]
    lft_ref[...] = -_softplus(-flt)


def _in_even(x, g, w, wf, wft, bf, bft, gq, gk, bd, tm):
    T = x.shape[0]
    row = lambda n: pl.BlockSpec((tm, n), lambda i: (i, 0))
    sds = jax.ShapeDtypeStruct
    return pl.pallas_call(
        _in_even_kernel,
        grid=(T // tm,),
        in_specs=[row(D_MODEL), _full((1, D_MODEL)), _full((D_MODEL, 4 * MIX_HALF)), _full((D_MODEL, LANES)),
                  _full((N_HEADS, D_MODEL)), _full((1, LANES)), _full((N_HEADS, 1)), _full((1, MIX_HALF)),
                  _full((1, MIX_HALF)), _full((MIX_HALF, MIX_HALF))],
        out_specs=[row(MIX_HALF), row(MIX_HALF), row(MIX_HALF), row(MIX_HALF), row(MIX_HALF), row(MIX_HALF),
                   row(N_HEADS), pl.BlockSpec((N_HEADS, tm), lambda i: (0, i))],
        out_shape=[sds((T, MIX_HALF), F32), sds((T, MIX_HALF), F32), sds((T, MIX_HALF), F32),
                   sds((T, MIX_HALF), BF16), sds((T, MIX_HALF), F32), sds((T, MIX_HALF), BF16),
                   sds((T, N_HEADS), F32), sds((N_HEADS, T), F32)],
        compiler_params=_params(("parallel",), 48),
        name="in_even",
    )(x, g, w, wf, wft, bf, bft, gq, gk, bd)


def _in_odd_kernel(x_ref, g_ref, w_ref, u_ref, q_ref, k32_ref, kb_ref, v32_ref, vb_ref):
    h = _rms(x_ref[...], g_ref[...]).astype(BF16)
    a = _dot(h, w_ref[:, 0:MIX_HALF])
    gate = _dot(h, w_ref[:, MIX_HALF:2 * MIX_HALF])
    u_ref[...] = a * (1.0 / (1.0 + jnp.exp(-gate)))
    q_ref[...] = _dot(h, w_ref[:, 2 * MIX_HALF:3 * MIX_HALF])
    k = _dot(h, w_ref[:, 3 * MIX_HALF:4 * MIX_HALF])
    k32_ref[...] = k
    kb_ref[...] = k.astype(BF16)
    v = _dot(h, w_ref[:, 4 * MIX_HALF:5 * MIX_HALF])
    v32_ref[...] = v
    vb_ref[...] = v.astype(BF16)


def _in_odd(x, g, w, tm):
    T = x.shape[0]
    row = lambda n: pl.BlockSpec((tm, n), lambda i: (i, 0))
    sds = jax.ShapeDtypeStruct
    return pl.pallas_call(
        _in_odd_kernel,
        grid=(T // tm,),
        in_specs=[row(D_MODEL), _full((1, D_MODEL)), _full((D_MODEL, 5 * MIX_HALF))],
        out_specs=[row(MIX_HALF)] * 6,
        out_shape=[sds((T, MIX_HALF), F32), sds((T, MIX_HALF), F32), sds((T, MIX_HALF), F32),
                   sds((T, MIX_HALF), BF16), sds((T, MIX_HALF), F32), sds((T, MIX_HALF), BF16)],
        compiler_params=_params(("parallel",), 48),
        name="in_odd",
    )(x, g, w)


POOL_PAD = 16
CONV_PAD = 32


def _pool_kernel(pre_ref, p_ref, w_ref, sc_ref, y_ref, st_ref, buf, *, ns, L, ch, start_pos):
    for s in range(ns):
        buf[0:POOL_PAD - POOL_BUF, :] = jnp.zeros((POOL_PAD - POOL_BUF, MIX_HALF), F32)
        buf[POOL_PAD - POOL_BUF:POOL_PAD, :] = pre_ref[s]
        buf[POOL_PAD:POOL_PAD + L, :] = p_ref[s]
        st_ref[s] = buf[POOL_PAD + L - POOL_BUF:POOL_PAD + L, :]
        for c0 in range(0, L, ch):
            pos = start_pos + c0 + lax.broadcasted_iota(jnp.int32, (ch, POOL_GROUP), 0)
            outs = []
            for gi, w in enumerate(POOL_WINDOWS):
                ls = slice(gi * POOL_GROUP, (gi + 1) * POOL_GROUP)
                cur = buf[POOL_PAD + c0:POOL_PAD + c0 + ch, ls]
                acc = cur
                for i in range(1, w):
                    acc = acc + buf[POOL_PAD + c0 - i:POOL_PAD + c0 - i + ch, ls]
                cnt = jnp.minimum(pos + 1, w).astype(F32)
                d = acc / cnt - cur
                outs.append(_dot(d.astype(BF16), w_ref[gi]) * sc_ref[:, ls])
            y_ref[s, c0:c0 + ch, :] = jnp.concatenate(outs, axis=-1)


def _pool(prefix, p, w_grp, scale, ns, ch, start_pos):
    B, L, _ = p.shape
    kern = functools.partial(_pool_kernel, ns=ns, L=L, ch=ch, start_pos=start_pos)
    return pl.pallas_call(
        kern,
        grid=(B // ns,),
        in_specs=[pl.BlockSpec((ns, POOL_BUF, MIX_HALF), lambda i: (i, 0, 0)),
                  pl.BlockSpec((ns, L, MIX_HALF), lambda i: (i, 0, 0)),
                  _full((len(POOL_WINDOWS), POOL_GROUP, POOL_GROUP)), _full((1, MIX_HALF))],
        out_specs=[pl.BlockSpec((ns, L, MIX_HALF), lambda i: (i, 0, 0)),
                   pl.BlockSpec((ns, POOL_BUF, MIX_HALF), lambda i: (i, 0, 0))],
        out_shape=[jax.ShapeDtypeStruct((B, L, MIX_HALF), F32),
                   jax.ShapeDtypeStruct((B, POOL_BUF, MIX_HALF), F32)],
        scratch_shapes=[pltpu.VMEM((POOL_PAD + L, MIX_HALF), F32)],
        compiler_params=_params(("parallel",), 40),
        name="pool_mixer",
    )(prefix, p, w_grp, scale)


def _conv_kernel(pre_ref, u_ref, w_ref, b_ref, lg_ref, lb_ref, y_ref, st_ref, buf, *, ns, L, ch):
    for s in range(ns):
        buf[0:CONV_PAD - CONV_BUF, :] = jnp.zeros((CONV_PAD - CONV_BUF, MIX_HALF), F32)
        buf[CONV_PAD - CONV_BUF:CONV_PAD, :] = pre_ref[s]
        buf[CONV_PAD:CONV_PAD + L, :] = u_ref[s]
        st_ref[s] = buf[CONV_PAD + L - CONV_BUF:CONV_PAD + L, :]
        base = CONV_PAD - CONV_BUF
        for c0 in range(0, L, ch):
            acc = buf[base + c0:base + c0 + ch, :] * w_ref[0:1, :]
            for j in range(1, CONV_WIDTH):
                acc = acc + buf[base + c0 + j:base + c0 + j + ch, :] * w_ref[j:j + 1, :]
            y = acc + b_ref[...]
            mu = jnp.mean(y, axis=-1, keepdims=True)
            yc = y - mu
            var = jnp.mean(yc * yc, axis=-1, keepdims=True)
            yn = yc * lax.rsqrt(var + EPS) * lg_ref[...] + lb_ref[...]
            y_ref[s, c0:c0 + ch, :] = yn * (1.0 / (1.0 + jnp.exp(-yn)))


def _conv(prefix, u, conv_w, conv_b, ln_g, ln_b, ns, ch):
    B, L, _ = u.shape
    kern = functools.partial(_conv_kernel, ns=ns, L=L, ch=ch)
    return pl.pallas_call(
        kern,
        grid=(B // ns,),
        in_specs=[pl.BlockSpec((ns, CONV_BUF, MIX_HALF), lambda i: (i, 0, 0)),
                  pl.BlockSpec((ns, L, MIX_HALF), lambda i: (i, 0, 0)),
                  _full((CONV_WIDTH, MIX_HALF)), _full((1, MIX_HALF)), _full((1, MIX_HALF)), _full((1, MIX_HALF))],
        out_specs=[pl.BlockSpec((ns, L, MIX_HALF), lambda i: (i, 0, 0)),
                   pl.BlockSpec((ns, CONV_BUF, MIX_HALF), lambda i: (i, 0, 0))],
        out_shape=[jax.ShapeDtypeStruct((B, L, MIX_HALF), F32),
                   jax.ShapeDtypeStruct((B, CONV_BUF, MIX_HALF), F32)],
        scratch_shapes=[pltpu.VMEM((CONV_PAD + L, MIX_HALF), F32)],
        compiler_params=_params(("parallel",), 40),
        name="conv_module",
    )(prefix, u, conv_w, conv_b, ln_g, ln_b)


def _cumsum_prompt_kernel(lf_ref, lft_ref, tril_ref, triu_ref, f_ref, ft_ref, *, L, blk):
    c_col = jnp.zeros((1, N_HEADS), F32)
    c_row = jnp.zeros((N_HEADS, 1), F32)
    for b0 in range(0, L, blk):
        x = lf_ref[b0:b0 + blk, :]
        xt = lft_ref[:, b0:b0 + blk]
        f = c_col
        for part in _split3(x):
            f = f + _dot(tril_ref[...], part)
        ft = c_row
        for part in _split3(xt):
            ft = ft + _dot(part, triu_ref[...])
        f_ref[b0:b0 + blk, :] = f
        ft_ref[:, b0:b0 + blk] = ft
        c_col = f[blk - 1:blk, :]
        c_row = ft[:, blk - 1:blk]


def _cumsum_prompt(lf, lft, tril, triu, B, L, blk):
    kern = functools.partial(_cumsum_prompt_kernel, L=L, blk=blk)
    return pl.pallas_call(
        kern,
        grid=(B,),
        in_specs=[pl.BlockSpec((L, N_HEADS), lambda b: (b, 0)), pl.BlockSpec((N_HEADS, L), lambda b: (0, b)),
                  _full((blk, blk)), _full((blk, blk))],
        out_specs=[pl.BlockSpec((L, N_HEADS), lambda b: (b, 0)), pl.BlockSpec((N_HEADS, L), lambda b: (0, b))],
        out_shape=[jax.ShapeDtypeStruct((B * L, N_HEADS), F32), jax.ShapeDtypeStruct((N_HEADS, B * L), F32)],
        compiler_params=_params(("parallel",), 32),
        name="fox_cumsum_prompt",
    )(lf, lft, tril, triu)


def _head_pair(q):
    lane = lax.broadcasted_iota(jnp.int32, q.shape, 1)
    zero = jnp.zeros_like(q)
    scale = jnp.asarray(ATT_SCALE, q.dtype)
    return [jnp.where(lane < HEAD_DIM, q, zero) * scale, jnp.where(lane >= HEAD_DIM, q, zero) * scale]


def _fox_prompt_kernel(q_ref, k_ref, v_ref, f_ref, ft_ref, o_ref, *, blk):
    hp = pl.program_id(1)
    qi = pl.program_id(2)
    qh = _head_pair(q_ref[...])
    f = f_ref[...]
    lane8 = lax.broadcasted_iota(jnp.int32, f.shape, 1)
    fq = [jnp.sum(jnp.where(lane8 == 2 * hp + i, f, 0.0), axis=1, keepdims=True) for i in range(2)]
    row = lax.broadcasted_iota(jnp.int32, (blk, blk), 0)
    col = lax.broadcasted_iota(jnp.int32, (blk, blk), 1)

    def block(ki, carry, diagonal):
        ks = pl.multiple_of(ki * blk, blk)
        k = k_ref[pl.ds(ks, blk), :]
        v = v_ref[pl.ds(ks, blk), :]
        out = []
        for i in range(2):
            m, l, acc = carry[i]
            s = _dot_nt(qh[i], k) + (fq[i] - ft_ref[0, i:i + 1, pl.ds(ks, blk)])
            if diagonal:
                s = jnp.where(col <= row, s, -jnp.inf)
            m_new = jnp.maximum(m, jnp.max(s, axis=1, keepdims=True))
            alpha = jnp.exp(m - m_new)
            p = jnp.exp(s - m_new)
            l = alpha * l + jnp.sum(p, axis=1, keepdims=True)
            acc = alpha * acc + _dot(p.astype(BF16), v)
            out.append((m_new, l, acc))
        return tuple(out)

    init = tuple((jnp.full((blk, 1), NEG_BIG, F32), jnp.zeros((blk, 1), F32), jnp.zeros((blk, LANES), F32))
                 for _ in range(2))
    carry = lax.fori_loop(0, qi, lambda ki, c: block(ki, c, False), init)
    carry = block(qi, carry, True)
    lane = lax.broadcasted_iota(jnp.int32, (blk, LANES), 1)
    o_ref[...] = jnp.where(lane < HEAD_DIM, carry[0][2] / carry[0][1], carry[1][2] / carry[1][1])


def _fox_prompt(q, k, v, f, ft, B, L, blk):
    nq = L // blk
    kern = functools.partial(_fox_prompt_kernel, blk=blk)
    return pl.pallas_call(
        kern,
        grid=(B, N_HEADS // 2, nq),
        in_specs=[pl.BlockSpec((blk, LANES), lambda b, h, i: (b * nq + i, h)),
                  pl.BlockSpec((L, LANES), lambda b, h, i: (b, h)),
                  pl.BlockSpec((L, LANES), lambda b, h, i: (b, h)),
                  pl.BlockSpec((blk, N_HEADS), lambda b, h, i: (b * nq + i, 0)),
                  pl.BlockSpec((1, 2, L), lambda b, h, i: (h, 0, b))],
        out_specs=pl.BlockSpec((blk, LANES), lambda b, h, i: (b * nq + i, h)),
        out_shape=jax.ShapeDtypeStruct((B * L, MIX_HALF), F32),
        compiler_params=_params(("parallel", "parallel", "arbitrary"), 32),
        name="fox_attention_prompt",
    )(q, k, v, f, ft)


def _sb_block(z, c, v, u, mask):
    sp = _softplus(z)
    ln = -sp
    if mask is not None:
        ln = jnp.where(mask, ln, 0.0)
    hi, lo = _split2(ln)
    e = _dot(hi, u) + _dot(lo, u)
    a = jnp.exp((z - sp) + e + c)
    if mask is not None:
        a = jnp.where(mask, a, 0.0)
    return _dot(a.astype(BF16), v), c + jnp.sum(ln, axis=1, keepdims=True)


def _sb_prompt_kernel(q_ref, k_ref, v_ref, u_ref, o_ref, *, blk):
    qi = pl.program_id(2)
    qh = _head_pair(q_ref[...])
    row = lax.broadcasted_iota(jnp.int32, (blk, blk), 0)
    col = lax.broadcasted_iota(jnp.int32, (blk, blk), 1)
    u = u_ref[...]

    def block(ki, carry, diagonal):
        ks = pl.multiple_of(ki * blk, blk)
        k = k_ref[pl.ds(ks, blk), :]
        v = v_ref[pl.ds(ks, blk), :]
        out = []
        for i in range(2):
            c, acc = carry[i]
            pv, c = _sb_block(_dot_nt(qh[i], k), c, v, u, (col < row) if diagonal else None)
            out.append((c, acc + pv))
        return tuple(out)

    init = tuple((jnp.zeros((blk, 1), F32), jnp.zeros((blk, LANES), F32)) for _ in range(2))
    carry = block(qi, init, True)
    carry = lax.fori_loop(0, qi, lambda j, c: block(qi - 1 - j, c, False), carry)
    lane = lax.broadcasted_iota(jnp.int32, (blk, LANES), 1)
    o_ref[...] = jnp.where(lane < HEAD_DIM, carry[0][1], carry[1][1])


def _sb_prompt(q, k, v, u, B, L, blk):
    nq = L // blk
    kern = functools.partial(_sb_prompt_kernel, blk=blk)
    return pl.pallas_call(
        kern,
        grid=(B, N_HEADS // 2, nq),
        in_specs=[pl.BlockSpec((blk, LANES), lambda b, h, i: (b * nq + i, h)),
                  pl.BlockSpec((L, LANES), lambda b, h, i: (b, h)),
                  pl.BlockSpec((L, LANES), lambda b, h, i: (b, h)),
                  _full((blk, blk))],
        out_specs=pl.BlockSpec((blk, LANES), lambda b, h, i: (b * nq + i, h)),
        out_shape=jax.ShapeDtypeStruct((B * L, MIX_HALF), F32),
        compiler_params=_params(("parallel", "parallel", "arbitrary"), 32),
        name="sb_attention_prompt",
    )(q, k, v, u)


N_PAGES = 16
DEC_SEQ = 8
BD_ROWS = DEC_SEQ * N_HEADS


def _block_diag_q(q, hm):
    rows = [jnp.broadcast_to(q[t:t + 1, :], (N_HEADS, MIX_HALF)) * hm for t in range(DEC_SEQ)]
    return (jnp.concatenate(rows, axis=0) * ATT_SCALE).astype(BF16)


def _pad_new(x):
    return jnp.concatenate([x, jnp.zeros((PAGE - DEC_SEQ, x.shape[1]), x.dtype)], axis=0)


def _merge_heads(o, hm):
    return jnp.concatenate(
        [jnp.sum(o[t * N_HEADS:(t + 1) * N_HEADS, :] * hm, axis=0, keepdims=True) for t in range(DEC_SEQ)], axis=0)


def _new_key_mask(strict):
    row = lax.broadcasted_iota(jnp.int32, (BD_ROWS, PAGE), 0) // N_HEADS
    col = lax.broadcasted_iota(jnp.int32, (BD_ROWS, PAGE), 1)
    return (col < row) if strict else (col <= row)


def _cumsum_lanes(x):
    n = x.shape[1]
    lane = lax.broadcasted_iota(jnp.int32, x.shape, 1)
    k = 1
    while k < n:
        x = x + jnp.where(lane >= k, pltpu.roll(x, k, 1), 0.0)
        k *= 2
    return x


def _fox_decode_kernel(pt_ref, *refs):
    k_pages = refs[0:N_PAGES]
    v_pages = refs[N_PAGES:2 * N_PAGES]
    lf_pages = refs[2 * N_PAGES:3 * N_PAGES]
    q_ref, kn_ref, vn_ref, lftn_ref, hm_ref, eye_ref, o_ref = refs[3 * N_PAGES:]
    del pt_ref
    hm = hm_ref[...]
    qbd = _block_diag_q(q_ref[...], hm)
    parts = []
    for j in range(N_PAGES):
        x = lf_pages[j][0]
        acc = jnp.zeros((N_HEADS, PAGE), F32)
        for part in _split3(x):
            acc = acc + _dot_nt(eye_ref[...], part)
        parts.append(acc)
    parts.append(jnp.concatenate([lftn_ref[0], jnp.zeros((N_HEADS, PAGE - DEC_SEQ), F32)], axis=1))
    fk = _cumsum_lanes(jnp.concatenate(parts, axis=1))
    fnew = fk[:, N_PAGES * PAGE:]
    lane = lax.broadcasted_iota(jnp.int32, (N_HEADS, PAGE), 1)
    fq = jnp.concatenate([jnp.sum(jnp.where(lane == t, fnew, 0.0), axis=1, keepdims=True)
                          for t in range(DEC_SEQ)], axis=0)
    scores = []
    for j in range(N_PAGES + 1):
        k = k_pages[j][0].astype(BF16) if j < N_PAGES else _pad_new(kn_ref[...]).astype(BF16)
        bias = jnp.concatenate([fk[:, j * PAGE:(j + 1) * PAGE]] * DEC_SEQ, axis=0)
        s = _dot_nt(qbd, k) + (fq - bias)
        if j == N_PAGES:
            s = jnp.where(_new_key_mask(False), s, -jnp.inf)
        scores.append(s)
    m = scores[0].max(axis=1, keepdims=True)
    for s in scores[1:]:
        m = jnp.maximum(m, s.max(axis=1, keepdims=True))
    l = jnp.zeros((BD_ROWS, 1), F32)
    acc = jnp.zeros((BD_ROWS, MIX_HALF), F32)
    for j in range(N_PAGES + 1):
        p = jnp.exp(scores[j] - m)
        l = l + jnp.sum(p, axis=1, keepdims=True)
        v = v_pages[j][0].astype(BF16) if j < N_PAGES else _pad_new(vn_ref[...]).astype(BF16)
        acc = acc + _dot(p.astype(BF16), v)
    o_ref[...] = _merge_heads(acc / l, hm)


def _sb_decode_kernel(pt_ref, *refs):
    k_pages = refs[0:N_PAGES]
    v_pages = refs[N_PAGES:2 * N_PAGES]
    q_ref, kn_ref, vn_ref, hm_ref, u_ref, o_ref = refs[2 * N_PAGES:]
    del pt_ref
    hm = hm_ref[...]
    qbd = _block_diag_q(q_ref[...], hm)
    u = u_ref[...]
    c = jnp.zeros((BD_ROWS, 1), F32)
    acc = jnp.zeros((BD_ROWS, MIX_HALF), F32)
    for j in range(N_PAGES, -1, -1):
        if j == N_PAGES:
            k = _pad_new(kn_ref[...]).astype(BF16)
            v = _pad_new(vn_ref[...]).astype(BF16)
            mask = _new_key_mask(True)
        else:
            k = k_pages[j][0].astype(BF16)
            v = v_pages[j][0].astype(BF16)
            mask = None
        pv, c = _sb_block(_dot_nt(qbd, k), c, v, u, mask)
        acc = acc + pv
    o_ref[...] = _merge_heads(acc, hm)


def _page_specs(shape_tail, n):
    specs = []
    for j in range(n):
        specs.append(pl.BlockSpec((1,) + shape_tail, functools.partial(
            lambda b, pt, j: (pt[b, j],) + (0,) * len(shape_tail), j=j)))
    return specs


def _fox_decode(page_table, ck, cv, clf, q, kn, vn, lftn, hm, eye):
    nb = page_table.shape[0]
    seq = lambda n: pl.BlockSpec((DEC_SEQ, n), lambda b, pt: (b, 0))
    const = lambda shape: pl.BlockSpec(shape, lambda b, pt: (0,) * len(shape))
    in_specs = (_page_specs((PAGE, MIX_HALF), N_PAGES) + _page_specs((PAGE, MIX_HALF), N_PAGES)
                + _page_specs((PAGE, N_HEADS), N_PAGES)
                + [seq(MIX_HALF), seq(MIX_HALF), seq(MIX_HALF),
                   pl.BlockSpec((1, N_HEADS, DEC_SEQ), lambda b, pt: (b, 0, 0)),
                   const((N_HEADS, MIX_HALF)), const((N_HEADS, N_HEADS))])
    return pl.pallas_call(
        _fox_decode_kernel,
        grid_spec=pltpu.PrefetchScalarGridSpec(
            num_scalar_prefetch=1, grid=(nb,), in_specs=in_specs, out_specs=seq(MIX_HALF)),
        out_shape=jax.ShapeDtypeStruct((nb * DEC_SEQ, MIX_HALF), F32),
        compiler_params=_params(("arbitrary",), 48),
        name="fox_attention_decode",
    )(page_table, *([ck] * N_PAGES), *([cv] * N_PAGES), *([clf] * N_PAGES), q, kn, vn, lftn, hm, eye)


def _sb_decode(page_table, ck, cv, q, kn, vn, hm, u):
    nb = page_table.shape[0]
    seq = lambda n: pl.BlockSpec((DEC_SEQ, n), lambda b, pt: (b, 0))
    const = lambda shape: pl.BlockSpec(shape, lambda b, pt: (0,) * len(shape))
    in_specs = (_page_specs((PAGE, MIX_HALF), N_PAGES) + _page_specs((PAGE, MIX_HALF), N_PAGES)
                + [seq(MIX_HALF), seq(MIX_HALF), seq(MIX_HALF), const((N_HEADS, MIX_HALF)), const((PAGE, PAGE))])
    return pl.pallas_call(
        _sb_decode_kernel,
        grid_spec=pltpu.PrefetchScalarGridSpec(
            num_scalar_prefetch=1, grid=(nb,), in_specs=in_specs, out_specs=seq(MIX_HALF)),
        out_shape=jax.ShapeDtypeStruct((nb * DEC_SEQ, MIX_HALF), F32),
        compiler_params=_params(("arbitrary",), 48),
        name="sb_attention_decode",
    )(page_table, *([ck] * N_PAGES), *([cv] * N_PAGES), q, kn, vn, hm, u)


def _x_head_rms(a, g):
    outs = []
    for h in range(X_HEADS):
        ah = a[:, h * X_HEAD_DIM:(h + 1) * X_HEAD_DIM]
        outs.append(ah * lax.rsqrt(jnp.mean(ah * ah, axis=-1, keepdims=True) + EPS))
    return jnp.concatenate(outs, axis=-1) * g


def _mix_out_kernel(x_ref, ya_ref, yb_ref, wo_ref, g_ref, wq_ref, gq_ref, x1_ref, qc_ref):
    y = _dot(ya_ref[...].astype(BF16), wo_ref[0:MIX_HALF, :]) + _dot(yb_ref[...].astype(BF16), wo_ref[MIX_HALF:, :])
    x1 = x_ref[...] + y
    x1_ref[...] = x1
    hc = _rms(x1, g_ref[...]).astype(BF16)
    qc_ref[...] = _x_head_rms(_dot(hc, wq_ref[...]), gq_ref[...])


def _mix_out(x, ya, yb, wo, g, wq, gq, tm):
    T = x.shape[0]
    row = lambda n: pl.BlockSpec((tm, n), lambda i: (i, 0))
    return pl.pallas_call(
        _mix_out_kernel,
        grid=(T // tm,),
        in_specs=[row(D_MODEL), row(MIX_HALF), row(MIX_HALF), _full((D_MODEL, D_MODEL)), _full((1, D_MODEL)),
                  _full((D_MODEL, X_WIDTH)), _full((1, X_WIDTH))],
        out_specs=[row(D_MODEL), row(X_WIDTH)],
        out_shape=[jax.ShapeDtypeStruct((T, D_MODEL), F32), jax.ShapeDtypeStruct((T, X_WIDTH), F32)],
        compiler_params=_params(("parallel",), 40),
        name="mixer_out_proj",
    )(x, ya, yb, wo, g, wq, gq)


def _mem_kv_kernel(m_ref, wk_ref, wv_ref, gk_ref, k_ref, v_ref):
    m = m_ref[...].astype(BF16)
    k_ref[0] = _x_head_rms(_dot(m, wk_ref[0]), gk_ref[0])
    v_ref[0] = _dot(m, wv_ref[0])


def _mem_kv(mem, wk, wv, gk, tm):
    T = mem.shape[0]
    depth = wk.shape[0]
    return pl.pallas_call(
        _mem_kv_kernel,
        grid=(depth, T // tm),
        in_specs=[pl.BlockSpec((tm, D_MODEL), lambda l, i: (i, 0)),
                  pl.BlockSpec((1, D_MODEL, X_WIDTH), lambda l, i: (l, 0, 0)),
                  pl.BlockSpec((1, D_MODEL, X_WIDTH), lambda l, i: (l, 0, 0)),
                  pl.BlockSpec((1, 1, X_WIDTH), lambda l, i: (l, 0, 0))],
        out_specs=[pl.BlockSpec((1, tm, X_WIDTH), lambda l, i: (l, i, 0))] * 2,
        out_shape=[jax.ShapeDtypeStruct((depth, T, X_WIDTH), F32)] * 2,
        compiler_params=_params(("parallel", "parallel"), 32),
        name="memory_kv",
    )(mem, wk, wv, gk)


def _cross_kernel(x_ref, q_ref, mk_ref, mv_ref, wo_ref, g_ref, x2_ref, hf_ref, *, ns, tq):
    outs = []
    for s in range(ns):
        q = q_ref[s * tq:(s + 1) * tq, :]
        heads = []
        for h in range(X_HEADS):
            ls = slice(h * X_HEAD_DIM, (h + 1) * X_HEAD_DIM)
            qh = (q[:, ls] * (X_HEAD_DIM ** -0.5)).astype(BF16)
            sc = _dot_nt(qh, mk_ref[s, :, ls].astype(BF16))
            p = jnp.exp(sc - jnp.max(sc, axis=1, keepdims=True))
            o = _dot(p.astype(BF16), mv_ref[s, :, ls].astype(BF16))
            heads.append(o / jnp.sum(p, axis=1, keepdims=True))
        outs.append(jnp.concatenate(heads, axis=-1))
    o = outs[0] if ns == 1 else jnp.concatenate(outs, axis=0)
    x2 = x_ref[...] + _dot(o.astype(BF16), wo_ref[...])
    x2_ref[...] = x2
    hf_ref[...] = _rms(x2, g_ref[...])


def _cross(x, q, mk, mv, wo, g, ns, tq):
    T = x.shape[0]
    L = T // mk.shape[0]
    nq = L // tq
    rows = ns * tq
    row = lambda n: pl.BlockSpec((rows, n), lambda i: (i, 0))
    mem = pl.BlockSpec((ns, N_MEM, X_WIDTH), lambda i: (i // nq, 0, 0))
    kern = functools.partial(_cross_kernel, ns=ns, tq=tq)
    return pl.pallas_call(
        kern,
        grid=(T // rows,),
        in_specs=[row(D_MODEL), row(X_WIDTH), mem, mem, _full((X_WIDTH, D_MODEL)), _full((1, D_MODEL))],
        out_specs=[row(D_MODEL), row(D_MODEL)],
        out_shape=[jax.ShapeDtypeStruct((T, D_MODEL), F32)] * 2,
        compiler_params=_params(("parallel",), 48),
        name="cross_attention",
    )(x, q, mk, mv, wo, g)


def _swiglu_kernel(h_ref, x_ref, wg_ref, wu_ref, wd_ref, o_ref, hb, acc):
    f = pl.program_id(1)

    @pl.when(f == 0)
    def _():
        hb[...] = h_ref[...].astype(BF16)
        acc[...] = jnp.zeros_like(acc)

    h = hb[...]
    gte = _dot(h, wg_ref[...])
    act = gte * (1.0 / (1.0 + jnp.exp(-gte))) * _dot(h, wu_ref[...])
    acc[...] += _dot(act.astype(BF16), wd_ref[...])

    @pl.when(f == pl.num_programs(1) - 1)
    def _():
        o_ref[...] = x_ref[...] + acc[...]


def _swiglu(h, x, wg, wu, wd, tm, tf):
    T = h.shape[0]
    dff = wg.shape[1]
    row = pl.BlockSpec((tm, D_MODEL), lambda i, f: (i, 0))
    return pl.pallas_call(
        _swiglu_kernel,
        grid=(T // tm, dff // tf),
        in_specs=[row, row, pl.BlockSpec((D_MODEL, tf), lambda i, f: (0, f)),
                  pl.BlockSpec((D_MODEL, tf), lambda i, f: (0, f)), pl.BlockSpec((tf, D_MODEL), lambda i, f: (f, 0))],
        out_specs=row,
        out_shape=jax.ShapeDtypeStruct((T, D_MODEL), F32),
        scratch_shapes=[pltpu.VMEM((tm, D_MODEL), BF16), pltpu.VMEM((tm, D_MODEL), F32)],
        compiler_params=_params(("parallel", "arbitrary"), 56),
        name="dense_swiglu",
    )(h, x, wg, wu, wd)


def _router_kernel(h_ref, w_ref, b_ref, c_ref):
    h_hi, h_lo = _split2(h_ref[...])
    w_hi, w_lo = _split2(w_ref[...])
    logits = _dot(h_hi, w_hi) + _dot(h_hi, w_lo) + _dot(h_lo, w_hi) + b_ref[...]
    lane = lax.broadcasted_iota(jnp.int32, logits.shape, 1)
    logits = jnp.where(lane < N_EXPERTS, logits, -jnp.inf)
    m1 = jnp.max(logits, axis=1, keepdims=True)
    i1 = jnp.min(jnp.where(logits == m1, lane, LANES), axis=1, keepdims=True)
    rest = jnp.where(lane == i1, -jnp.inf, logits)
    m2 = jnp.max(rest, axis=1, keepdims=True)
    i2 = jnp.min(jnp.where(rest == m2, lane, LANES), axis=1, keepdims=True)
    e2 = jnp.exp(m2 - m1)
    g1 = 1.0 / (1.0 + e2)
    g2 = e2 / (1.0 + e2)
    c_ref[...] = jnp.where(lane == i1, g1, 0.0) + jnp.where(lane == i2, g2, 0.0)


def _router(h, w, b, tm):
    T = h.shape[0]
    return pl.pallas_call(
        _router_kernel,
        grid=(T // tm,),
        in_specs=[pl.BlockSpec((tm, D_MODEL), lambda i: (i, 0)), _full((D_MODEL, LANES)), _full((1, LANES))],
        out_specs=pl.BlockSpec((tm, LANES), lambda i: (i, 0)),
        out_shape=jax.ShapeDtypeStruct((T, LANES), F32),
        compiler_params=_params(("parallel",), 32),
        name="moe_router",
    )(h, w, b)


def _moe_kernel(h_ref, x_ref, c_ref, wg_ref, wu_ref, wd_ref, o_ref, hb, acc_e, acc):
    e = pl.program_id(1)
    f = pl.program_id(2)
    last_f = pl.num_programs(2) - 1

    @pl.when((e == 0) & (f == 0))
    def _():
        hb[...] = h_ref[...].astype(BF16)
        acc[...] = jnp.zeros_like(acc)

    @pl.when(f == 0)
    def _():
        acc_e[...] = jnp.zeros_like(acc_e)

    h = hb[...]
    gte = _dot(h, wg_ref[0])
    act = gte * (1.0 / (1.0 + jnp.exp(-gte))) * _dot(h, wu_ref[0])
    acc_e[...] += _dot(act.astype(BF16), wd_ref[0])

    @pl.when(f == last_f)
    def _():
        c = c_ref[...]
        lane = lax.broadcasted_iota(jnp.int32, c.shape, 1)
        ce = jnp.sum(jnp.where(lane == e, c, 0.0), axis=1, keepdims=True)
        acc[...] += ce * acc_e[...]

    @pl.when((e == pl.num_programs(1) - 1) & (f == last_f))
    def _():
        o_ref[...] = x_ref[...] + acc[...]


def _moe(h, x, c, wg, wu, wd, tm, tf):
    T = h.shape[0]
    ne, _, dff = wg.shape
    row = pl.BlockSpec((tm, D_MODEL), lambda i, e, f: (i, 0))
    return pl.pallas_call(
        _moe_kernel,
        grid=(T // tm, ne, dff // tf),
        in_specs=[row, row, pl.BlockSpec((tm, LANES), lambda i, e, f: (i, 0)),
                  pl.BlockSpec((1, D_MODEL, tf), lambda i, e, f: (e, 0, f)),
                  pl.BlockSpec((1, D_MODEL, tf), lambda i, e, f: (e, 0, f)),
                  pl.BlockSpec((1, tf, D_MODEL), lambda i, e, f: (e, f, 0))],
        out_specs=row,
        out_shape=jax.ShapeDtypeStruct((T, D_MODEL), F32),
        scratch_shapes=[pltpu.VMEM((tm, D_MODEL), BF16), pltpu.VMEM((tm, D_MODEL), F32),
                        pltpu.VMEM((tm, D_MODEL), F32)],
        compiler_params=_params(("parallel", "arbitrary", "arbitrary"), 56),
        name="moe_swiglu",
    )(h, x, c, wg, wu, wd)


def _tri(n, kind):
    r = lax.broadcasted_iota(jnp.int32, (n, n), 0)
    c = lax.broadcasted_iota(jnp.int32, (n, n), 1)
    m = {"lower_incl": r >= c, "upper_incl": r <= c, "lower_strict": r > c}[kind]
    return m.astype(BF16)


def _head_mask(n_heads, head_dim):
    h = lax.broadcasted_iota(jnp.int32, (n_heads, n_heads * head_dim), 0)
    c = lax.broadcasted_iota(jnp.int32, (n_heads, n_heads * head_dim), 1) // head_dim
    return (h == c).astype(F32)


def _stack(x, mem_k, mem_v, W, group):
    T = x.shape[0]
    tm = 512
    B, L = group["B"], group["L"]
    prompt = group["kind"] == "prompt"
    new = {}

    p, q, k32, kb, v32, vb, lf, lft = _in_even(x, W["g_mix0"], W["w_in_e"], W["wf"], W["wft"], W["bf"], W["bft"],
                                               W["g_fq"], W["g_fk"], W["bd64"], tm)
    ya, pool_state = _pool(group["pool_prefix"], p.reshape(B, L, MIX_HALF), W["w_pool"], W["pool_scale"],
                           ns=group["seq_per_step"], ch=group["chunk"], start_pos=group["start_pos"])
    if prompt:
        f, ft = _cumsum_prompt(lf, lft, W["tril256"], W["triu256"], B, L, 256)
        yb = _fox_prompt(q.astype(BF16), kb, vb, f, ft.reshape(N_HEADS // 2, 2, T), B, L, 256)
    else:
        lftn = lft.reshape(N_HEADS, B, L).transpose(1, 0, 2)
        yb = _fox_decode(group["page_table"], group["fox_k"], group["fox_v"], group["fox_lf"], q, k32, v32, lftn,
                         W["hm64"], W["eye8"])
    new.update(fox_k=k32, fox_v=v32, fox_logf=lf, pool=pool_state)
    x, qc = _mix_out(x, ya.reshape(T, MIX_HALF), yb, W["w_out_e"], W["g_cross0"], W["w_cq0"], W["g_cq0"], tm)
    x, hf = _cross(x, qc, mem_k[0], mem_v[0], W["w_co0"], W["g_ffn0"], ns=group["x_ns"], tq=group["x_tq"])
    x = _swiglu(hf, x, W["w_gate"], W["w_up"], W["w_down"], tm, 1408)

    u, q, k32, kb, v32, vb = _in_odd(x, W["g_mix1"], W["w_in_o"], tm)
    yc, conv_state = _conv(group["conv_prefix"], u.reshape(B, L, MIX_HALF), W["conv_w"], W["conv_b"], W["ln_g"],
                           W["ln_b"], ns=group["seq_per_step"], ch=group["conv_chunk"])
    if prompt:
        yd = _sb_prompt(q.astype(BF16), kb, vb, W["ustrict256"], B, L, 256)
    else:
        yd = _sb_decode(group["page_table"], group["sb_k"], group["sb_v"], q, k32, v32, W["hm64"], W["ustrict128"])
    new.update(sb_k=k32, sb_v=v32, conv=conv_state)
    x, qc = _mix_out(x, yc.reshape(T, MIX_HALF), yd, W["w_out_o"], W["g_cross1"], W["w_cq1"], W["g_cq1"], tm)
    x, hf = _cross(x, qc, mem_k[1], mem_v[1], W["w_co1"], W["g_ffn1"], ns=group["x_ns"], tq=group["x_tq"])
    c = _router(hf, W["w_router"], W["b_router"], tm)
    x = _moe(hf, x, c, W["we_gate"], W["we_up"], W["we_down"], tm, 1792)
    return x, new


def kernel(x_prompt, x_sample, cache_fox_k, cache_fox_v, cache_fox_logf, cache_sb_k, cache_sb_v, cache_mem_k, cache_mem_v, state_pool, state_conv, page_table, mem_prompt, norm_mix, norm_cross, norm_ffn, w_cq, w_ck, w_cv, w_co, g_cq, g_ck, w_in_e, b_f, w_pool_grp, pool_scale, g_fq, g_fk, w_out_e, w_gate, w_up, w_down, w_in_o, conv_w, conv_b, ln_g, ln_b, w_out_o, w_router, b_router, we_gate, we_up, we_down):
    B, L, D = x_prompt.shape
    SB, SL, _ = x_sample.shape
    n_pool = cache_fox_k.shape[1]
    bf = lambda a: a.astype(BF16)
    row = lambda a: a.reshape(1, -1)

    wf = jnp.pad(w_in_e[0][:, 4 * MIX_HALF:], ((0, 0), (0, LANES - N_HEADS)))
    W = dict(
        g_mix0=row(norm_mix[0]), g_mix1=row(norm_mix[1]), g_cross0=row(norm_cross[0]), g_cross1=row(norm_cross[1]),
        g_ffn0=row(norm_ffn[0]), g_ffn1=row(norm_ffn[1]),
        w_in_e=bf(w_in_e[0][:, :4 * MIX_HALF]), wf=bf(wf), wft=bf(w_in_e[0][:, 4 * MIX_HALF:].T),
        bf=jnp.pad(row(b_f[0]), ((0, 0), (0, LANES - N_HEADS))), bft=b_f[0].reshape(N_HEADS, 1),
        g_fq=row(jnp.tile(g_fq[0], N_HEADS)), g_fk=row(jnp.tile(g_fk[0], N_HEADS)),
        w_pool=bf(w_pool_grp[0]), pool_scale=row(pool_scale[0]), w_out_e=bf(w_out_e[0]),
        w_cq0=bf(w_cq[0]), w_cq1=bf(w_cq[1]), w_co0=bf(w_co[0]), w_co1=bf(w_co[1]),
        g_cq0=row(jnp.tile(g_cq[0], X_HEADS)), g_cq1=row(jnp.tile(g_cq[1], X_HEADS)),
        w_gate=bf(w_gate[0]), w_up=bf(w_up[0]), w_down=bf(w_down[0]),
        w_in_o=bf(w_in_o[0]), conv_w=conv_w[0], conv_b=row(conv_b[0]), ln_g=row(ln_g[0]), ln_b=row(ln_b[0]),
        w_out_o=bf(w_out_o[0]),
        w_router=jnp.pad(w_router[0], ((0, 0), (0, LANES - N_EXPERTS))),
        b_router=jnp.pad(row(b_router[0]), ((0, 0), (0, LANES - N_EXPERTS))),
        we_gate=bf(we_gate[0]), we_up=bf(we_up[0]), we_down=bf(we_down[0]),
        bd64=jnp.kron(jnp.eye(N_HEADS, dtype=F32), jnp.ones((HEAD_DIM, HEAD_DIM), F32)).astype(BF16),
        hm64=_head_mask(N_HEADS, HEAD_DIM), eye8=jnp.eye(N_HEADS, dtype=BF16),
        tril256=_tri(256, "lower_incl"), triu256=_tri(256, "upper_incl"),
        ustrict256=_tri(256, "lower_strict"), ustrict128=_tri(128, "lower_strict"),
    )

    mk, mv = _mem_kv(mem_prompt.reshape(B * N_MEM, D), bf(w_ck), bf(w_cv),
                     jnp.tile(g_ck, (1, X_HEADS)).reshape(-1, 1, X_WIDTH), 512)
    mk = mk.reshape(-1, B, N_MEM, X_WIDTH)
    mv = mv.reshape(-1, B, N_MEM, X_WIDTH)

    prompt = dict(kind="prompt", B=B, L=L, start_pos=0, seq_per_step=1, chunk=256, conv_chunk=128,
                  pool_prefix=jnp.zeros((B, POOL_BUF, MIX_HALF), F32),
                  conv_prefix=jnp.zeros((B, CONV_BUF, MIX_HALF), F32), x_ns=1, x_tq=512)
    yp, sp = _stack(x_prompt.reshape(B * L, D), mk, mv, W, prompt)

    sample = dict(kind="sample", B=SB, L=SL, start_pos=page_table.shape[1] * PAGE, seq_per_step=16,
                  chunk=SL, conv_chunk=SL, pool_prefix=state_pool[0], conv_prefix=state_conv[0], x_ns=8, x_tq=SL,
                  page_table=page_table,
                  fox_k=cache_fox_k[0].reshape(n_pool, PAGE, MIX_HALF),
                  fox_v=cache_fox_v[0].reshape(n_pool, PAGE, MIX_HALF),
                  fox_lf=cache_fox_logf[0],
                  sb_k=cache_sb_k[0].reshape(n_pool, PAGE, MIX_HALF),
                  sb_v=cache_sb_v[0].reshape(n_pool, PAGE, MIX_HALF))
    smk = cache_mem_k.reshape(-1, SB, N_MEM, X_WIDTH)
    smv = cache_mem_v.reshape(-1, SB, N_MEM, X_WIDTH)
    ys, ss = _stack(x_sample.reshape(SB * SL, D), smk, smv, W, sample)

    def outs(new, b, l):
        kv = lambda a: a.reshape(1, b, l, N_HEADS, HEAD_DIM)
        return (kv(new["fox_k"]), kv(new["fox_v"]), new["fox_logf"].reshape(1, b, l, N_HEADS),
                kv(new["sb_k"]), kv(new["sb_v"]), new["pool"][None], new["conv"][None])

    op = outs(sp, B, L)
    os_ = outs(ss, SB, SL)
    memk = mk.reshape(-1, B, N_MEM, X_HEADS, X_HEAD_DIM)
    memv = mv.reshape(-1, B, N_MEM, X_HEADS, X_HEAD_DIM)
    return (yp.reshape(B, L, D), ys.reshape(SB, SL, D), *op, memk, memv, *os_)
```

```python
import functools

import jax
import jax.numpy as jnp
from jax import lax
from jax.experimental import pallas as pl
from jax.experimental.pallas import tpu as pltpu

F32 = jnp.float32
BF16 = jnp.bfloat16

D_MODEL = 1024
MIX_HALF = 512
HEAD_DIM = 64
N_HEADS = 8
POOL_WINDOWS = (2, 4, 8, 16)
POOL_GROUP = 128
POOL_BUF = 15
CONV_WIDTH = 31
CONV_BUF = 30
N_MEM = 256
X_HEADS = 4
X_HEAD_DIM = 128
X_WIDTH = 512
N_EXPERTS = 8
PAGE = 128
EPS = 1e-6
ATT_SCALE = HEAD_DIM ** -0.5
NEG_BIG = -1e30

LANES = 128
SUBLANES = 8
MIB = 1024 * 1024


def _params(sem, vmem_mib):
    return pltpu.CompilerParams(dimension_semantics=sem, vmem_limit_bytes=vmem_mib * MIB)


def _full(shape):
    return pl.BlockSpec(shape, lambda *_: (0,) * len(shape))


def _rms(x, g):
    return x * lax.rsqrt(jnp.mean(x * x, axis=-1, keepdims=True) + EPS) * g


def _split2(x):
    hi = x.astype(BF16)
    lo = (x - hi.astype(F32)).astype(BF16)
    return hi, lo


def _split3(x):
    hi = x.astype(BF16)
    r = x - hi.astype(F32)
    mid = r.astype(BF16)
    lo = (r - mid.astype(F32)).astype(BF16)
    return hi, mid, lo


def _dot(a, b):
    return jnp.dot(a, b, preferred_element_type=F32)


def _dot_nt(a, b):
    return lax.dot_general(a, b, (((1,), (1,)), ((), ())), preferred_element_type=F32)


def _softplus(z):
    return jnp.maximum(z, 0.0) + jnp.log1p(jnp.exp(-jnp.abs(z)))


def _head_rms(a, g, bd, head_dim):
    hi, lo = _split2(a * a)
    ssq = _dot(hi, bd) + _dot(lo, bd)
    return a * lax.rsqrt(ssq * (1.0 / head_dim) + EPS) * g


def _in_even_kernel(x_ref, g_ref, w_ref, wf_ref, wft_ref, bf_ref, bft_ref, gq_ref, gk_ref, bd_ref,
                    p_ref, q_ref, k32_ref, kb_ref, v32_ref, vb_ref, lf_ref, lft_ref):
    h = _rms(x_ref[...], g_ref[...]).astype(BF16)
    bd = bd_ref[...]
    p_ref[...] = _dot(h, w_ref[:, 0:MIX_HALF])
    q = _head_rms(_dot(h, w_ref[:, MIX_HALF:2 * MIX_HALF]), gq_ref[...], bd, HEAD_DIM)
    q_ref[...] = q
    k = _head_rms(_dot(h, w_ref[:, 2 * MIX_HALF:3 * MIX_HALF]), gk_ref[...], bd, HEAD_DIM)
    k32_ref[...] = k
    kb_ref[...] = k.astype(BF16)
    v = _dot(h, w_ref[:, 3 * MIX_HALF:4 * MIX_HALF])
    v32_ref[...] = v
    vb_ref[...] = v.astype(BF16)
    fl = _dot(h, wf_ref[...]) + bf_ref[...]
    lf_ref[...] = (-_softplus(-fl))[:, 0:N_HEADS]
    flt = _dot_nt(wft_ref[...], h) + bft_ref[...]
    lft_ref[...] = -_softplus(-flt)


def _in_even(x, g, w, wf, wft, bf, bft, gq, gk, bd, tm):
    T = x.shape[0]
    row = lambda n: pl.BlockSpec((tm, n), lambda i: (i, 0))
    sds = jax.ShapeDtypeStruct
    return pl.pallas_call(
        _in_even_kernel,
        grid=(T // tm,),
        in_specs=[row(D_MODEL), _full((1, D_MODEL)), _full((D_MODEL, 4 * MIX_HALF)), _full((D_MODEL, LANES)),
                  _full((N_HEADS, D_MODEL)), _full((1, LANES)), _full((N_HEADS, 1)), _full((1, MIX_HALF)),
                  _full((1, MIX_HALF)), _full((MIX_HALF, MIX_HALF))],
        out_specs=[row(MIX_HALF), row(MIX_HALF), row(MIX_HALF), row(MIX_HALF), row(MIX_HALF), row(MIX_HALF),
                   row(N_HEADS), pl.BlockSpec((N_HEADS, tm), lambda i: (0, i))],
        out_shape=[sds((T, MIX_HALF), F32), sds((T, MIX_HALF), F32), sds((T, MIX_HALF), F32),
                   sds((T, MIX_HALF), BF16), sds((T, MIX_HALF), F32), sds((T, MIX_HALF), BF16),
                   sds((T, N_HEADS), F32), sds((N_HEADS, T), F32)],
        compiler_params=_params(("parallel",), 48),
        name="in_even",
    )(x, g, w, wf, wft, bf, bft, gq, gk, bd)


def _in_odd_kernel(x_ref, g_ref, w_ref, u_ref, q_ref, k32_ref, kb_ref, v32_ref, vb_ref):
    h = _rms(x_ref[...], g_ref[...]).astype(BF16)
    a = _dot(h, w_ref[:, 0:MIX_HALF])
    gate = _dot(h, w_ref[:, MIX_HALF:2 * MIX_HALF])
    u_ref[...] = a * (1.0 / (1.0 + jnp.exp(-gate)))
    q_ref[...] = _dot(h, w_ref[:, 2 * MIX_HALF:3 * MIX_HALF])
    k = _dot(h, w_ref[:, 3 * MIX_HALF:4 * MIX_HALF])
    k32_ref[...] = k
    kb_ref[...] = k.astype(BF16)
    v = _dot(h, w_ref[:, 4 * MIX_HALF:5 * MIX_HALF])
    v32_ref[...] = v
    vb_ref[...] = v.astype(BF16)


def _in_odd(x, g, w, tm):
    T = x.shape[0]
    row = lambda n: pl.BlockSpec((tm, n), lambda i: (i, 0))
    sds = jax.ShapeDtypeStruct
    return pl.pallas_call(
        _in_odd_kernel,
        grid=(T // tm,),
        in_specs=[row(D_MODEL), _full((1, D_MODEL)), _full((D_MODEL, 5 * MIX_HALF))],
        out_specs=[row(MIX_HALF)] * 6,
        out_shape=[sds((T, MIX_HALF), F32), sds((T, MIX_HALF), F32), sds((T, MIX_HALF), F32),
                   sds((T, MIX_HALF), BF16), sds((T, MIX_HALF), F32), sds((T, MIX_HALF), BF16)],
        compiler_params=_params(("parallel",), 48),
        name="in_odd",
    )(x, g, w)


POOL_PAD = 16
CONV_PAD = 32


def _pool_kernel(pre_ref, p_ref, w_ref, sc_ref, y_ref, st_ref, buf, *, ns, L, ch, start_pos):
    for s in range(ns):
        buf[0:POOL_PAD - POOL_BUF, :] = jnp.zeros((POOL_PAD - POOL_BUF, MIX_HALF), F32)
        buf[POOL_PAD - POOL_BUF:POOL_PAD, :] = pre_ref[s]
        buf[POOL_PAD:POOL_PAD + L, :] = p_ref[s]
        st_ref[s] = buf[POOL_PAD + L - POOL_BUF:POOL_PAD + L, :]
        for c0 in range(0, L, ch):
            pos = start_pos + c0 + lax.broadcasted_iota(jnp.int32, (ch, POOL_GROUP), 0)
            outs = []
            for gi, w in enumerate(POOL_WINDOWS):
                ls = slice(gi * POOL_GROUP, (gi + 1) * POOL_GROUP)
                cur = buf[POOL_PAD + c0:POOL_PAD + c0 + ch, ls]
                acc = cur
                for i in range(1, w):
                    acc = acc + buf[POOL_PAD + c0 - i:POOL_PAD + c0 - i + ch, ls]
                cnt = jnp.minimum(pos + 1, w).astype(F32)
                d = acc / cnt - cur
                outs.append(_dot(d.astype(BF16), w_ref[gi]) * sc_ref[:, ls])
            y_ref[s, c0:c0 + ch, :] = jnp.concatenate(outs, axis=-1)


def _pool(prefix, p, w_grp, scale, ns, ch, start_pos):
    B, L, _ = p.shape
    kern = functools.partial(_pool_kernel, ns=ns, L=L, ch=ch, start_pos=start_pos)
    return pl.pallas_call(
        kern,
        grid=(B // ns,),
        in_specs=[pl.BlockSpec((ns, POOL_BUF, MIX_HALF), lambda i: (i, 0, 0)),
                  pl.BlockSpec((ns, L, MIX_HALF), lambda i: (i, 0, 0)),
                  _full((len(POOL_WINDOWS), POOL_GROUP, POOL_GROUP)), _full((1, MIX_HALF))],
        out_specs=[pl.BlockSpec((ns, L, MIX_HALF), lambda i: (i, 0, 0)),
                   pl.BlockSpec((ns, POOL_BUF, MIX_HALF), lambda i: (i, 0, 0))],
        out_shape=[jax.ShapeDtypeStruct((B, L, MIX_HALF), F32),
                   jax.ShapeDtypeStruct((B, POOL_BUF, MIX_HALF), F32)],
        scratch_shapes=[pltpu.VMEM((POOL_PAD + L, MIX_HALF), F32)],
        compiler_params=_params(("parallel",), 40),
        name="pool_mixer",
    )(prefix, p, w_grp, scale)


def _conv_kernel(pre_ref, u_ref, w_ref, b_ref, lg_ref, lb_ref, y_ref, st_ref, buf, *, ns, L, ch):
    for s in range(ns):
        buf[0:CONV_PAD - CONV_BUF, :] = jnp.zeros((CONV_PAD - CONV_BUF, MIX_HALF), F32)
        buf[CONV_PAD - CONV_BUF:CONV_PAD, :] = pre_ref[s]
        buf[CONV_PAD:CONV_PAD + L, :] = u_ref[s]
        st_ref[s] = buf[CONV_PAD + L - CONV_BUF:CONV_PAD + L, :]
        base = CONV_PAD - CONV_BUF
        for c0 in range(0, L, ch):
            acc = buf[base + c0:base + c0 + ch, :] * w_ref[0:1, :]
            for j in range(1, CONV_WIDTH):
                acc = acc + buf[base + c0 + j:base + c0 + j + ch, :] * w_ref[j:j + 1, :]
            y = acc + b_ref[...]
            mu = jnp.mean(y, axis=-1, keepdims=True)
            yc = y - mu
            var = jnp.mean(yc * yc, axis=-1, keepdims=True)
            yn = yc * lax.rsqrt(var + EPS) * lg_ref[...] + lb_ref[...]
            y_ref[s, c0:c0 + ch, :] = yn * (1.0 / (1.0 + jnp.exp(-yn)))


def _conv(prefix, u, conv_w, conv_b, ln_g, ln_b, ns, ch):
    B, L, _ = u.shape
    kern = functools.partial(_conv_kernel, ns=ns, L=L, ch=ch)
    return pl.pallas_call(
        kern,
        grid=(B // ns,),
        in_specs=[pl.BlockSpec((ns, CONV_BUF, MIX_HALF), lambda i: (i, 0, 0)),
                  pl.BlockSpec((ns, L, MIX_HALF), lambda i: (i, 0, 0)),
                  _full((CONV_WIDTH, MIX_HALF)), _full((1, MIX_HALF)), _full((1, MIX_HALF)), _full((1, MIX_HALF))],
        out_specs=[pl.BlockSpec((ns, L, MIX_HALF), lambda i: (i, 0, 0)),
                   pl.BlockSpec((ns, CONV_BUF, MIX_HALF), lambda i: (i, 0, 0))],
        out_shape=[jax.ShapeDtypeStruct((B, L, MIX_HALF), F32),
                   jax.ShapeDtypeStruct((B, CONV_BUF, MIX_HALF), F32)],
        scratch_shapes=[pltpu.VMEM((CONV_PAD + L, MIX_HALF), F32)],
        compiler_params=_params(("parallel",), 40),
        name="conv_module",
    )(prefix, u, conv_w, conv_b, ln_g, ln_b)


def _cumsum_prompt_kernel(lf_ref, lft_ref, tril_ref, triu_ref, f_ref, ft_ref, *, L, blk):
    c_col = jnp.zeros((1, N_HEADS), F32)
    c_row = jnp.zeros((N_HEADS, 1), F32)
    for b0 in range(0, L, blk):
        x = lf_ref[b0:b0 + blk, :]
        xt = lft_ref[:, b0:b0 + blk]
        f = c_col
        for part in _split3(x):
            f = f + _dot(tril_ref[...], part)
        ft = c_row
        for part in _split3(xt):
            ft = ft + _dot(part, triu_ref[...])
        f_ref[b0:b0 + blk, :] = f
        ft_ref[:, b0:b0 + blk] = ft
        c_col = f[blk - 1:blk, :]
        c_row = ft[:, blk - 1:blk]


def _cumsum_prompt(lf, lft, tril, triu, B, L, blk):
    kern = functools.partial(_cumsum_prompt_kernel, L=L, blk=blk)
    return pl.pallas_call(
        kern,
        grid=(B,),
        in_specs=[pl.BlockSpec((L, N_HEADS), lambda b: (b, 0)), pl.BlockSpec((N_HEADS, L), lambda b: (0, b)),
                  _full((blk, blk)), _full((blk, blk))],
        out_specs=[pl.BlockSpec((L, N_HEADS), lambda b: (b, 0)), pl.BlockSpec((N_HEADS, L), lambda b: (0, b))],
        out_shape=[jax.ShapeDtypeStruct((B * L, N_HEADS), F32), jax.ShapeDtypeStruct((N_HEADS, B * L), F32)],
        compiler_params=_params(("parallel",), 32),
        name="fox_cumsum_prompt",
    )(lf, lft, tril, triu)


def _head_pair(q):
    lane = lax.broadcasted_iota(jnp.int32, q.shape, 1)
    zero = jnp.zeros_like(q)
    scale = jnp.asarray(ATT_SCALE, q.dtype)
    return [jnp.where(lane < HEAD_DIM, q, zero) * scale, jnp.where(lane >= HEAD_DIM, q, zero) * scale]


def _stack_heads(q):
    return jnp.concatenate(_head_pair(q), axis=0)


SB_DEAD = -105.0


def _fox_prompt_kernel(q_ref, k_ref, v_ref, f_ref, ft_ref, o_ref, *, blk):
    hp = pl.program_id(1)
    qi = pl.program_id(2)
    qs = _stack_heads(q_ref[...])
    f = f_ref[...]
    lane8 = lax.broadcasted_iota(jnp.int32, f.shape, 1)
    fq = [jnp.sum(jnp.where(lane8 == 2 * hp + i, f, 0.0), axis=1, keepdims=True) for i in range(2)]
    row = lax.broadcasted_iota(jnp.int32, (2 * blk, blk), 0) & (blk - 1)
    col = lax.broadcasted_iota(jnp.int32, (2 * blk, blk), 1)

    def block(ki, carry, diagonal):
        m, l, acc = carry
        ks = pl.multiple_of(ki * blk, blk)
        s = _dot_nt(qs, k_ref[pl.ds(ks, blk), :])
        s = jnp.concatenate([s[i * blk:(i + 1) * blk] + (fq[i] - ft_ref[0, i:i + 1, pl.ds(ks, blk)])
                             for i in range(2)], axis=0)
        if diagonal:
            s = jnp.where(col <= row, s, -jnp.inf)
        m_new = jnp.maximum(m, jnp.max(s, axis=1, keepdims=True))
        alpha = jnp.exp(m - m_new)
        p = jnp.exp(s - m_new)
        l = alpha * l + jnp.sum(p, axis=1, keepdims=True)
        acc = alpha * acc + _dot(p.astype(BF16), v_ref[pl.ds(ks, blk), :])
        return m_new, l, acc

    init = (jnp.full((2 * blk, 1), NEG_BIG, F32), jnp.zeros((2 * blk, 1), F32), jnp.zeros((2 * blk, LANES), F32))
    carry = lax.fori_loop(0, qi, lambda ki, c: block(ki, c, False), init)
    _, l, acc = block(qi, carry, True)
    o = acc / l
    lane = lax.broadcasted_iota(jnp.int32, (blk, LANES), 1)
    o_ref[...] = jnp.where(lane < HEAD_DIM, o[0:blk], o[blk:2 * blk])


def _fox_prompt(q, k, v, f, ft, B, L, blk):
    nq = L // blk
    kern = functools.partial(_fox_prompt_kernel, blk=blk)
    return pl.pallas_call(
        kern,
        grid=(B, N_HEADS // 2, nq),
        in_specs=[pl.BlockSpec((blk, LANES), lambda b, h, i: (b * nq + i, h)),
                  pl.BlockSpec((L, LANES), lambda b, h, i: (b, h)),
                  pl.BlockSpec((L, LANES), lambda b, h, i: (b, h)),
                  pl.BlockSpec((blk, N_HEADS), lambda b, h, i: (b * nq + i, 0)),
                  pl.BlockSpec((1, 2, L), lambda b, h, i: (h, 0, b))],
        out_specs=pl.BlockSpec((blk, LANES), lambda b, h, i: (b * nq + i, h)),
        out_shape=jax.ShapeDtypeStruct((B * L, MIX_HALF), F32),
        compiler_params=_params(("parallel", "parallel", "arbitrary"), 32),
        name="fox_attention_prompt",
    )(q, k, v, f, ft)


def _sb_block(z, c, v, u, mask):
    sp = _softplus(z)
    ln = -sp
    if mask is not None:
        ln = jnp.where(mask, ln, 0.0)
    hi, lo = _split2(ln)
    e = _dot(hi, u) + _dot(lo, u)
    a = jnp.exp((z - sp) + e + c)
    if mask is not None:
        a = jnp.where(mask, a, 0.0)
    return _dot(a.astype(BF16), v), c + jnp.sum(ln, axis=1, keepdims=True)


def _sb_prompt_kernel(q_ref, k_ref, v_ref, u_ref, o_ref, *, blk):
    qi = pl.program_id(2)
    qs = _stack_heads(q_ref[...])
    row = lax.broadcasted_iota(jnp.int32, (2 * blk, blk), 0) & (blk - 1)
    col = lax.broadcasted_iota(jnp.int32, (2 * blk, blk), 1)
    u = u_ref[...]

    def block(ki, c, acc, diagonal):
        ks = pl.multiple_of(ki * blk, blk)
        z = _dot_nt(qs, k_ref[pl.ds(ks, blk), :])
        pv, c = _sb_block(z, c, v_ref[pl.ds(ks, blk), :], u, (col < row) if diagonal else None)
        return c, acc + pv

    c, acc = block(qi, jnp.zeros((2 * blk, 1), F32), jnp.zeros((2 * blk, LANES), F32), True)

    def cond(state):
        j, alive, _, _ = state
        return (j < qi) & (alive > 0)

    def body(state):
        j, _, c, acc = state
        c, acc = block(qi - 1 - j, c, acc, False)
        return j + 1, (jnp.max(c) > SB_DEAD).astype(jnp.int32), c, acc

    _, _, _, acc = lax.while_loop(cond, body, (jnp.int32(0), jnp.int32(1), c, acc))
    lane = lax.broadcasted_iota(jnp.int32, (blk, LANES), 1)
    o_ref[...] = jnp.where(lane < HEAD_DIM, acc[0:blk], acc[blk:2 * blk])


def _sb_prompt(q, k, v, u, B, L, blk):
    nq = L // blk
    kern = functools.partial(_sb_prompt_kernel, blk=blk)
    return pl.pallas_call(
        kern,
        grid=(B, N_HEADS // 2, nq),
        in_specs=[pl.BlockSpec((blk, LANES), lambda b, h, i: (b * nq + i, h)),
                  pl.BlockSpec((L, LANES), lambda b, h, i: (b, h)),
                  pl.BlockSpec((L, LANES), lambda b, h, i: (b, h)),
                  _full((blk, blk))],
        out_specs=pl.BlockSpec((blk, LANES), lambda b, h, i: (b * nq + i, h)),
        out_shape=jax.ShapeDtypeStruct((B * L, MIX_HALF), F32),
        compiler_params=_params(("parallel", "parallel", "arbitrary"), 32),
        name="sb_attention_prompt",
    )(q, k, v, u)


N_PAGES = 16
DEC_SEQ = 8
BD_ROWS = DEC_SEQ * N_HEADS


def _block_diag_q(q, hm):
    rows = [jnp.broadcast_to(q[t:t + 1, :], (N_HEADS, MIX_HALF)) * hm for t in range(DEC_SEQ)]
    return (jnp.concatenate(rows, axis=0) * ATT_SCALE).astype(BF16)


def _pad_new(x):
    return jnp.concatenate([x, jnp.zeros((PAGE - DEC_SEQ, x.shape[1]), x.dtype)], axis=0)


def _merge_heads(o, hm):
    return jnp.concatenate(
        [jnp.sum(o[t * N_HEADS:(t + 1) * N_HEADS, :] * hm, axis=0, keepdims=True) for t in range(DEC_SEQ)], axis=0)


def _new_key_mask(strict):
    row = lax.broadcasted_iota(jnp.int32, (BD_ROWS, PAGE), 0) // N_HEADS
    col = lax.broadcasted_iota(jnp.int32, (BD_ROWS, PAGE), 1)
    return (col < row) if strict else (col <= row)


def _cumsum_lanes(x):
    n = x.shape[1]
    lane = lax.broadcasted_iota(jnp.int32, x.shape, 1)
    k = 1
    while k < n:
        x = x + jnp.where(lane >= k, pltpu.roll(x, k, 1), 0.0)
        k *= 2
    return x


def _fox_decode_kernel(pt_ref, *refs):
    k_pages = refs[0:N_PAGES]
    v_pages = refs[N_PAGES:2 * N_PAGES]
    lf_pages = refs[2 * N_PAGES:3 * N_PAGES]
    q_ref, kn_ref, vn_ref, lftn_ref, hm_ref, eye_ref, o_ref = refs[3 * N_PAGES:]
    del pt_ref
    hm = hm_ref[...]
    qbd = _block_diag_q(q_ref[...], hm)
    parts = []
    for j in range(N_PAGES):
        x = lf_pages[j][0]
        acc = jnp.zeros((N_HEADS, PAGE), F32)
        for part in _split3(x):
            acc = acc + _dot_nt(eye_ref[...], part)
        parts.append(acc)
    parts.append(jnp.concatenate([lftn_ref[0], jnp.zeros((N_HEADS, PAGE - DEC_SEQ), F32)], axis=1))
    fk = _cumsum_lanes(jnp.concatenate(parts, axis=1))
    fnew = fk[:, N_PAGES * PAGE:]
    lane = lax.broadcasted_iota(jnp.int32, (N_HEADS, PAGE), 1)
    fq = jnp.concatenate([jnp.sum(jnp.where(lane == t, fnew, 0.0), axis=1, keepdims=True)
                          for t in range(DEC_SEQ)], axis=0)
    scores = []
    for j in range(N_PAGES + 1):
        k = k_pages[j][0].astype(BF16) if j < N_PAGES else _pad_new(kn_ref[...]).astype(BF16)
        bias = jnp.concatenate([fk[:, j * PAGE:(j + 1) * PAGE]] * DEC_SEQ, axis=0)
        s = _dot_nt(qbd, k) + (fq - bias)
        if j == N_PAGES:
            s = jnp.where(_new_key_mask(False), s, -jnp.inf)
        scores.append(s)
    m = scores[0].max(axis=1, keepdims=True)
    for s in scores[1:]:
        m = jnp.maximum(m, s.max(axis=1, keepdims=True))
    l = jnp.zeros((BD_ROWS, 1), F32)
    acc = jnp.zeros((BD_ROWS, MIX_HALF), F32)
    for j in range(N_PAGES + 1):
        p = jnp.exp(scores[j] - m)
        l = l + jnp.sum(p, axis=1, keepdims=True)
        v = v_pages[j][0].astype(BF16) if j < N_PAGES else _pad_new(vn_ref[...]).astype(BF16)
        acc = acc + _dot(p.astype(BF16), v)
    o_ref[...] = _merge_heads(acc / l, hm)


def _sb_decode_kernel(pt_ref, *refs):
    k_pages = refs[0:N_PAGES]
    v_pages = refs[N_PAGES:2 * N_PAGES]
    q_ref, kn_ref, vn_ref, hm_ref, u_ref, o_ref = refs[2 * N_PAGES:]
    del pt_ref
    hm = hm_ref[...]
    qbd = _block_diag_q(q_ref[...], hm)
    u = u_ref[...]
    c = jnp.zeros((BD_ROWS, 1), F32)
    acc = jnp.zeros((BD_ROWS, MIX_HALF), F32)
    for j in range(N_PAGES, -1, -1):
        if j == N_PAGES:
            k = _pad_new(kn_ref[...]).astype(BF16)
            v = _pad_new(vn_ref[...]).astype(BF16)
            mask = _new_key_mask(True)
        else:
            k = k_pages[j][0].astype(BF16)
            v = v_pages[j][0].astype(BF16)
            mask = None
        pv, c = _sb_block(_dot_nt(qbd, k), c, v, u, mask)
        acc = acc + pv
    o_ref[...] = _merge_heads(acc, hm)


def _page_specs(shape_tail, n):
    specs = []
    for j in range(n):
        specs.append(pl.BlockSpec((1,) + shape_tail, functools.partial(
            lambda b, pt, j: (pt[b, j],) + (0,) * len(shape_tail), j=j)))
    return specs


def _fox_decode(page_table, ck, cv, clf, q, kn, vn, lftn, hm, eye):
    nb = page_table.shape[0]
    seq = lambda n: pl.BlockSpec((DEC_SEQ, n), lambda b, pt: (b, 0))
    const = lambda shape: pl.BlockSpec(shape, lambda b, pt: (0,) * len(shape))
    in_specs = (_page_specs((PAGE, MIX_HALF), N_PAGES) + _page_specs((PAGE, MIX_HALF), N_PAGES)
                + _page_specs((PAGE, N_HEADS), N_PAGES)
                + [seq(MIX_HALF), seq(MIX_HALF), seq(MIX_HALF),
                   pl.BlockSpec((1, N_HEADS, DEC_SEQ), lambda b, pt: (b, 0, 0)),
                   const((N_HEADS, MIX_HALF)), const((N_HEADS, N_HEADS))])
    return pl.pallas_call(
        _fox_decode_kernel,
        grid_spec=pltpu.PrefetchScalarGridSpec(
            num_scalar_prefetch=1, grid=(nb,), in_specs=in_specs, out_specs=seq(MIX_HALF)),
        out_shape=jax.ShapeDtypeStruct((nb * DEC_SEQ, MIX_HALF), F32),
        compiler_params=_params(("arbitrary",), 48),
        name="fox_attention_decode",
    )(page_table, *([ck] * N_PAGES), *([cv] * N_PAGES), *([clf] * N_PAGES), q, kn, vn, lftn, hm, eye)


def _sb_decode(page_table, ck, cv, q, kn, vn, hm, u):
    nb = page_table.shape[0]
    seq = lambda n: pl.BlockSpec((DEC_SEQ, n), lambda b, pt: (b, 0))
    const = lambda shape: pl.BlockSpec(shape, lambda b, pt: (0,) * len(shape))
    in_specs = (_page_specs((PAGE, MIX_HALF), N_PAGES) + _page_specs((PAGE, MIX_HALF), N_PAGES)
                + [seq(MIX_HALF), seq(MIX_HALF), seq(MIX_HALF), const((N_HEADS, MIX_HALF)), const((PAGE, PAGE))])
    return pl.pallas_call(
        _sb_decode_kernel,
        grid_spec=pltpu.PrefetchScalarGridSpec(
            num_scalar_prefetch=1, grid=(nb,), in_specs=in_specs, out_specs=seq(MIX_HALF)),
        out_shape=jax.ShapeDtypeStruct((nb * DEC_SEQ, MIX_HALF), F32),
        compiler_params=_params(("arbitrary",), 48),
        name="sb_attention_decode",
    )(page_table, *([ck] * N_PAGES), *([cv] * N_PAGES), q, kn, vn, hm, u)


def _x_head_rms(a, g):
    outs = []
    for h in range(X_HEADS):
        ah = a[:, h * X_HEAD_DIM:(h + 1) * X_HEAD_DIM]
        outs.append(ah * lax.rsqrt(jnp.mean(ah * ah, axis=-1, keepdims=True) + EPS))
    return jnp.concatenate(outs, axis=-1) * g


def _mix_out_kernel(x_ref, ya_ref, yb_ref, wo_ref, g_ref, wq_ref, gq_ref, x1_ref, qc_ref):
    y = _dot(ya_ref[...].astype(BF16), wo_ref[0:MIX_HALF, :]) + _dot(yb_ref[...].astype(BF16), wo_ref[MIX_HALF:, :])
    x1 = x_ref[...] + y
    x1_ref[...] = x1
    hc = _rms(x1, g_ref[...]).astype(BF16)
    qc_ref[...] = _x_head_rms(_dot(hc, wq_ref[...]), gq_ref[...])


def _mix_out(x, ya, yb, wo, g, wq, gq, tm):
    T = x.shape[0]
    row = lambda n: pl.BlockSpec((tm, n), lambda i: (i, 0))
    return pl.pallas_call(
        _mix_out_kernel,
        grid=(T // tm,),
        in_specs=[row(D_MODEL), row(MIX_HALF), row(MIX_HALF), _full((D_MODEL, D_MODEL)), _full((1, D_MODEL)),
                  _full((D_MODEL, X_WIDTH)), _full((1, X_WIDTH))],
        out_specs=[row(D_MODEL), row(X_WIDTH)],
        out_shape=[jax.ShapeDtypeStruct((T, D_MODEL), F32), jax.ShapeDtypeStruct((T, X_WIDTH), F32)],
        compiler_params=_params(("parallel",), 40),
        name="mixer_out_proj",
    )(x, ya, yb, wo, g, wq, gq)


def _mem_kv_kernel(m_ref, wk_ref, wv_ref, gk_ref, k_ref, v_ref):
    m = m_ref[...].astype(BF16)
    k_ref[0] = _x_head_rms(_dot(m, wk_ref[0]), gk_ref[0])
    v_ref[0] = _dot(m, wv_ref[0])


def _mem_kv(mem, wk, wv, gk, tm):
    T = mem.shape[0]
    depth = wk.shape[0]
    return pl.pallas_call(
        _mem_kv_kernel,
        grid=(depth, T // tm),
        in_specs=[pl.BlockSpec((tm, D_MODEL), lambda l, i: (i, 0)),
                  pl.BlockSpec((1, D_MODEL, X_WIDTH), lambda l, i: (l, 0, 0)),
                  pl.BlockSpec((1, D_MODEL, X_WIDTH), lambda l, i: (l, 0, 0)),
                  pl.BlockSpec((1, 1, X_WIDTH), lambda l, i: (l, 0, 0))],
        out_specs=[pl.BlockSpec((1, tm, X_WIDTH), lambda l, i: (l, i, 0))] * 2,
        out_shape=[jax.ShapeDtypeStruct((depth, T, X_WIDTH), F32)] * 2,
        compiler_params=_params(("parallel", "parallel"), 32),
        name="memory_kv",
    )(mem, wk, wv, gk)


def _cross_kernel(x_ref, q_ref, mk_ref, mv_ref, wo_ref, g_ref, x2_ref, hf_ref, *, ns, tq):
    outs = []
    for s in range(ns):
        q = q_ref[s * tq:(s + 1) * tq, :]
        heads = []
        for h in range(X_HEADS):
            ls = slice(h * X_HEAD_DIM, (h + 1) * X_HEAD_DIM)
            qh = (q[:, ls] * (X_HEAD_DIM ** -0.5)).astype(BF16)
            sc = _dot_nt(qh, mk_ref[s, :, ls].astype(BF16))
            p = jnp.exp(sc - jnp.max(sc, axis=1, keepdims=True))
            o = _dot(p.astype(BF16), mv_ref[s, :, ls].astype(BF16))
            heads.append(o / jnp.sum(p, axis=1, keepdims=True))
        outs.append(jnp.concatenate(heads, axis=-1))
    o = outs[0] if ns == 1 else jnp.concatenate(outs, axis=0)
    x2 = x_ref[...] + _dot(o.astype(BF16), wo_ref[...])
    x2_ref[...] = x2
    hf_ref[...] = _rms(x2, g_ref[...])


def _cross(x, q, mk, mv, wo, g, ns, tq):
    T = x.shape[0]
    L = T // mk.shape[0]
    nq = L // tq
    rows = ns * tq
    row = lambda n: pl.BlockSpec((rows, n), lambda i: (i, 0))
    mem = pl.BlockSpec((ns, N_MEM, X_WIDTH), lambda i: (i // nq, 0, 0))
    kern = functools.partial(_cross_kernel, ns=ns, tq=tq)
    return pl.pallas_call(
        kern,
        grid=(T // rows,),
        in_specs=[row(D_MODEL), row(X_WIDTH), mem, mem, _full((X_WIDTH, D_MODEL)), _full((1, D_MODEL))],
        out_specs=[row(D_MODEL), row(D_MODEL)],
        out_shape=[jax.ShapeDtypeStruct((T, D_MODEL), F32)] * 2,
        compiler_params=_params(("parallel",), 48),
        name="cross_attention",
    )(x, q, mk, mv, wo, g)


def _swiglu_kernel(h_ref, x_ref, wg_ref, wu_ref, wd_ref, o_ref, hb, acc):
    f = pl.program_id(1)

    @pl.when(f == 0)
    def _():
        hb[...] = h_ref[...].astype(BF16)
        acc[...] = jnp.zeros_like(acc)

    h = hb[...]
    gte = _dot(h, wg_ref[...])
    act = gte * (1.0 / (1.0 + jnp.exp(-gte))) * _dot(h, wu_ref[...])
    acc[...] += _dot(act.astype(BF16), wd_ref[...])

    @pl.when(f == pl.num_programs(1) - 1)
    def _():
        o_ref[...] = x_ref[...] + acc[...]


def _swiglu(h, x, wg, wu, wd, tm, tf):
    T = h.shape[0]
    dff = wg.shape[1]
    row = pl.BlockSpec((tm, D_MODEL), lambda i, f: (i, 0))
    return pl.pallas_call(
        _swiglu_kernel,
        grid=(T // tm, dff // tf),
        in_specs=[row, row, pl.BlockSpec((D_MODEL, tf), lambda i, f: (0, f)),
                  pl.BlockSpec((D_MODEL, tf), lambda i, f: (0, f)), pl.BlockSpec((tf, D_MODEL), lambda i, f: (f, 0))],
        out_specs=row,
        out_shape=jax.ShapeDtypeStruct((T, D_MODEL), F32),
        scratch_shapes=[pltpu.VMEM((tm, D_MODEL), BF16), pltpu.VMEM((tm, D_MODEL), F32)],
        compiler_params=_params(("parallel", "arbitrary"), 56),
        name="dense_swiglu",
    )(h, x, wg, wu, wd)


def _router_kernel(h_ref, w_ref, b_ref, r_ref):
    h_hi, h_lo = _split2(h_ref[...])
    w_hi, w_lo = _split2(w_ref[...])
    logits = _dot(h_hi, w_hi) + _dot(h_hi, w_lo) + _dot(h_lo, w_hi) + b_ref[...]
    lane = lax.broadcasted_iota(jnp.int32, logits.shape, 1)
    logits = jnp.where(lane < N_EXPERTS, logits, -jnp.inf)
    m1 = jnp.max(logits, axis=1, keepdims=True)
    i1 = jnp.min(jnp.where(logits == m1, lane, LANES), axis=1, keepdims=True)
    rest = jnp.where(lane == i1, -jnp.inf, logits)
    m2 = jnp.max(rest, axis=1, keepdims=True)
    i2 = jnp.min(jnp.where(rest == m2, lane, LANES), axis=1, keepdims=True)
    e2 = jnp.exp(m2 - m1)
    g1 = 1.0 / (1.0 + e2)
    g2 = e2 / (1.0 + e2)
    r_ref[...] = jnp.where(lane == 0, i1.astype(F32), jnp.where(lane == 1, i2.astype(F32),
                           jnp.where(lane == 2, g1, jnp.where(lane == 3, g2, 0.0))))


def _router(h, w, b, tm):
    T = h.shape[0]
    return pl.pallas_call(
        _router_kernel,
        grid=(T // tm,),
        in_specs=[pl.BlockSpec((tm, D_MODEL), lambda i: (i, 0)), _full((D_MODEL, LANES)), _full((1, LANES))],
        out_specs=pl.BlockSpec((tm, LANES), lambda i: (i, 0)),
        out_shape=jax.ShapeDtypeStruct((T, LANES), F32),
        compiler_params=_params(("parallel",), 32),
        name="moe_router",
    )(h, w, b)


MOE_TM = 512
MOE_TF = 1792


def _moe_plan(r, tm):
    T = r.shape[0]
    n_pairs = 2 * T
    e = r[:, 0:2].astype(jnp.int32).T.reshape(-1)
    g = r[:, 2:4].T.reshape(-1)
    onehot = (e[:, None] == jnp.arange(N_EXPERTS, dtype=jnp.int32)[None, :]).astype(jnp.int32)
    csum = jnp.cumsum(onehot, axis=0)
    rank = jnp.sum(csum * onehot, axis=1) - 1
    cnt = csum[-1]
    psz = (cnt + tm - 1) // tm * tm
    gend = jnp.cumsum(psz)
    gstart = gend - psz
    pos = gstart[e] + rank
    n_rows = n_pairs + N_EXPERTS * tm
    n_tiles = n_rows // tm
    pair = jnp.arange(n_pairs, dtype=jnp.int32)
    src = jnp.zeros((n_rows,), jnp.int32).at[pos].set(pair % T)
    dst = jnp.zeros((n_rows,), jnp.int32).at[pos].set(pair)
    gate = jnp.zeros((n_rows,), F32).at[pos].set(g)
    t0 = jnp.arange(n_tiles, dtype=jnp.int32) * tm
    tile_e = jnp.minimum(jnp.searchsorted(gend, t0, side="right"), N_EXPERTS - 1).astype(jnp.int32)
    n_real = jnp.clip((gstart + cnt)[tile_e] - t0, 0, tm).astype(jnp.int32)
    return src.reshape(n_tiles, 1, tm), dst.reshape(n_tiles, 1, tm), gate.reshape(n_rows, 1), tile_e, n_real


def _moe_kernel(te_ref, nreal_ref, src_ref, srcn_ref, dst_ref, gate_ref, h_hbm, wg_ref, wu_ref, wd_ref, y_hbm,
                xin, hb, acc, yout, gsem, ssem, pend, *, tm):
    del te_ref
    i = pl.program_id(0)
    f = pl.program_id(1)
    nt = pl.num_programs(0)
    last_f = pl.num_programs(1) - 1
    slot = i % 2
    n_real = nreal_ref[i]
    is_valid = n_real > 0

    def gather(idx_ref, s):
        def body(r, carry):
            pltpu.make_async_copy(h_hbm.at[pl.ds(idx_ref[0, 0, r], 1)], xin.at[s, pl.ds(r, 1)], gsem.at[s]).start()
            return carry
        lax.fori_loop(0, tm, body, 0)

    def wait_scatter(s):
        def body(r, carry):
            pltpu.make_async_copy(yout.at[s, pl.ds(0, 1)], y_hbm.at[pl.ds(0, 1)], ssem.at[s]).wait()
            return carry
        lax.fori_loop(0, pend[s], body, 0)
        pend[s] = 0

    @pl.when(f == 0)
    def _():
        @pl.when(i == 0)
        def _():
            pend[0] = 0
            pend[1] = 0
            gather(src_ref, 0)

        @pl.when((i + 1 < nt) & (nreal_ref[jnp.minimum(i + 1, nt - 1)] > 0))
        def _():
            gather(srcn_ref, 1 - slot)

        @pl.when(is_valid)
        def _():
            pltpu.make_async_copy(h_hbm.at[pl.ds(0, tm)], xin.at[slot], gsem.at[slot]).wait()
            hb[...] = xin[slot].astype(BF16)
            acc[...] = jnp.zeros_like(acc)

    @pl.when(is_valid)
    def _():
        h = hb[...]
        gte = _dot(h, wg_ref[0])
        act = gte * (1.0 / (1.0 + jnp.exp(-gte))) * _dot(h, wu_ref[0])
        acc[...] += _dot(act.astype(BF16), wd_ref[0])

    @pl.when(is_valid & (f == last_f))
    def _():
        wait_scatter(slot)
        yout[slot] = acc[...] * gate_ref[...]

        def body(r, carry):
            pltpu.make_async_copy(yout.at[slot, pl.ds(r, 1)], y_hbm.at[pl.ds(dst_ref[0, 0, r], 1)],
                                  ssem.at[slot]).start()
            return carry
        lax.fori_loop(0, n_real, body, 0)
        pend[slot] = n_real

    @pl.when((i == nt - 1) & (f == last_f))
    def _():
        wait_scatter(0)
        wait_scatter(1)


def _moe(h, plan, wg, wu, wd, tm, tf):
    src, dst, gate, tile_e, n_real = plan
    n_tiles = src.shape[0]
    nf = wg.shape[2] // tf
    fser = lambda i, f: jnp.where(i % 2 == 0, f, nf - 1 - f)
    smem_tile = lambda imap: pl.BlockSpec((1, 1, tm), imap, memory_space=pltpu.SMEM)
    kern = functools.partial(_moe_kernel, tm=tm)
    return pl.pallas_call(
        kern,
        grid_spec=pltpu.PrefetchScalarGridSpec(
            num_scalar_prefetch=2,
            grid=(n_tiles, nf),
            in_specs=[smem_tile(lambda i, f, te, v: (i, 0, 0)),
                      smem_tile(lambda i, f, te, v: (jnp.minimum(i + 1, n_tiles - 1), 0, 0)),
                      smem_tile(lambda i, f, te, v: (i, 0, 0)),
                      pl.BlockSpec((tm, 1), lambda i, f, te, v: (i, 0)),
                      pl.BlockSpec(memory_space=pl.ANY),
                      pl.BlockSpec((1, D_MODEL, tf), lambda i, f, te, v: (te[i], 0, fser(i, f))),
                      pl.BlockSpec((1, D_MODEL, tf), lambda i, f, te, v: (te[i], 0, fser(i, f))),
                      pl.BlockSpec((1, tf, D_MODEL), lambda i, f, te, v: (te[i], fser(i, f), 0))],
            out_specs=pl.BlockSpec(memory_space=pl.ANY),
            scratch_shapes=[pltpu.VMEM((2, tm, D_MODEL), F32), pltpu.VMEM((tm, D_MODEL), BF16),
                            pltpu.VMEM((tm, D_MODEL), F32), pltpu.VMEM((2, tm, D_MODEL), F32),
                            pltpu.SemaphoreType.DMA((2,)), pltpu.SemaphoreType.DMA((2,)),
                            pltpu.SMEM((2,), jnp.int32)]),
        out_shape=jax.ShapeDtypeStruct((2 * h.shape[0], D_MODEL), F32),
        compiler_params=_params(("arbitrary", "arbitrary"), 56),
        name="moe_swiglu",
    )(tile_e, n_real, src, src, dst, gate, h, wg, wu, wd)


def _combine_kernel(x_ref, y0_ref, y1_ref, o_ref):
    o_ref[...] = x_ref[...] + (y0_ref[...] + y1_ref[...])


def _combine(x, y, row0, n_tok, tm):
    T = x.shape[0]
    row = pl.BlockSpec((tm, D_MODEL), lambda i: (i, 0))
    return pl.pallas_call(
        _combine_kernel,
        grid=(T // tm,),
        in_specs=[row, pl.BlockSpec((tm, D_MODEL), lambda i: (i + row0 // tm, 0)),
                  pl.BlockSpec((tm, D_MODEL), lambda i: (i + (n_tok + row0) // tm, 0))],
        out_specs=row,
        out_shape=jax.ShapeDtypeStruct((T, D_MODEL), F32),
        compiler_params=_params(("parallel",), 32),
        name="moe_combine",
    )(x, y, y)


def _tri(n, kind):
    r = lax.broadcasted_iota(jnp.int32, (n, n), 0)
    c = lax.broadcasted_iota(jnp.int32, (n, n), 1)
    m = {"lower_incl": r >= c, "upper_incl": r <= c, "lower_strict": r > c}[kind]
    return m.astype(BF16)


def _head_mask(n_heads, head_dim):
    h = lax.broadcasted_iota(jnp.int32, (n_heads, n_heads * head_dim), 0)
    c = lax.broadcasted_iota(jnp.int32, (n_heads, n_heads * head_dim), 1) // head_dim
    return (h == c).astype(F32)


def _stack(x, mem_k, mem_v, W, group):
    T = x.shape[0]
    tm = 512
    B, L = group["B"], group["L"]
    prompt = group["kind"] == "prompt"
    new = {}

    p, q, k32, kb, v32, vb, lf, lft = _in_even(x, W["g_mix0"], W["w_in_e"], W["wf"], W["wft"], W["bf"], W["bft"],
                                               W["g_fq"], W["g_fk"], W["bd64"], tm)
    ya, pool_state = _pool(group["pool_prefix"], p.reshape(B, L, MIX_HALF), W["w_pool"], W["pool_scale"],
                           ns=group["seq_per_step"], ch=group["chunk"], start_pos=group["start_pos"])
    if prompt:
        f, ft = _cumsum_prompt(lf, lft, W["tril256"], W["triu256"], B, L, 256)
        yb = _fox_prompt(q.astype(BF16), kb, vb, f, ft.reshape(N_HEADS // 2, 2, T), B, L, 256)
    else:
        lftn = lft.reshape(N_HEADS, B, L).transpose(1, 0, 2)
        yb = _fox_decode(group["page_table"], group["fox_k"], group["fox_v"], group["fox_lf"], q, k32, v32, lftn,
                         W["hm64"], W["eye8"])
    new.update(fox_k=k32, fox_v=v32, fox_logf=lf, pool=pool_state)
    x, qc = _mix_out(x, ya.reshape(T, MIX_HALF), yb, W["w_out_e"], W["g_cross0"], W["w_cq0"], W["g_cq0"], tm)
    x, hf = _cross(x, qc, mem_k[0], mem_v[0], W["w_co0"], W["g_ffn0"], ns=group["x_ns"], tq=group["x_tq"])
    x = _swiglu(hf, x, W["w_gate"], W["w_up"], W["w_down"], tm, 1408)

    u, q, k32, kb, v32, vb = _in_odd(x, W["g_mix1"], W["w_in_o"], tm)
    yc, conv_state = _conv(group["conv_prefix"], u.reshape(B, L, MIX_HALF), W["conv_w"], W["conv_b"], W["ln_g"],
                           W["ln_b"], ns=group["seq_per_step"], ch=group["conv_chunk"])
    if prompt:
        yd = _sb_prompt(q.astype(BF16), kb, vb, W["ustrict256"], B, L, 256)
    else:
        yd = _sb_decode(group["page_table"], group["sb_k"], group["sb_v"], q, k32, v32, W["hm64"], W["ustrict128"])
    new.update(sb_k=k32, sb_v=v32, conv=conv_state)
    x, qc = _mix_out(x, yc.reshape(T, MIX_HALF), yd, W["w_out_o"], W["g_cross1"], W["w_cq1"], W["g_cq1"], tm)
    x, hf = _cross(x, qc, mem_k[1], mem_v[1], W["w_co1"], W["g_ffn1"], ns=group["x_ns"], tq=group["x_tq"])
    r = _router(hf, W["w_router"], W["b_router"], tm)
    return x, hf, r, new


def kernel(x_prompt, x_sample, cache_fox_k, cache_fox_v, cache_fox_logf, cache_sb_k, cache_sb_v, cache_mem_k, cache_mem_v, state_pool, state_conv, page_table, mem_prompt, norm_mix, norm_cross, norm_ffn, w_cq, w_ck, w_cv, w_co, g_cq, g_ck, w_in_e, b_f, w_pool_grp, pool_scale, g_fq, g_fk, w_out_e, w_gate, w_up, w_down, w_in_o, conv_w, conv_b, ln_g, ln_b, w_out_o, w_router, b_router, we_gate, we_up, we_down):
    B, L, D = x_prompt.shape
    SB, SL, _ = x_sample.shape
    n_pool = cache_fox_k.shape[1]
    bf = lambda a: a.astype(BF16)
    row = lambda a: a.reshape(1, -1)

    wf = jnp.pad(w_in_e[0][:, 4 * MIX_HALF:], ((0, 0), (0, LANES - N_HEADS)))
    W = dict(
        g_mix0=row(norm_mix[0]), g_mix1=row(norm_mix[1]), g_cross0=row(norm_cross[0]), g_cross1=row(norm_cross[1]),
        g_ffn0=row(norm_ffn[0]), g_ffn1=row(norm_ffn[1]),
        w_in_e=bf(w_in_e[0][:, :4 * MIX_HALF]), wf=bf(wf), wft=bf(w_in_e[0][:, 4 * MIX_HALF:].T),
        bf=jnp.pad(row(b_f[0]), ((0, 0), (0, LANES - N_HEADS))), bft=b_f[0].reshape(N_HEADS, 1),
        g_fq=row(jnp.tile(g_fq[0], N_HEADS)), g_fk=row(jnp.tile(g_fk[0], N_HEADS)),
        w_pool=bf(w_pool_grp[0]), pool_scale=row(pool_scale[0]), w_out_e=bf(w_out_e[0]),
        w_cq0=bf(w_cq[0]), w_cq1=bf(w_cq[1]), w_co0=bf(w_co[0]), w_co1=bf(w_co[1]),
        g_cq0=row(jnp.tile(g_cq[0], X_HEADS)), g_cq1=row(jnp.tile(g_cq[1], X_HEADS)),
        w_gate=bf(w_gate[0]), w_up=bf(w_up[0]), w_down=bf(w_down[0]),
        w_in_o=bf(w_in_o[0]), conv_w=conv_w[0], conv_b=row(conv_b[0]), ln_g=row(ln_g[0]), ln_b=row(ln_b[0]),
        w_out_o=bf(w_out_o[0]),
        w_router=jnp.pad(w_router[0], ((0, 0), (0, LANES - N_EXPERTS))),
        b_router=jnp.pad(row(b_router[0]), ((0, 0), (0, LANES - N_EXPERTS))),
        we_gate=bf(we_gate[0]), we_up=bf(we_up[0]), we_down=bf(we_down[0]),
        bd64=jnp.kron(jnp.eye(N_HEADS, dtype=F32), jnp.ones((HEAD_DIM, HEAD_DIM), F32)).astype(BF16),
        hm64=_head_mask(N_HEADS, HEAD_DIM), eye8=jnp.eye(N_HEADS, dtype=BF16),
        tril256=_tri(256, "lower_incl"), triu256=_tri(256, "upper_incl"),
        ustrict256=_tri(256, "lower_strict"), ustrict128=_tri(128, "lower_strict"),
    )

    mk, mv = _mem_kv(mem_prompt.reshape(B * N_MEM, D), bf(w_ck), bf(w_cv),
                     jnp.tile(g_ck, (1, X_HEADS)).reshape(-1, 1, X_WIDTH), 512)
    mk = mk.reshape(-1, B, N_MEM, X_WIDTH)
    mv = mv.reshape(-1, B, N_MEM, X_WIDTH)

    prompt = dict(kind="prompt", B=B, L=L, start_pos=0, seq_per_step=1, chunk=256, conv_chunk=128,
                  pool_prefix=jnp.zeros((B, POOL_BUF, MIX_HALF), F32),
                  conv_prefix=jnp.zeros((B, CONV_BUF, MIX_HALF), F32), x_ns=1, x_tq=512)
    xp, hp, rp, sp = _stack(x_prompt.reshape(B * L, D), mk, mv, W, prompt)

    sample = dict(kind="sample", B=SB, L=SL, start_pos=page_table.shape[1] * PAGE, seq_per_step=16,
                  chunk=SL, conv_chunk=SL, pool_prefix=state_pool[0], conv_prefix=state_conv[0], x_ns=8, x_tq=SL,
                  page_table=page_table,
                  fox_k=cache_fox_k[0].reshape(n_pool, PAGE, MIX_HALF),
                  fox_v=cache_fox_v[0].reshape(n_pool, PAGE, MIX_HALF),
                  fox_lf=cache_fox_logf[0],
                  sb_k=cache_sb_k[0].reshape(n_pool, PAGE, MIX_HALF),
                  sb_v=cache_sb_v[0].reshape(n_pool, PAGE, MIX_HALF))
    smk = cache_mem_k.reshape(-1, SB, N_MEM, X_WIDTH)
    smv = cache_mem_v.reshape(-1, SB, N_MEM, X_WIDTH)
    xs, hs, rs, ss = _stack(x_sample.reshape(SB * SL, D), smk, smv, W, sample)

    n_tok = B * L + SB * SL
    plan = _moe_plan(jnp.concatenate([rp, rs], axis=0), MOE_TM)
    y2 = _moe(jnp.concatenate([hp, hs], axis=0), plan, W["we_gate"], W["we_up"], W["we_down"], MOE_TM, MOE_TF)
    yp = _combine(xp, y2, 0, n_tok, 512)
    ys = _combine(xs, y2, B * L, n_tok, 512)

    def outs(new, b, l):
        kv = lambda a: a.reshape(1, b, l, N_HEADS, HEAD_DIM)
        return (kv(new["fox_k"]), kv(new["fox_v"]), new["fox_logf"].reshape(1, b, l, N_HEADS),
                kv(new["sb_k"]), kv(new["sb_v"]), new["pool"][None], new["conv"][None])

    op = outs(sp, B, L)
    os_ = outs(ss, SB, SL)
    memk = mk.reshape(-1, B, N_MEM, X_HEADS, X_HEAD_DIM)
    memv = mv.reshape(-1, B, N_MEM, X_HEADS, X_HEAD_DIM)
    return (yp.reshape(B, L, D), ys.reshape(SB, SL, D), *op, memk, memv, *os_)
```

```python
import functools

import jax
import jax.numpy as jnp
from jax import lax
from jax.experimental import pallas as pl
from jax.experimental.pallas import tpu as pltpu

F32 = jnp.float32
BF16 = jnp.bfloat16

D_MODEL = 1024
MIX_HALF = 512
HEAD_DIM = 64
N_HEADS = 8
POOL_WINDOWS = (2, 4, 8, 16)
POOL_GROUP = 128
POOL_BUF = 15
CONV_WIDTH = 31
CONV_BUF = 30
N_MEM = 256
X_HEADS = 4
X_HEAD_DIM = 128
X_WIDTH = 512
N_EXPERTS = 8
PAGE = 128
EPS = 1e-6
ATT_SCALE = HEAD_DIM ** -0.5
NEG_BIG = -1e30

LANES = 128
SUBLANES = 8
MIB = 1024 * 1024


def _params(sem, vmem_mib):
    return pltpu.CompilerParams(dimension_semantics=sem, vmem_limit_bytes=vmem_mib * MIB)


def _full(shape):
    return pl.BlockSpec(shape, lambda *_: (0,) * len(shape))


def _rms(x, g):
    return x * lax.rsqrt(jnp.mean(x * x, axis=-1, keepdims=True) + EPS) * g


def _split2(x):
    hi = x.astype(BF16)
    lo = (x - hi.astype(F32)).astype(BF16)
    return hi, lo


def _split3(x):
    hi = x.astype(BF16)
    r = x - hi.astype(F32)
    mid = r.astype(BF16)
    lo = (r - mid.astype(F32)).astype(BF16)
    return hi, mid, lo


def _dot(a, b):
    return jnp.dot(a, b, preferred_element_type=F32)


def _dot_nt(a, b):
    return lax.dot_general(a, b, (((1,), (1,)), ((), ())), preferred_element_type=F32)


def _softplus(z):
    return jnp.maximum(z, 0.0) + jnp.log1p(jnp.exp(-jnp.abs(z)))


def _head_rms(a, g, bd, head_dim):
    hi, lo = _split2(a * a)
    ssq = _dot(hi, bd) + _dot(lo, bd)
    return a * lax.rsqrt(ssq * (1.0 / head_dim) + EPS) * g


def _heads_spec(tm):
    return pl.BlockSpec((tm, N_HEADS, HEAD_DIM), lambda i: (i, 0, 0))


def _store_heads(ref, a):
    for h in range(N_HEADS):
        ref[:, h, :] = a[:, h * HEAD_DIM:(h + 1) * HEAD_DIM]


def _in_even_kernel(x_ref, g_ref, w_ref, wf_ref, wft_ref, bf_ref, bft_ref, gq_ref, gk_ref, bd_ref,
                    p_ref, q_ref, k32_ref, kb_ref, v32_ref, vb_ref, lf_ref, lft_ref):
    h = _rms(x_ref[...], g_ref[...]).astype(BF16)
    bd = bd_ref[...]
    p_ref[...] = _dot(h, w_ref[:, 0:MIX_HALF])
    q = _head_rms(_dot(h, w_ref[:, MIX_HALF:2 * MIX_HALF]), gq_ref[...], bd, HEAD_DIM)
    q_ref[...] = q
    k = _head_rms(_dot(h, w_ref[:, 2 * MIX_HALF:3 * MIX_HALF]), gk_ref[...], bd, HEAD_DIM)
    _store_heads(k32_ref, k)
    kb_ref[...] = k.astype(BF16)
    v = _dot(h, w_ref[:, 3 * MIX_HALF:4 * MIX_HALF])
    _store_heads(v32_ref, v)
    vb_ref[...] = v.astype(BF16)
    fl = _dot(h, wf_ref[...]) + bf_ref[...]
    lf_ref[...] = (-_softplus(-fl))[:, 0:N_HEADS]
    flt = _dot_nt(wft_ref[...], h) + bft_ref[...]
    lft_ref[...] = -_softplus(-flt)


def _in_even(x, g, w, wf, wft, bf, bft, gq, gk, bd, tm):
    T = x.shape[0]
    row = lambda n: pl.BlockSpec((tm, n), lambda i: (i, 0))
    sds = jax.ShapeDtypeStruct
    return pl.pallas_call(
        _in_even_kernel,
        grid=(T // tm,),
        in_specs=[row(D_MODEL), _full((1, D_MODEL)), _full((D_MODEL, 4 * MIX_HALF)), _full((D_MODEL, LANES)),
                  _full((N_HEADS, D_MODEL)), _full((1, LANES)), _full((N_HEADS, 1)), _full((1, MIX_HALF)),
                  _full((1, MIX_HALF)), _full((MIX_HALF, MIX_HALF))],
        out_specs=[row(MIX_HALF), row(MIX_HALF), _heads_spec(tm), row(MIX_HALF), _heads_spec(tm), row(MIX_HALF),
                   row(N_HEADS), pl.BlockSpec((N_HEADS, tm), lambda i: (0, i))],
        out_shape=[sds((T, MIX_HALF), F32), sds((T, MIX_HALF), F32), sds((T, N_HEADS, HEAD_DIM), F32),
                   sds((T, MIX_HALF), BF16), sds((T, N_HEADS, HEAD_DIM), F32), sds((T, MIX_HALF), BF16),
                   sds((T, N_HEADS), F32), sds((N_HEADS, T), F32)],
        compiler_params=_params(("parallel",), 48),
        name="in_even",
    )(x, g, w, wf, wft, bf, bft, gq, gk, bd)


def _in_odd_kernel(x_ref, g_ref, w_ref, u_ref, q_ref, k32_ref, kb_ref, v32_ref, vb_ref):
    h = _rms(x_ref[...], g_ref[...]).astype(BF16)
    a = _dot(h, w_ref[:, 0:MIX_HALF])
    gate = _dot(h, w_ref[:, MIX_HALF:2 * MIX_HALF])
    u_ref[...] = a * (1.0 / (1.0 + jnp.exp(-gate)))
    q_ref[...] = _dot(h, w_ref[:, 2 * MIX_HALF:3 * MIX_HALF])
    k = _dot(h, w_ref[:, 3 * MIX_HALF:4 * MIX_HALF])
    _store_heads(k32_ref, k)
    kb_ref[...] = k.astype(BF16)
    v = _dot(h, w_ref[:, 4 * MIX_HALF:5 * MIX_HALF])
    _store_heads(v32_ref, v)
    vb_ref[...] = v.astype(BF16)


def _in_odd(x, g, w, tm):
    T = x.shape[0]
    row = lambda n: pl.BlockSpec((tm, n), lambda i: (i, 0))
    sds = jax.ShapeDtypeStruct
    return pl.pallas_call(
        _in_odd_kernel,
        grid=(T // tm,),
        in_specs=[row(D_MODEL), _full((1, D_MODEL)), _full((D_MODEL, 5 * MIX_HALF))],
        out_specs=[row(MIX_HALF), row(MIX_HALF), _heads_spec(tm), row(MIX_HALF), _heads_spec(tm), row(MIX_HALF)],
        out_shape=[sds((T, MIX_HALF), F32), sds((T, MIX_HALF), F32), sds((T, N_HEADS, HEAD_DIM), F32),
                   sds((T, MIX_HALF), BF16), sds((T, N_HEADS, HEAD_DIM), F32), sds((T, MIX_HALF), BF16)],
        compiler_params=_params(("parallel",), 48),
        name="in_odd",
    )(x, g, w)


POOL_PAD = 16
CONV_PAD = 32


def _pool_kernel(pre_ref, p_ref, w_ref, sc_ref, y_ref, st_ref, buf, *, ns, L, ch, start_pos):
    for s in range(ns):
        buf[0:POOL_PAD - POOL_BUF, :] = jnp.zeros((POOL_PAD - POOL_BUF, MIX_HALF), F32)
        buf[POOL_PAD - POOL_BUF:POOL_PAD, :] = pre_ref[s]
        buf[POOL_PAD:POOL_PAD + L, :] = p_ref[s]
        st_ref[s] = buf[POOL_PAD + L - POOL_BUF:POOL_PAD + L, :]
        for c0 in range(0, L, ch):
            pos = start_pos + c0 + lax.broadcasted_iota(jnp.int32, (ch, POOL_GROUP), 0)
            outs = []
            for gi, w in enumerate(POOL_WINDOWS):
                ls = slice(gi * POOL_GROUP, (gi + 1) * POOL_GROUP)
                cur = buf[POOL_PAD + c0:POOL_PAD + c0 + ch, ls]
                acc = cur
                for i in range(1, w):
                    acc = acc + buf[POOL_PAD + c0 - i:POOL_PAD + c0 - i + ch, ls]
                cnt = jnp.minimum(pos + 1, w).astype(F32)
                d = acc / cnt - cur
                outs.append(_dot(d.astype(BF16), w_ref[gi]) * sc_ref[:, ls])
            y_ref[s, c0:c0 + ch, :] = jnp.concatenate(outs, axis=-1)


def _pool(prefix, p, w_grp, scale, ns, ch, start_pos):
    B, L, _ = p.shape
    kern = functools.partial(_pool_kernel, ns=ns, L=L, ch=ch, start_pos=start_pos)
    return pl.pallas_call(
        kern,
        grid=(B // ns,),
        in_specs=[pl.BlockSpec((ns, POOL_BUF, MIX_HALF), lambda i: (i, 0, 0)),
                  pl.BlockSpec((ns, L, MIX_HALF), lambda i: (i, 0, 0)),
                  _full((len(POOL_WINDOWS), POOL_GROUP, POOL_GROUP)), _full((1, MIX_HALF))],
        out_specs=[pl.BlockSpec((ns, L, MIX_HALF), lambda i: (i, 0, 0)),
                   pl.BlockSpec((ns, POOL_BUF, MIX_HALF), lambda i: (i, 0, 0))],
        out_shape=[jax.ShapeDtypeStruct((B, L, MIX_HALF), F32),
                   jax.ShapeDtypeStruct((B, POOL_BUF, MIX_HALF), F32)],
        scratch_shapes=[pltpu.VMEM((POOL_PAD + L, MIX_HALF), F32)],
        compiler_params=_params(("parallel",), 40),
        name="pool_mixer",
    )(prefix, p, w_grp, scale)


def _conv_kernel(pre_ref, u_ref, w_ref, b_ref, lg_ref, lb_ref, y_ref, st_ref, buf, *, ns, L, ch):
    for s in range(ns):
        buf[0:CONV_PAD - CONV_BUF, :] = jnp.zeros((CONV_PAD - CONV_BUF, MIX_HALF), F32)
        buf[CONV_PAD - CONV_BUF:CONV_PAD, :] = pre_ref[s]
        buf[CONV_PAD:CONV_PAD + L, :] = u_ref[s]
        st_ref[s] = buf[CONV_PAD + L - CONV_BUF:CONV_PAD + L, :]
        base = CONV_PAD - CONV_BUF
        for c0 in range(0, L, ch):
            acc = buf[base + c0:base + c0 + ch, :] * w_ref[0:1, :]
            for j in range(1, CONV_WIDTH):
                acc = acc + buf[base + c0 + j:base + c0 + j + ch, :] * w_ref[j:j + 1, :]
            y = acc + b_ref[...]
            mu = jnp.mean(y, axis=-1, keepdims=True)
            yc = y - mu
            var = jnp.mean(yc * yc, axis=-1, keepdims=True)
            yn = yc * lax.rsqrt(var + EPS) * lg_ref[...] + lb_ref[...]
            y_ref[s, c0:c0 + ch, :] = yn * (1.0 / (1.0 + jnp.exp(-yn)))


def _conv(prefix, u, conv_w, conv_b, ln_g, ln_b, ns, ch):
    B, L, _ = u.shape
    kern = functools.partial(_conv_kernel, ns=ns, L=L, ch=ch)
    return pl.pallas_call(
        kern,
        grid=(B // ns,),
        in_specs=[pl.BlockSpec((ns, CONV_BUF, MIX_HALF), lambda i: (i, 0, 0)),
                  pl.BlockSpec((ns, L, MIX_HALF), lambda i: (i, 0, 0)),
                  _full((CONV_WIDTH, MIX_HALF)), _full((1, MIX_HALF)), _full((1, MIX_HALF)), _full((1, MIX_HALF))],
        out_specs=[pl.BlockSpec((ns, L, MIX_HALF), lambda i: (i, 0, 0)),
                   pl.BlockSpec((ns, CONV_BUF, MIX_HALF), lambda i: (i, 0, 0))],
        out_shape=[jax.ShapeDtypeStruct((B, L, MIX_HALF), F32),
                   jax.ShapeDtypeStruct((B, CONV_BUF, MIX_HALF), F32)],
        scratch_shapes=[pltpu.VMEM((CONV_PAD + L, MIX_HALF), F32)],
        compiler_params=_params(("parallel",), 40),
        name="conv_module",
    )(prefix, u, conv_w, conv_b, ln_g, ln_b)


def _cumsum_prompt_kernel(lf_ref, lft_ref, tril_ref, triu_ref, f_ref, ft_ref, *, L, blk):
    c_col = jnp.zeros((1, N_HEADS), F32)
    c_row = jnp.zeros((N_HEADS, 1), F32)
    for b0 in range(0, L, blk):
        x = lf_ref[b0:b0 + blk, :]
        xt = lft_ref[:, b0:b0 + blk]
        f = c_col
        for part in _split3(x):
            f = f + _dot(tril_ref[...], part)
        ft = c_row
        for part in _split3(xt):
            ft = ft + _dot(part, triu_ref[...])
        f_ref[b0:b0 + blk, :] = f
        ft_ref[:, b0:b0 + blk] = ft
        c_col = f[blk - 1:blk, :]
        c_row = ft[:, blk - 1:blk]


def _cumsum_prompt(lf, lft, tril, triu, B, L, blk):
    kern = functools.partial(_cumsum_prompt_kernel, L=L, blk=blk)
    return pl.pallas_call(
        kern,
        grid=(B,),
        in_specs=[pl.BlockSpec((L, N_HEADS), lambda b: (b, 0)), pl.BlockSpec((N_HEADS, L), lambda b: (0, b)),
                  _full((blk, blk)), _full((blk, blk))],
        out_specs=[pl.BlockSpec((L, N_HEADS), lambda b: (b, 0)), pl.BlockSpec((N_HEADS, L), lambda b: (0, b))],
        out_shape=[jax.ShapeDtypeStruct((B * L, N_HEADS), F32), jax.ShapeDtypeStruct((N_HEADS, B * L), F32)],
        compiler_params=_params(("parallel",), 32),
        name="fox_cumsum_prompt",
    )(lf, lft, tril, triu)


def _head_pair(q):
    lane = lax.broadcasted_iota(jnp.int32, q.shape, 1)
    zero = jnp.zeros_like(q)
    scale = jnp.asarray(ATT_SCALE, q.dtype)
    return [jnp.where(lane < HEAD_DIM, q, zero) * scale, jnp.where(lane >= HEAD_DIM, q, zero) * scale]


def _stack_heads(q):
    return jnp.concatenate(_head_pair(q), axis=0)


SB_DEAD = -105.0


def _fox_prompt_kernel(q_ref, k_ref, v_ref, f_ref, ft_ref, o_ref, *, blk):
    hp = pl.program_id(1)
    qi = pl.program_id(2)
    qs = _stack_heads(q_ref[...])
    f = f_ref[...]
    lane8 = lax.broadcasted_iota(jnp.int32, f.shape, 1)
    fq = [jnp.sum(jnp.where(lane8 == 2 * hp + i, f, 0.0), axis=1, keepdims=True) for i in range(2)]
    row = lax.broadcasted_iota(jnp.int32, (2 * blk, blk), 0) & (blk - 1)
    col = lax.broadcasted_iota(jnp.int32, (2 * blk, blk), 1)

    def block(ki, carry, diagonal):
        m, l, acc = carry
        ks = pl.multiple_of(ki * blk, blk)
        s = _dot_nt(qs, k_ref[pl.ds(ks, blk), :])
        s = jnp.concatenate([s[i * blk:(i + 1) * blk] + (fq[i] - ft_ref[0, i:i + 1, pl.ds(ks, blk)])
                             for i in range(2)], axis=0)
        if diagonal:
            s = jnp.where(col <= row, s, -jnp.inf)
        m_new = jnp.maximum(m, jnp.max(s, axis=1, keepdims=True))
        alpha = jnp.exp(m - m_new)
        p = jnp.exp(s - m_new)
        l = alpha * l + jnp.sum(p, axis=1, keepdims=True)
        acc = alpha * acc + _dot(p.astype(BF16), v_ref[pl.ds(ks, blk), :])
        return m_new, l, acc

    init = (jnp.full((2 * blk, 1), NEG_BIG, F32), jnp.zeros((2 * blk, 1), F32), jnp.zeros((2 * blk, LANES), F32))
    carry = lax.fori_loop(0, qi, lambda ki, c: block(ki, c, False), init)
    _, l, acc = block(qi, carry, True)
    o = acc / l
    lane = lax.broadcasted_iota(jnp.int32, (blk, LANES), 1)
    o_ref[...] = jnp.where(lane < HEAD_DIM, o[0:blk], o[blk:2 * blk])


def _fox_prompt(q, k, v, f, ft, B, L, blk):
    nq = L // blk
    kern = functools.partial(_fox_prompt_kernel, blk=blk)
    return pl.pallas_call(
        kern,
        grid=(B, N_HEADS // 2, nq),
        in_specs=[pl.BlockSpec((blk, LANES), lambda b, h, i: (b * nq + i, h)),
                  pl.BlockSpec((L, LANES), lambda b, h, i: (b, h)),
                  pl.BlockSpec((L, LANES), lambda b, h, i: (b, h)),
                  pl.BlockSpec((blk, N_HEADS), lambda b, h, i: (b * nq + i, 0)),
                  pl.BlockSpec((1, 2, L), lambda b, h, i: (h, 0, b))],
        out_specs=pl.BlockSpec((blk, LANES), lambda b, h, i: (b * nq + i, h)),
        out_shape=jax.ShapeDtypeStruct((B * L, MIX_HALF), F32),
        compiler_params=_params(("parallel", "parallel", "arbitrary"), 32),
        name="fox_attention_prompt",
    )(q, k, v, f, ft)


def _sb_weights(z, c, u, mask):
    sp = _softplus(z)
    ln = -sp
    if mask is not None:
        ln = jnp.where(mask, ln, 0.0)
    hi, lo = _split2(ln)
    e = _dot(hi, u) + _dot(lo, u)
    a = jnp.exp((z - sp) + e + c)
    if mask is not None:
        a = jnp.where(mask, a, 0.0)
    return a, c + jnp.sum(ln, axis=1, keepdims=True)


def _sb_block(z, c, v, u, mask):
    a, c = _sb_weights(z, c, u, mask)
    return _dot(a.astype(BF16), v), c


def _sb_prompt_kernel(q_ref, k_ref, v_ref, u_ref, o_ref, *, blk):
    qi = pl.program_id(2)
    qs = _stack_heads(q_ref[...])
    row = lax.broadcasted_iota(jnp.int32, (2 * blk, blk), 0) & (blk - 1)
    col = lax.broadcasted_iota(jnp.int32, (2 * blk, blk), 1)
    u = u_ref[...]

    def block(ki, c, acc, diagonal):
        ks = pl.multiple_of(ki * blk, blk)
        z = _dot_nt(qs, k_ref[pl.ds(ks, blk), :])
        pv, c = _sb_block(z, c, v_ref[pl.ds(ks, blk), :], u, (col < row) if diagonal else None)
        return c, acc + pv

    c, acc = block(qi, jnp.zeros((2 * blk, 1), F32), jnp.zeros((2 * blk, LANES), F32), True)

    def cond(state):
        j, alive, _, _ = state
        return (j < qi) & (alive > 0)

    def body(state):
        j, _, c, acc = state
        c, acc = block(qi - 1 - j, c, acc, False)
        return j + 1, (jnp.max(c) > SB_DEAD).astype(jnp.int32), c, acc

    _, _, _, acc = lax.while_loop(cond, body, (jnp.int32(0), jnp.int32(1), c, acc))
    lane = lax.broadcasted_iota(jnp.int32, (blk, LANES), 1)
    o_ref[...] = jnp.where(lane < HEAD_DIM, acc[0:blk], acc[blk:2 * blk])


def _sb_prompt(q, k, v, u, B, L, blk):
    nq = L // blk
    kern = functools.partial(_sb_prompt_kernel, blk=blk)
    return pl.pallas_call(
        kern,
        grid=(B, N_HEADS // 2, nq),
        in_specs=[pl.BlockSpec((blk, LANES), lambda b, h, i: (b * nq + i, h)),
                  pl.BlockSpec((L, LANES), lambda b, h, i: (b, h)),
                  pl.BlockSpec((L, LANES), lambda b, h, i: (b, h)),
                  _full((blk, blk))],
        out_specs=pl.BlockSpec((blk, LANES), lambda b, h, i: (b * nq + i, h)),
        out_shape=jax.ShapeDtypeStruct((B * L, MIX_HALF), F32),
        compiler_params=_params(("parallel", "parallel", "arbitrary"), 32),
        name="sb_attention_prompt",
    )(q, k, v, u)


N_PAGES = 16
DEC_SEQ = 8
BD_ROWS = DEC_SEQ * N_HEADS


def _pad_new(x):
    return jnp.concatenate([x, jnp.zeros((PAGE - DEC_SEQ, x.shape[1]), x.dtype)], axis=0)


def _head_cols(h):
    return slice(h * HEAD_DIM, (h + 1) * HEAD_DIM)


def _page_head(pages, new_ref, j, h):
    if j < N_PAGES:
        return pages[j][0, :, h, :].astype(BF16)
    return _pad_new(new_ref[:, h, :]).astype(BF16)


def _new_key_mask(rows, strict):
    row = lax.broadcasted_iota(jnp.int32, (rows, PAGE), 0) & (DEC_SEQ - 1)
    col = lax.broadcasted_iota(jnp.int32, (rows, PAGE), 1)
    return (col < row) if strict else (col <= row)


def _cumsum_lanes(x):
    n = x.shape[1]
    lane = lax.broadcasted_iota(jnp.int32, x.shape, 1)
    k = 1
    while k < n:
        x = x + jnp.where(lane >= k, pltpu.roll(x, k, 1), 0.0)
        k *= 2
    return x


def _fox_decode_kernel(pt_ref, *refs):
    k_pages = refs[0:N_PAGES]
    v_pages = refs[N_PAGES:2 * N_PAGES]
    lf_pages = refs[2 * N_PAGES:3 * N_PAGES]
    q_ref, kn_ref, vn_ref, lftn_ref, eye_ref, o_ref = refs[3 * N_PAGES:]
    del pt_ref
    parts = []
    for j in range(N_PAGES):
        x = lf_pages[j][0]
        acc = jnp.zeros((N_HEADS, PAGE), F32)
        for part in _split3(x):
            acc = acc + _dot_nt(eye_ref[...], part)
        parts.append(acc)
    parts.append(jnp.concatenate([lftn_ref[0], jnp.zeros((N_HEADS, PAGE - DEC_SEQ), F32)], axis=1))
    fk = _cumsum_lanes(jnp.concatenate(parts, axis=1))
    fnew = fk[:, N_PAGES * PAGE:]
    diag = (lax.broadcasted_iota(jnp.int32, (DEC_SEQ, PAGE), 0) == lax.broadcasted_iota(jnp.int32, (DEC_SEQ, PAGE), 1))
    new_mask = _new_key_mask(DEC_SEQ, False)
    q = q_ref[...]
    for h in range(N_HEADS):
        qh = (q[:, _head_cols(h)] * ATT_SCALE).astype(BF16)
        fq = jnp.sum(jnp.where(diag, fnew[h:h + 1, :], 0.0), axis=1, keepdims=True)
        scores = []
        for j in range(N_PAGES + 1):
            s = _dot_nt(qh, _page_head(k_pages, kn_ref, j, h)) + (fq - fk[h:h + 1, j * PAGE:(j + 1) * PAGE])
            if j == N_PAGES:
                s = jnp.where(new_mask, s, -jnp.inf)
            scores.append(s)
        m = scores[0].max(axis=1, keepdims=True)
        for s in scores[1:]:
            m = jnp.maximum(m, s.max(axis=1, keepdims=True))
        l = jnp.zeros((DEC_SEQ, 1), F32)
        acc = jnp.zeros((DEC_SEQ, HEAD_DIM), F32)
        for j in range(N_PAGES + 1):
            p = jnp.exp(scores[j] - m)
            l = l + jnp.sum(p, axis=1, keepdims=True)
            acc = acc + _dot(p.astype(BF16), _page_head(v_pages, vn_ref, j, h))
        o_ref[:, _head_cols(h)] = acc / l


def _sb_decode_kernel(pt_ref, *refs):
    k_pages = refs[0:N_PAGES]
    v_pages = refs[N_PAGES:2 * N_PAGES]
    q_ref, kn_ref, vn_ref, u_ref, o_ref = refs[2 * N_PAGES:]
    del pt_ref
    q = q_ref[...]
    qh = [(q[:, _head_cols(h)] * ATT_SCALE).astype(BF16) for h in range(N_HEADS)]
    u = u_ref[...]
    new_mask = _new_key_mask(BD_ROWS, True)
    c = jnp.zeros((BD_ROWS, 1), F32)
    acc = [jnp.zeros((DEC_SEQ, HEAD_DIM), F32) for _ in range(N_HEADS)]
    for j in range(N_PAGES, -1, -1):
        z = jnp.concatenate([_dot_nt(qh[h], _page_head(k_pages, kn_ref, j, h)) for h in range(N_HEADS)], axis=0)
        a, c = _sb_weights(z, c, u, new_mask if j == N_PAGES else None)
        for h in range(N_HEADS):
            ah = a[h * DEC_SEQ:(h + 1) * DEC_SEQ].astype(BF16)
            acc[h] = acc[h] + _dot(ah, _page_head(v_pages, vn_ref, j, h))
    for h in range(N_HEADS):
        o_ref[:, _head_cols(h)] = acc[h]


def _new_kv_spec():
    return pl.BlockSpec((DEC_SEQ, N_HEADS, HEAD_DIM), lambda b, pt: (b, 0, 0))


def _page_specs(shape_tail, n):
    specs = []
    for j in range(n):
        specs.append(pl.BlockSpec((1,) + shape_tail, functools.partial(
            lambda b, pt, j: (pt[b, j],) + (0,) * len(shape_tail), j=j)))
    return specs


def _fox_decode(page_table, ck, cv, clf, q, kn, vn, lftn, eye):
    nb = page_table.shape[0]
    seq = lambda n: pl.BlockSpec((DEC_SEQ, n), lambda b, pt: (b, 0))
    const = lambda shape: pl.BlockSpec(shape, lambda b, pt: (0,) * len(shape))
    kv_page = (PAGE, N_HEADS, HEAD_DIM)
    in_specs = (_page_specs(kv_page, N_PAGES) + _page_specs(kv_page, N_PAGES) + _page_specs((PAGE, N_HEADS), N_PAGES)
                + [seq(MIX_HALF), _new_kv_spec(), _new_kv_spec(),
                   pl.BlockSpec((1, N_HEADS, DEC_SEQ), lambda b, pt: (b, 0, 0)), const((N_HEADS, N_HEADS))])
    return pl.pallas_call(
        _fox_decode_kernel,
        grid_spec=pltpu.PrefetchScalarGridSpec(
            num_scalar_prefetch=1, grid=(nb,), in_specs=in_specs, out_specs=seq(MIX_HALF)),
        out_shape=jax.ShapeDtypeStruct((nb * DEC_SEQ, MIX_HALF), F32),
        compiler_params=_params(("arbitrary",), 56),
        name="fox_attention_decode",
    )(page_table, *([ck] * N_PAGES), *([cv] * N_PAGES), *([clf] * N_PAGES), q, kn, vn, lftn, eye)


def _sb_decode(page_table, ck, cv, q, kn, vn, u):
    nb = page_table.shape[0]
    seq = lambda n: pl.BlockSpec((DEC_SEQ, n), lambda b, pt: (b, 0))
    const = lambda shape: pl.BlockSpec(shape, lambda b, pt: (0,) * len(shape))
    kv_page = (PAGE, N_HEADS, HEAD_DIM)
    in_specs = (_page_specs(kv_page, N_PAGES) + _page_specs(kv_page, N_PAGES)
                + [seq(MIX_HALF), _new_kv_spec(), _new_kv_spec(), const((PAGE, PAGE))])
    return pl.pallas_call(
        _sb_decode_kernel,
        grid_spec=pltpu.PrefetchScalarGridSpec(
            num_scalar_prefetch=1, grid=(nb,), in_specs=in_specs, out_specs=seq(MIX_HALF)),
        out_shape=jax.ShapeDtypeStruct((nb * DEC_SEQ, MIX_HALF), F32),
        compiler_params=_params(("arbitrary",), 56),
        name="sb_attention_decode",
    )(page_table, *([ck] * N_PAGES), *([cv] * N_PAGES), q, kn, vn, u)


def _x_head_rms(a, g):
    outs = []
    for h in range(X_HEADS):
        ah = a[:, h * X_HEAD_DIM:(h + 1) * X_HEAD_DIM]
        outs.append(ah * lax.rsqrt(jnp.mean(ah * ah, axis=-1, keepdims=True) + EPS))
    return jnp.concatenate(outs, axis=-1) * g


def _mix_out_kernel(x_ref, ya_ref, yb_ref, wo_ref, g_ref, wq_ref, gq_ref, x1_ref, qc_ref):
    y = _dot(ya_ref[...].astype(BF16), wo_ref[0:MIX_HALF, :]) + _dot(yb_ref[...].astype(BF16), wo_ref[MIX_HALF:, :])
    x1 = x_ref[...] + y
    x1_ref[...] = x1
    hc = _rms(x1, g_ref[...]).astype(BF16)
    qc_ref[...] = _x_head_rms(_dot(hc, wq_ref[...]), gq_ref[...])


def _mix_out(x, ya, yb, wo, g, wq, gq, tm):
    T = x.shape[0]
    row = lambda n: pl.BlockSpec((tm, n), lambda i: (i, 0))
    return pl.pallas_call(
        _mix_out_kernel,
        grid=(T // tm,),
        in_specs=[row(D_MODEL), row(MIX_HALF), row(MIX_HALF), _full((D_MODEL, D_MODEL)), _full((1, D_MODEL)),
                  _full((D_MODEL, X_WIDTH)), _full((1, X_WIDTH))],
        out_specs=[row(D_MODEL), row(X_WIDTH)],
        out_shape=[jax.ShapeDtypeStruct((T, D_MODEL), F32), jax.ShapeDtypeStruct((T, X_WIDTH), F32)],
        compiler_params=_params(("parallel",), 40),
        name="mixer_out_proj",
    )(x, ya, yb, wo, g, wq, gq)


def _mem_kv_kernel(m_ref, wk_ref, wv_ref, gk_ref, k_ref, v_ref, *, nb):
    m = m_ref[...].astype(BF16)
    k = _x_head_rms(_dot(m, wk_ref[0]), gk_ref[0])
    v = _dot(m, wv_ref[0])
    for b in range(nb):
        for h in range(X_HEADS):
            ls = slice(h * X_HEAD_DIM, (h + 1) * X_HEAD_DIM)
            k_ref[0, b, :, h, :] = k[b * N_MEM:(b + 1) * N_MEM, ls]
            v_ref[0, b, :, h, :] = v[b * N_MEM:(b + 1) * N_MEM, ls]


def _mem_kv(mem, wk, wv, gk, nb):
    B = mem.shape[0] // N_MEM
    depth = wk.shape[0]
    out = pl.BlockSpec((1, nb, N_MEM, X_HEADS, X_HEAD_DIM), lambda l, i: (l, i, 0, 0, 0))
    return pl.pallas_call(
        functools.partial(_mem_kv_kernel, nb=nb),
        grid=(depth, B // nb),
        in_specs=[pl.BlockSpec((nb * N_MEM, D_MODEL), lambda l, i: (i, 0)),
                  pl.BlockSpec((1, D_MODEL, X_WIDTH), lambda l, i: (l, 0, 0)),
                  pl.BlockSpec((1, D_MODEL, X_WIDTH), lambda l, i: (l, 0, 0)),
                  pl.BlockSpec((1, 1, X_WIDTH), lambda l, i: (l, 0, 0))],
        out_specs=[out, out],
        out_shape=[jax.ShapeDtypeStruct((depth, B, N_MEM, X_HEADS, X_HEAD_DIM), F32)] * 2,
        compiler_params=_params(("parallel", "parallel"), 32),
        name="memory_kv",
    )(mem, wk, wv, gk)


def _cross_kernel(x_ref, q_ref, mk_ref, mv_ref, wo_ref, g_ref, x2_ref, hf_ref, *, ns, tq):
    outs = []
    for s in range(ns):
        q = q_ref[s * tq:(s + 1) * tq, :]
        heads = []
        for h in range(X_HEADS):
            ls = slice(h * X_HEAD_DIM, (h + 1) * X_HEAD_DIM)
            qh = (q[:, ls] * (X_HEAD_DIM ** -0.5)).astype(BF16)
            sc = _dot_nt(qh, mk_ref[0, s, :, h, :].astype(BF16))
            p = jnp.exp(sc - jnp.max(sc, axis=1, keepdims=True))
            o = _dot(p.astype(BF16), mv_ref[0, s, :, h, :].astype(BF16))
            heads.append(o / jnp.sum(p, axis=1, keepdims=True))
        outs.append(jnp.concatenate(heads, axis=-1))
    o = outs[0] if ns == 1 else jnp.concatenate(outs, axis=0)
    x2 = x_ref[...] + _dot(o.astype(BF16), wo_ref[...])
    x2_ref[...] = x2
    hf_ref[...] = _rms(x2, g_ref[...])


def _cross(x, q, mk, mv, layer, wo, g, ns, tq):
    T = x.shape[0]
    L = T // mk.shape[1]
    nq = L // tq
    rows = ns * tq
    row = lambda n: pl.BlockSpec((rows, n), lambda i: (i, 0))
    mem = pl.BlockSpec((1, ns, N_MEM, X_HEADS, X_HEAD_DIM), lambda i: (layer, i // nq, 0, 0, 0))
    kern = functools.partial(_cross_kernel, ns=ns, tq=tq)
    return pl.pallas_call(
        kern,
        grid=(T // rows,),
        in_specs=[row(D_MODEL), row(X_WIDTH), mem, mem, _full((X_WIDTH, D_MODEL)), _full((1, D_MODEL))],
        out_specs=[row(D_MODEL), row(D_MODEL)],
        out_shape=[jax.ShapeDtypeStruct((T, D_MODEL), F32)] * 2,
        compiler_params=_params(("parallel",), 48),
        name="cross_attention",
    )(x, q, mk, mv, wo, g)


def _swiglu_kernel(h_ref, x_ref, wg_ref, wu_ref, wd_ref, o_ref, hb, acc):
    f = pl.program_id(1)

    @pl.when(f == 0)
    def _():
        hb[...] = h_ref[...].astype(BF16)
        acc[...] = jnp.zeros_like(acc)

    h = hb[...]
    gte = _dot(h, wg_ref[...])
    act = gte * (1.0 / (1.0 + jnp.exp(-gte))) * _dot(h, wu_ref[...])
    acc[...] += _dot(act.astype(BF16), wd_ref[...])

    @pl.when(f == pl.num_programs(1) - 1)
    def _():
        o_ref[...] = x_ref[...] + acc[...]


def _swiglu(h, x, wg, wu, wd, tm, tf):
    T = h.shape[0]
    dff = wg.shape[1]
    row = pl.BlockSpec((tm, D_MODEL), lambda i, f: (i, 0))
    return pl.pallas_call(
        _swiglu_kernel,
        grid=(T // tm, dff // tf),
        in_specs=[row, row, pl.BlockSpec((D_MODEL, tf), lambda i, f: (0, f)),
                  pl.BlockSpec((D_MODEL, tf), lambda i, f: (0, f)), pl.BlockSpec((tf, D_MODEL), lambda i, f: (f, 0))],
        out_specs=row,
        out_shape=jax.ShapeDtypeStruct((T, D_MODEL), F32),
        scratch_shapes=[pltpu.VMEM((tm, D_MODEL), BF16), pltpu.VMEM((tm, D_MODEL), F32)],
        compiler_params=_params(("parallel", "arbitrary"), 56),
        name="dense_swiglu",
    )(h, x, wg, wu, wd)


def _router_kernel(h_ref, w_ref, b_ref, r_ref):
    h_hi, h_lo = _split2(h_ref[...])
    w_hi, w_lo = _split2(w_ref[...])
    logits = _dot(h_hi, w_hi) + _dot(h_hi, w_lo) + _dot(h_lo, w_hi) + b_ref[...]
    lane = lax.broadcasted_iota(jnp.int32, logits.shape, 1)
    logits = jnp.where(lane < N_EXPERTS, logits, -jnp.inf)
    m1 = jnp.max(logits, axis=1, keepdims=True)
    i1 = jnp.min(jnp.where(logits == m1, lane, LANES), axis=1, keepdims=True)
    rest = jnp.where(lane == i1, -jnp.inf, logits)
    m2 = jnp.max(rest, axis=1, keepdims=True)
    i2 = jnp.min(jnp.where(rest == m2, lane, LANES), axis=1, keepdims=True)
    e2 = jnp.exp(m2 - m1)
    g1 = 1.0 / (1.0 + e2)
    g2 = e2 / (1.0 + e2)
    r_ref[...] = jnp.where(lane == 0, i1.astype(F32), jnp.where(lane == 1, i2.astype(F32),
                           jnp.where(lane == 2, g1, jnp.where(lane == 3, g2, 0.0))))


def _router(h, w, b, tm):
    T = h.shape[0]
    return pl.pallas_call(
        _router_kernel,
        grid=(T // tm,),
        in_specs=[pl.BlockSpec((tm, D_MODEL), lambda i: (i, 0)), _full((D_MODEL, LANES)), _full((1, LANES))],
        out_specs=pl.BlockSpec((tm, LANES), lambda i: (i, 0)),
        out_shape=jax.ShapeDtypeStruct((T, LANES), F32),
        compiler_params=_params(("parallel",), 32),
        name="moe_router",
    )(h, w, b)


MOE_TM = 512
MOE_NF = 2
COMBINE_TM = 512


def _moe_plan(r, tm):
    T = r.shape[0]
    n_pairs = 2 * T
    e = r[:, 0:2].astype(jnp.int32).T.reshape(-1)
    onehot = (e[:, None] == jnp.arange(N_EXPERTS, dtype=jnp.int32)[None, :]).astype(jnp.int32)
    csum = jnp.cumsum(onehot, axis=0)
    rank = jnp.sum(csum * onehot, axis=1) - 1
    psz = (csum[-1] + tm - 1) // tm * tm
    gend = jnp.cumsum(psz)
    pos = (gend - psz)[e] + rank
    n_rows = n_pairs + N_EXPERTS * tm
    n_tiles = n_rows // tm
    src = jnp.zeros((n_rows,), jnp.int32).at[pos].set(jnp.arange(n_pairs, dtype=jnp.int32) % T)
    t0 = jnp.arange(n_tiles, dtype=jnp.int32) * tm
    tile_e = jnp.minimum(jnp.sum((t0[:, None] >= gend[None, :]).astype(jnp.int32), axis=1), N_EXPERTS - 1)
    n_valid = (gend[-1] // tm).astype(jnp.int32).reshape(1)
    return src.reshape(n_tiles, 1, tm), pos.astype(jnp.int32), tile_e, n_valid


def _row_gather(src_hbm, idx_ref, dst, sem, r0, n, unrolled):
    def start(r):
        pltpu.make_async_copy(src_hbm.at[pl.ds(idx_ref[0, 0, r0 + r], 1)], dst.at[pl.ds(r0 + r, 1)], sem).start()

    if unrolled:
        for r in range(n):
            start(r)
    else:
        def body(r, carry):
            start(r)
            return carry
        lax.fori_loop(0, n, body, 0, unroll=8)


def _moe_kernel(te_ref, nv_ref, src_ref, srcn_ref, h_hbm, wg_ref, wu_ref, wd_ref, y_ref, xin, hb, acc, gsem, *, tm):
    del te_ref
    i = pl.program_id(0)
    f = pl.program_id(1)
    nt = pl.num_programs(0)
    nf = pl.num_programs(1)
    slot = i % 2
    is_valid = i < nv_ref[0]
    part = tm // MOE_NF
    r0 = f * part

    def wait_tile(s):
        pltpu.make_async_copy(h_hbm.at[pl.ds(0, tm)], xin.at[s], gsem.at[s]).wait()

    @pl.when((i == 0) & (f == 0))
    def _():
        _row_gather(h_hbm, src_ref, xin.at[0], gsem.at[0], 0, tm, False)

    @pl.when(f == 0)
    def _():
        wait_tile(slot)
        hb[...] = xin[slot].astype(BF16)
        acc[...] = jnp.zeros_like(acc)

    @pl.when(is_valid)
    def _():
        _row_gather(h_hbm, srcn_ref, xin.at[1 - slot], gsem.at[1 - slot], r0, part, True)
        h = hb[...]
        gte = _dot(h, wg_ref[0])
        act = gte * (1.0 / (1.0 + jnp.exp(-gte))) * _dot(h, wu_ref[0])
        acc[...] += _dot(act.astype(BF16), wd_ref[0])

    @pl.when(jnp.logical_not(is_valid))
    def _():
        _row_gather(h_hbm, srcn_ref, xin.at[1 - slot], gsem.at[1 - slot], r0, part, False)

    @pl.when(f == nf - 1)
    def _():
        y_ref[...] = acc[...]

    @pl.when((i == nt - 1) & (f == nf - 1))
    def _():
        wait_tile(1 - slot)


def _moe(h, src, tile_e, n_valid, wg, wu, wd, tm):
    n_tiles = src.shape[0]
    nf = MOE_NF
    tf = wg.shape[2] // nf
    fser = lambda i, f: jnp.where(i % 2 == 0, f, nf - 1 - f)
    smem_tile = lambda imap: pl.BlockSpec((1, 1, tm), imap, memory_space=pltpu.SMEM)
    kern = functools.partial(_moe_kernel, tm=tm)
    return pl.pallas_call(
        kern,
        grid_spec=pltpu.PrefetchScalarGridSpec(
            num_scalar_prefetch=2,
            grid=(n_tiles, nf),
            in_specs=[smem_tile(lambda i, f, te, nv: (i, 0, 0)),
                      smem_tile(lambda i, f, te, nv: (jnp.minimum(i + 1, n_tiles - 1), 0, 0)),
                      pl.BlockSpec(memory_space=pl.ANY),
                      pl.BlockSpec((1, D_MODEL, tf), lambda i, f, te, nv: (te[i], 0, fser(i, f))),
                      pl.BlockSpec((1, D_MODEL, tf), lambda i, f, te, nv: (te[i], 0, fser(i, f))),
                      pl.BlockSpec((1, tf, D_MODEL), lambda i, f, te, nv: (te[i], fser(i, f), 0))],
            out_specs=pl.BlockSpec((tm, D_MODEL), lambda i, f, te, nv: (i, 0)),
            scratch_shapes=[pltpu.VMEM((2, tm, D_MODEL), F32), pltpu.VMEM((tm, D_MODEL), BF16),
                            pltpu.VMEM((tm, D_MODEL), F32), pltpu.SemaphoreType.DMA((2,))]),
        out_shape=jax.ShapeDtypeStruct((n_tiles * tm, D_MODEL), F32),
        compiler_params=_params(("arbitrary", "arbitrary"), 56),
        name="moe_swiglu",
    )(tile_e, n_valid, src, src, h, wg, wu, wd)


def _combine_kernel(pos_ref, posn_ref, x_ref, r_ref, y_hbm, o_ref, buf, sem, *, tm):
    i = pl.program_id(0)
    nt = pl.num_programs(0)
    slot = i % 2

    @pl.when(i == 0)
    def _():
        _row_gather(y_hbm, pos_ref, buf.at[0], sem.at[0], 0, 2 * tm, False)

    @pl.when(i + 1 < nt)
    def _():
        _row_gather(y_hbm, posn_ref, buf.at[1 - slot], sem.at[1 - slot], 0, 2 * tm, False)

    pltpu.make_async_copy(y_hbm.at[pl.ds(0, 2 * tm)], buf.at[slot], sem.at[slot]).wait()
    r = r_ref[...]
    o_ref[...] = x_ref[...] + (r[:, 2:3] * buf[slot, 0:tm, :] + r[:, 3:4] * buf[slot, tm:2 * tm, :])


def _combine(x, r, pos, y, tm):
    T = x.shape[0]
    nt = T // tm
    smem_tile = lambda imap: pl.BlockSpec((1, 1, 2 * tm), imap, memory_space=pltpu.SMEM)
    return pl.pallas_call(
        functools.partial(_combine_kernel, tm=tm),
        grid=(nt,),
        in_specs=[smem_tile(lambda i: (i, 0, 0)), smem_tile(lambda i: (jnp.minimum(i + 1, nt - 1), 0, 0)),
                  pl.BlockSpec((tm, D_MODEL), lambda i: (i, 0)), pl.BlockSpec((tm, LANES), lambda i: (i, 0)),
                  pl.BlockSpec(memory_space=pl.ANY)],
        out_specs=pl.BlockSpec((tm, D_MODEL), lambda i: (i, 0)),
        out_shape=jax.ShapeDtypeStruct((T, D_MODEL), F32),
        scratch_shapes=[pltpu.VMEM((2, 2 * tm, D_MODEL), F32), pltpu.SemaphoreType.DMA((2,))],
        compiler_params=_params(("arbitrary",), 40),
        name="moe_combine",
    )(pos, pos, x, r, y)


def _tri(n, kind):
    r = lax.broadcasted_iota(jnp.int32, (n, n), 0)
    c = lax.broadcasted_iota(jnp.int32, (n, n), 1)
    m = {"lower_incl": r >= c, "upper_incl": r <= c, "lower_strict": r > c}[kind]
    return m.astype(BF16)


def _head_mask(n_heads, head_dim):
    h = lax.broadcasted_iota(jnp.int32, (n_heads, n_heads * head_dim), 0)
    c = lax.broadcasted_iota(jnp.int32, (n_heads, n_heads * head_dim), 1) // head_dim
    return (h == c).astype(F32)


def _stack(x, mem_k, mem_v, W, group):
    T = x.shape[0]
    tm = 512
    B, L = group["B"], group["L"]
    prompt = group["kind"] == "prompt"
    new = {}

    p, q, k32, kb, v32, vb, lf, lft = _in_even(x, W["g_mix0"], W["w_in_e"], W["wf"], W["wft"], W["bf"], W["bft"],
                                               W["g_fq"], W["g_fk"], W["bd64"], tm)
    ya, pool_state = _pool(group["pool_prefix"], p.reshape(B, L, MIX_HALF), W["w_pool"], W["pool_scale"],
                           ns=group["seq_per_step"], ch=group["chunk"], start_pos=group["start_pos"])
    if prompt:
        f, ft = _cumsum_prompt(lf, lft, W["tril256"], W["triu256"], B, L, 256)
        yb = _fox_prompt(q.astype(BF16), kb, vb, f, ft.reshape(N_HEADS // 2, 2, T), B, L, 256)
    else:
        lftn = lft.reshape(N_HEADS, B, L).transpose(1, 0, 2)
        yb = _fox_decode(group["page_table"], group["fox_k"], group["fox_v"], group["fox_lf"], q, k32, v32, lftn,
                         W["eye8"])
    new.update(fox_k=k32, fox_v=v32, fox_logf=lf, pool=pool_state)
    x, qc = _mix_out(x, ya.reshape(T, MIX_HALF), yb, W["w_out_e"], W["g_cross0"], W["w_cq0"], W["g_cq0"], tm)
    x, hf = _cross(x, qc, mem_k, mem_v, 0, W["w_co0"], W["g_ffn0"], ns=group["x_ns"], tq=group["x_tq"])
    x = _swiglu(hf, x, W["w_gate"], W["w_up"], W["w_down"], tm, 1408)

    u, q, k32, kb, v32, vb = _in_odd(x, W["g_mix1"], W["w_in_o"], tm)
    yc, conv_state = _conv(group["conv_prefix"], u.reshape(B, L, MIX_HALF), W["conv_w"], W["conv_b"], W["ln_g"],
                           W["ln_b"], ns=group["seq_per_step"], ch=group["conv_chunk"])
    if prompt:
        yd = _sb_prompt(q.astype(BF16), kb, vb, W["ustrict256"], B, L, 256)
    else:
        yd = _sb_decode(group["page_table"], group["sb_k"], group["sb_v"], q, k32, v32, W["ustrict128"])
    new.update(sb_k=k32, sb_v=v32, conv=conv_state)
    x, qc = _mix_out(x, yc.reshape(T, MIX_HALF), yd, W["w_out_o"], W["g_cross1"], W["w_cq1"], W["g_cq1"], tm)
    x, hf = _cross(x, qc, mem_k, mem_v, 1, W["w_co1"], W["g_ffn1"], ns=group["x_ns"], tq=group["x_tq"])
    r = _router(hf, W["w_router"], W["b_router"], tm)
    return x, hf, r, new


def kernel(x_prompt, x_sample, cache_fox_k, cache_fox_v, cache_fox_logf, cache_sb_k, cache_sb_v, cache_mem_k, cache_mem_v, state_pool, state_conv, page_table, mem_prompt, norm_mix, norm_cross, norm_ffn, w_cq, w_ck, w_cv, w_co, g_cq, g_ck, w_in_e, b_f, w_pool_grp, pool_scale, g_fq, g_fk, w_out_e, w_gate, w_up, w_down, w_in_o, conv_w, conv_b, ln_g, ln_b, w_out_o, w_router, b_router, we_gate, we_up, we_down):
    B, L, D = x_prompt.shape
    SB, SL, _ = x_sample.shape
    n_pool = cache_fox_k.shape[1]
    bf = lambda a: a.astype(BF16)
    row = lambda a: a.reshape(1, -1)

    wf = jnp.pad(w_in_e[0][:, 4 * MIX_HALF:], ((0, 0), (0, LANES - N_HEADS)))
    W = dict(
        g_mix0=row(norm_mix[0]), g_mix1=row(norm_mix[1]), g_cross0=row(norm_cross[0]), g_cross1=row(norm_cross[1]),
        g_ffn0=row(norm_ffn[0]), g_ffn1=row(norm_ffn[1]),
        w_in_e=bf(w_in_e[0][:, :4 * MIX_HALF]), wf=bf(wf), wft=bf(w_in_e[0][:, 4 * MIX_HALF:].T),
        bf=jnp.pad(row(b_f[0]), ((0, 0), (0, LANES - N_HEADS))), bft=b_f[0].reshape(N_HEADS, 1),
        g_fq=row(jnp.tile(g_fq[0], N_HEADS)), g_fk=row(jnp.tile(g_fk[0], N_HEADS)),
        w_pool=bf(w_pool_grp[0]), pool_scale=row(pool_scale[0]), w_out_e=bf(w_out_e[0]),
        w_cq0=bf(w_cq[0]), w_cq1=bf(w_cq[1]), w_co0=bf(w_co[0]), w_co1=bf(w_co[1]),
        g_cq0=row(jnp.tile(g_cq[0], X_HEADS)), g_cq1=row(jnp.tile(g_cq[1], X_HEADS)),
        w_gate=bf(w_gate[0]), w_up=bf(w_up[0]), w_down=bf(w_down[0]),
        w_in_o=bf(w_in_o[0]), conv_w=conv_w[0], conv_b=row(conv_b[0]), ln_g=row(ln_g[0]), ln_b=row(ln_b[0]),
        w_out_o=bf(w_out_o[0]),
        w_router=jnp.pad(w_router[0], ((0, 0), (0, LANES - N_EXPERTS))),
        b_router=jnp.pad(row(b_router[0]), ((0, 0), (0, LANES - N_EXPERTS))),
        we_gate=bf(we_gate[0]), we_up=bf(we_up[0]), we_down=bf(we_down[0]),
        bd64=jnp.kron(jnp.eye(N_HEADS, dtype=F32), jnp.ones((HEAD_DIM, HEAD_DIM), F32)).astype(BF16),
        hm64=_head_mask(N_HEADS, HEAD_DIM), eye8=jnp.eye(N_HEADS, dtype=BF16),
        tril256=_tri(256, "lower_incl"), triu256=_tri(256, "upper_incl"),
        ustrict256=_tri(256, "lower_strict"), ustrict128=_tri(128, "lower_strict"),
    )

    mk, mv = _mem_kv(mem_prompt.reshape(B * N_MEM, D), bf(w_ck), bf(w_cv),
                     jnp.tile(g_ck, (1, X_HEADS)).reshape(-1, 1, X_WIDTH), 2)

    prompt = dict(kind="prompt", B=B, L=L, start_pos=0, seq_per_step=1, chunk=256, conv_chunk=128,
                  pool_prefix=jnp.zeros((B, POOL_BUF, MIX_HALF), F32),
                  conv_prefix=jnp.zeros((B, CONV_BUF, MIX_HALF), F32), x_ns=1, x_tq=512)
    xp, hp, rp, sp = _stack(x_prompt.reshape(B * L, D), mk, mv, W, prompt)

    sample = dict(kind="sample", B=SB, L=SL, start_pos=page_table.shape[1] * PAGE, seq_per_step=16,
                  chunk=SL, conv_chunk=SL, pool_prefix=state_pool[0], conv_prefix=state_conv[0], x_ns=8, x_tq=SL,
                  page_table=page_table,
                  fox_k=cache_fox_k[0], fox_v=cache_fox_v[0], fox_lf=cache_fox_logf[0],
                  sb_k=cache_sb_k[0], sb_v=cache_sb_v[0])
    xs, hs, rs, ss = _stack(x_sample.reshape(SB * SL, D), cache_mem_k, cache_mem_v, W, sample)

    n_tok = B * L + SB * SL
    src, pos, tile_e, n_valid = _moe_plan(jnp.concatenate([rp, rs], axis=0), MOE_TM)
    ye = _moe(jnp.concatenate([hp, hs], axis=0), src, tile_e, n_valid, W["we_gate"], W["we_up"], W["we_down"], MOE_TM)

    def pair_rows(row0, n):
        both = [pos[s * n_tok + row0:s * n_tok + row0 + n].reshape(n // COMBINE_TM, COMBINE_TM) for s in range(2)]
        return jnp.concatenate(both, axis=1).reshape(n // COMBINE_TM, 1, 2 * COMBINE_TM)

    yp = _combine(xp, rp, pair_rows(0, B * L), ye, COMBINE_TM)
    ys = _combine(xs, rs, pair_rows(B * L, SB * SL), ye, COMBINE_TM)

    def outs(new, b, l):
        kv = lambda a: a.reshape(1, b, l, N_HEADS, HEAD_DIM)
        return (kv(new["fox_k"]), kv(new["fox_v"]), new["fox_logf"].reshape(1, b, l, N_HEADS),
                kv(new["sb_k"]), kv(new["sb_v"]), new["pool"][None], new["conv"][None])

    op = outs(sp, B, L)
    os_ = outs(ss, SB, SL)
    return (yp.reshape(B, L, D), ys.reshape(SB, SL, D), *op, mk, mv, *os_)
```

```python
import functools

import jax
import jax.numpy as jnp
from jax import lax
from jax.experimental import pallas as pl
from jax.experimental.pallas import tpu as pltpu

F32 = jnp.float32
BF16 = jnp.bfloat16

D_MODEL = 1024
MIX_HALF = 512
HEAD_DIM = 64
N_HEADS = 8
POOL_WINDOWS = (2, 4, 8, 16)
POOL_GROUP = 128
POOL_BUF = 15
CONV_WIDTH = 31
CONV_BUF = 30
N_MEM = 256
X_HEADS = 4
X_HEAD_DIM = 128
X_WIDTH = 512
N_EXPERTS = 8
PAGE = 128
EPS = 1e-6
ATT_SCALE = HEAD_DIM ** -0.5
NEG_BIG = -1e30

LANES = 128
SUBLANES = 8
MIB = 1024 * 1024


def _params(sem, vmem_mib):
    return pltpu.CompilerParams(dimension_semantics=sem, vmem_limit_bytes=vmem_mib * MIB)


def _full(shape):
    return pl.BlockSpec(shape, lambda *_: (0,) * len(shape))


def _rms(x, g):
    return x * lax.rsqrt(jnp.mean(x * x, axis=-1, keepdims=True) + EPS) * g


def _split2(x):
    hi = x.astype(BF16)
    lo = (x - hi.astype(F32)).astype(BF16)
    return hi, lo


def _split3(x):
    hi = x.astype(BF16)
    r = x - hi.astype(F32)
    mid = r.astype(BF16)
    lo = (r - mid.astype(F32)).astype(BF16)
    return hi, mid, lo


def _dot(a, b):
    return jnp.dot(a, b, preferred_element_type=F32)


def _dot_nt(a, b):
    return lax.dot_general(a, b, (((1,), (1,)), ((), ())), preferred_element_type=F32)


def _softplus(z):
    return jnp.maximum(z, 0.0) + jnp.log1p(jnp.exp(-jnp.abs(z)))


def _head_rms(a, g, bd, head_dim):
    hi, lo = _split2(a * a)
    ssq = _dot(hi, bd) + _dot(lo, bd)
    return a * lax.rsqrt(ssq * (1.0 / head_dim) + EPS) * g


def _in_even_kernel(x_ref, g_ref, w_ref, wf_ref, wft_ref, bf_ref, bft_ref, gq_ref, gk_ref, bd_ref,
                    p_ref, q_ref, k32_ref, kb_ref, v32_ref, vb_ref, lf_ref, lft_ref):
    h = _rms(x_ref[...], g_ref[...]).astype(BF16)
    bd = bd_ref[...]
    p_ref[...] = _dot(h, w_ref[:, 0:MIX_HALF])
    q = _head_rms(_dot(h, w_ref[:, MIX_HALF:2 * MIX_HALF]), gq_ref[...], bd, HEAD_DIM)
    q_ref[...] = q
    k = _head_rms(_dot(h, w_ref[:, 2 * MIX_HALF:3 * MIX_HALF]), gk_ref[...], bd, HEAD_DIM)
    k32_ref[...] = k
    kb_ref[...] = k.astype(BF16)
    v = _dot(h, w_ref[:, 3 * MIX_HALF:4 * MIX_HALF])
    v32_ref[...] = v
    vb_ref[...] = v.astype(BF16)
    fl = _dot(h, wf_ref[...]) + bf_ref[...]
    lf_ref[...] = (-_softplus(-fl))[:, 0:N_HEADS]
    flt = _dot_nt(wft_ref[...], h) + bft_ref[...]
    lft_ref[...] = -_softplus(-flt)


def _in_even(x, g, w, wf, wft, bf, bft, gq, gk, bd, tm):
    T = x.shape[0]
    row = lambda n: pl.BlockSpec((tm, n), lambda i: (i, 0))
    sds = jax.ShapeDtypeStruct
    return pl.pallas_call(
        _in_even_kernel,
        grid=(T // tm,),
        in_specs=[row(D_MODEL), _full((1, D_MODEL)), _full((D_MODEL, 4 * MIX_HALF)), _full((D_MODEL, LANES)),
                  _full((N_HEADS, D_MODEL)), _full((1, LANES)), _full((N_HEADS, 1)), _full((1, MIX_HALF)),
                  _full((1, MIX_HALF)), _full((MIX_HALF, MIX_HALF))],
        out_specs=[row(MIX_HALF), row(MIX_HALF), row(MIX_HALF), row(MIX_HALF), row(MIX_HALF), row(MIX_HALF),
                   row(N_HEADS), pl.BlockSpec((N_HEADS, tm), lambda i: (0, i))],
        out_shape=[sds((T, MIX_HALF), F32), sds((T, MIX_HALF), F32), sds((T, MIX_HALF), F32),
                   sds((T, MIX_HALF), BF16), sds((T, MIX_HALF), F32), sds((T, MIX_HALF), BF16),
                   sds((T, N_HEADS), F32), sds((N_HEADS, T), F32)],
        compiler_params=_params(("parallel",), 48),
        name="in_even",
    )(x, g, w, wf, wft, bf, bft, gq, gk, bd)


def _in_odd_kernel(x_ref, g_ref, w_ref, u_ref, q_ref, k32_ref, kb_ref, v32_ref, vb_ref):
    h = _rms(x_ref[...], g_ref[...]).astype(BF16)
    a = _dot(h, w_ref[:, 0:MIX_HALF])
    gate = _dot(h, w_ref[:, MIX_HALF:2 * MIX_HALF])
    u_ref[...] = a * (1.0 / (1.0 + jnp.exp(-gate)))
    q_ref[...] = _dot(h, w_ref[:, 2 * MIX_HALF:3 * MIX_HALF])
    k = _dot(h, w_ref[:, 3 * MIX_HALF:4 * MIX_HALF])
    k32_ref[...] = k
    kb_ref[...] = k.astype(BF16)
    v = _dot(h, w_ref[:, 4 * MIX_HALF:5 * MIX_HALF])
    v32_ref[...] = v
    vb_ref[...] = v.astype(BF16)


def _in_odd(x, g, w, tm):
    T = x.shape[0]
    row = lambda n: pl.BlockSpec((tm, n), lambda i: (i, 0))
    sds = jax.ShapeDtypeStruct
    return pl.pallas_call(
        _in_odd_kernel,
        grid=(T // tm,),
        in_specs=[row(D_MODEL), _full((1, D_MODEL)), _full((D_MODEL, 5 * MIX_HALF))],
        out_specs=[row(MIX_HALF)] * 6,
        out_shape=[sds((T, MIX_HALF), F32), sds((T, MIX_HALF), F32), sds((T, MIX_HALF), F32),
                   sds((T, MIX_HALF), BF16), sds((T, MIX_HALF), F32), sds((T, MIX_HALF), BF16)],
        compiler_params=_params(("parallel",), 48),
        name="in_odd",
    )(x, g, w)


POOL_PAD = 16
CONV_PAD = 32


def _pool_kernel(pre_ref, p_ref, w_ref, sc_ref, y_ref, st_ref, buf, *, ns, L, ch, start_pos):
    for s in range(ns):
        buf[0:POOL_PAD - POOL_BUF, :] = jnp.zeros((POOL_PAD - POOL_BUF, MIX_HALF), F32)
        buf[POOL_PAD - POOL_BUF:POOL_PAD, :] = pre_ref[s]
        buf[POOL_PAD:POOL_PAD + L, :] = p_ref[s]
        st_ref[s] = buf[POOL_PAD + L - POOL_BUF:POOL_PAD + L, :]
        for c0 in range(0, L, ch):
            pos = start_pos + c0 + lax.broadcasted_iota(jnp.int32, (ch, POOL_GROUP), 0)
            outs = []
            for gi, w in enumerate(POOL_WINDOWS):
                ls = slice(gi * POOL_GROUP, (gi + 1) * POOL_GROUP)
                cur = buf[POOL_PAD + c0:POOL_PAD + c0 + ch, ls]
                acc = cur
                for i in range(1, w):
                    acc = acc + buf[POOL_PAD + c0 - i:POOL_PAD + c0 - i + ch, ls]
                cnt = jnp.minimum(pos + 1, w).astype(F32)
                d = acc / cnt - cur
                outs.append(_dot(d.astype(BF16), w_ref[gi]) * sc_ref[:, ls])
            y_ref[s, c0:c0 + ch, :] = jnp.concatenate(outs, axis=-1)


def _pool(prefix, p, w_grp, scale, ns, ch, start_pos):
    B, L, _ = p.shape
    kern = functools.partial(_pool_kernel, ns=ns, L=L, ch=ch, start_pos=start_pos)
    return pl.pallas_call(
        kern,
        grid=(B // ns,),
        in_specs=[pl.BlockSpec((ns, POOL_BUF, MIX_HALF), lambda i: (i, 0, 0)),
                  pl.BlockSpec((ns, L, MIX_HALF), lambda i: (i, 0, 0)),
                  _full((len(POOL_WINDOWS), POOL_GROUP, POOL_GROUP)), _full((1, MIX_HALF))],
        out_specs=[pl.BlockSpec((ns, L, MIX_HALF), lambda i: (i, 0, 0)),
                   pl.BlockSpec((ns, POOL_BUF, MIX_HALF), lambda i: (i, 0, 0))],
        out_shape=[jax.ShapeDtypeStruct((B, L, MIX_HALF), F32),
                   jax.ShapeDtypeStruct((B, POOL_BUF, MIX_HALF), F32)],
        scratch_shapes=[pltpu.VMEM((POOL_PAD + L, MIX_HALF), F32)],
        compiler_params=_params(("parallel",), 40),
        name="pool_mixer",
    )(prefix, p, w_grp, scale)


def _conv_kernel(pre_ref, u_ref, w_ref, b_ref, lg_ref, lb_ref, y_ref, st_ref, buf, *, ns, L, ch):
    for s in range(ns):
        buf[0:CONV_PAD - CONV_BUF, :] = jnp.zeros((CONV_PAD - CONV_BUF, MIX_HALF), F32)
        buf[CONV_PAD - CONV_BUF:CONV_PAD, :] = pre_ref[s]
        buf[CONV_PAD:CONV_PAD + L, :] = u_ref[s]
        st_ref[s] = buf[CONV_PAD + L - CONV_BUF:CONV_PAD + L, :]
        base = CONV_PAD - CONV_BUF
        for c0 in range(0, L, ch):
            acc = buf[base + c0:base + c0 + ch, :] * w_ref[0:1, :]
            for j in range(1, CONV_WIDTH):
                acc = acc + buf[base + c0 + j:base + c0 + j + ch, :] * w_ref[j:j + 1, :]
            y = acc + b_ref[...]
            mu = jnp.mean(y, axis=-1, keepdims=True)
            yc = y - mu
            var = jnp.mean(yc * yc, axis=-1, keepdims=True)
            yn = yc * lax.rsqrt(var + EPS) * lg_ref[...] + lb_ref[...]
            y_ref[s, c0:c0 + ch, :] = yn * (1.0 / (1.0 + jnp.exp(-yn)))


def _conv(prefix, u, conv_w, conv_b, ln_g, ln_b, ns, ch):
    B, L, _ = u.shape
    kern = functools.partial(_conv_kernel, ns=ns, L=L, ch=ch)
    return pl.pallas_call(
        kern,
        grid=(B // ns,),
        in_specs=[pl.BlockSpec((ns, CONV_BUF, MIX_HALF), lambda i: (i, 0, 0)),
                  pl.BlockSpec((ns, L, MIX_HALF), lambda i: (i, 0, 0)),
                  _full((CONV_WIDTH, MIX_HALF)), _full((1, MIX_HALF)), _full((1, MIX_HALF)), _full((1, MIX_HALF))],
        out_specs=[pl.BlockSpec((ns, L, MIX_HALF), lambda i: (i, 0, 0)),
                   pl.BlockSpec((ns, CONV_BUF, MIX_HALF), lambda i: (i, 0, 0))],
        out_shape=[jax.ShapeDtypeStruct((B, L, MIX_HALF), F32),
                   jax.ShapeDtypeStruct((B, CONV_BUF, MIX_HALF), F32)],
        scratch_shapes=[pltpu.VMEM((CONV_PAD + L, MIX_HALF), F32)],
        compiler_params=_params(("parallel",), 40),
        name="conv_module",
    )(prefix, u, conv_w, conv_b, ln_g, ln_b)


def _cumsum_prompt_kernel(lf_ref, lft_ref, tril_ref, triu_ref, f_ref, ft_ref, *, L, blk):
    c_col = jnp.zeros((1, N_HEADS), F32)
    c_row = jnp.zeros((N_HEADS, 1), F32)
    for b0 in range(0, L, blk):
        x = lf_ref[b0:b0 + blk, :]
        xt = lft_ref[:, b0:b0 + blk]
        f = c_col
        for part in _split3(x):
            f = f + _dot(tril_ref[...], part)
        ft = c_row
        for part in _split3(xt):
            ft = ft + _dot(part, triu_ref[...])
        f_ref[b0:b0 + blk, :] = f
        ft_ref[:, b0:b0 + blk] = ft
        c_col = f[blk - 1:blk, :]
        c_row = ft[:, blk - 1:blk]


def _cumsum_prompt(lf, lft, tril, triu, B, L, blk):
    kern = functools.partial(_cumsum_prompt_kernel, L=L, blk=blk)
    return pl.pallas_call(
        kern,
        grid=(B,),
        in_specs=[pl.BlockSpec((L, N_HEADS), lambda b: (b, 0)), pl.BlockSpec((N_HEADS, L), lambda b: (0, b)),
                  _full((blk, blk)), _full((blk, blk))],
        out_specs=[pl.BlockSpec((L, N_HEADS), lambda b: (b, 0)), pl.BlockSpec((N_HEADS, L), lambda b: (0, b))],
        out_shape=[jax.ShapeDtypeStruct((B * L, N_HEADS), F32), jax.ShapeDtypeStruct((N_HEADS, B * L), F32)],
        compiler_params=_params(("parallel",), 32),
        name="fox_cumsum_prompt",
    )(lf, lft, tril, triu)


def _head_pair(q):
    lane = lax.broadcasted_iota(jnp.int32, q.shape, 1)
    zero = jnp.zeros_like(q)
    scale = jnp.asarray(ATT_SCALE, q.dtype)
    return [jnp.where(lane < HEAD_DIM, q, zero) * scale, jnp.where(lane >= HEAD_DIM, q, zero) * scale]


def _stack_heads(q):
    return jnp.concatenate(_head_pair(q), axis=0)


SB_DEAD = -105.0


def _fox_prompt_kernel(q_ref, k_ref, v_ref, f_ref, ft_ref, o_ref, *, blk):
    hp = pl.program_id(1)
    qi = pl.program_id(2)
    qs = _stack_heads(q_ref[...])
    f = f_ref[...]
    lane8 = lax.broadcasted_iota(jnp.int32, f.shape, 1)
    fq = [jnp.sum(jnp.where(lane8 == 2 * hp + i, f, 0.0), axis=1, keepdims=True) for i in range(2)]
    row = lax.broadcasted_iota(jnp.int32, (2 * blk, blk), 0) & (blk - 1)
    col = lax.broadcasted_iota(jnp.int32, (2 * blk, blk), 1)

    def block(ki, carry, diagonal):
        m, l, acc = carry
        ks = pl.multiple_of(ki * blk, blk)
        s = _dot_nt(qs, k_ref[pl.ds(ks, blk), :])
        s = jnp.concatenate([s[i * blk:(i + 1) * blk] + (fq[i] - ft_ref[0, i:i + 1, pl.ds(ks, blk)])
                             for i in range(2)], axis=0)
        if diagonal:
            s = jnp.where(col <= row, s, -jnp.inf)
        m_new = jnp.maximum(m, jnp.max(s, axis=1, keepdims=True))
        alpha = jnp.exp(m - m_new)
        p = jnp.exp(s - m_new)
        l = alpha * l + jnp.sum(p, axis=1, keepdims=True)
        acc = alpha * acc + _dot(p.astype(BF16), v_ref[pl.ds(ks, blk), :])
        return m_new, l, acc

    init = (jnp.full((2 * blk, 1), NEG_BIG, F32), jnp.zeros((2 * blk, 1), F32), jnp.zeros((2 * blk, LANES), F32))
    carry = lax.fori_loop(0, qi, lambda ki, c: block(ki, c, False), init)
    _, l, acc = block(qi, carry, True)
    o = acc / l
    lane = lax.broadcasted_iota(jnp.int32, (blk, LANES), 1)
    o_ref[...] = jnp.where(lane < HEAD_DIM, o[0:blk], o[blk:2 * blk])


def _fox_prompt(q, k, v, f, ft, B, L, blk):
    nq = L // blk
    kern = functools.partial(_fox_prompt_kernel, blk=blk)
    return pl.pallas_call(
        kern,
        grid=(B, N_HEADS // 2, nq),
        in_specs=[pl.BlockSpec((blk, LANES), lambda b, h, i: (b * nq + i, h)),
                  pl.BlockSpec((L, LANES), lambda b, h, i: (b, h)),
                  pl.BlockSpec((L, LANES), lambda b, h, i: (b, h)),
                  pl.BlockSpec((blk, N_HEADS), lambda b, h, i: (b * nq + i, 0)),
                  pl.BlockSpec((1, 2, L), lambda b, h, i: (h, 0, b))],
        out_specs=pl.BlockSpec((blk, LANES), lambda b, h, i: (b * nq + i, h)),
        out_shape=jax.ShapeDtypeStruct((B * L, MIX_HALF), F32),
        compiler_params=_params(("parallel", "parallel", "arbitrary"), 32),
        name="fox_attention_prompt",
    )(q, k, v, f, ft)


def _sb_weights(z, c, u, mask):
    sp = _softplus(z)
    ln = -sp
    if mask is not None:
        ln = jnp.where(mask, ln, 0.0)
    hi, lo = _split2(ln)
    e = _dot(hi, u) + _dot(lo, u)
    a = jnp.exp((z - sp) + e + c)
    if mask is not None:
        a = jnp.where(mask, a, 0.0)
    return a, c + jnp.sum(ln, axis=1, keepdims=True)


def _sb_block(z, c, v, u, mask):
    a, c = _sb_weights(z, c, u, mask)
    return _dot(a.astype(BF16), v), c


def _sb_prompt_kernel(q_ref, k_ref, v_ref, u_ref, o_ref, *, blk):
    qi = pl.program_id(2)
    qs = _stack_heads(q_ref[...])
    row = lax.broadcasted_iota(jnp.int32, (2 * blk, blk), 0) & (blk - 1)
    col = lax.broadcasted_iota(jnp.int32, (2 * blk, blk), 1)
    u = u_ref[...]

    def block(ki, c, acc, diagonal):
        ks = pl.multiple_of(ki * blk, blk)
        z = _dot_nt(qs, k_ref[pl.ds(ks, blk), :])
        pv, c = _sb_block(z, c, v_ref[pl.ds(ks, blk), :], u, (col < row) if diagonal else None)
        return c, acc + pv

    c, acc = block(qi, jnp.zeros((2 * blk, 1), F32), jnp.zeros((2 * blk, LANES), F32), True)

    def cond(state):
        j, alive, _, _ = state
        return (j < qi) & (alive > 0)

    def body(state):
        j, _, c, acc = state
        c, acc = block(qi - 1 - j, c, acc, False)
        return j + 1, (jnp.max(c) > SB_DEAD).astype(jnp.int32), c, acc

    _, _, _, acc = lax.while_loop(cond, body, (jnp.int32(0), jnp.int32(1), c, acc))
    lane = lax.broadcasted_iota(jnp.int32, (blk, LANES), 1)
    o_ref[...] = jnp.where(lane < HEAD_DIM, acc[0:blk], acc[blk:2 * blk])


def _sb_prompt(q, k, v, u, B, L, blk):
    nq = L // blk
    kern = functools.partial(_sb_prompt_kernel, blk=blk)
    return pl.pallas_call(
        kern,
        grid=(B, N_HEADS // 2, nq),
        in_specs=[pl.BlockSpec((blk, LANES), lambda b, h, i: (b * nq + i, h)),
                  pl.BlockSpec((L, LANES), lambda b, h, i: (b, h)),
                  pl.BlockSpec((L, LANES), lambda b, h, i: (b, h)),
                  _full((blk, blk))],
        out_specs=pl.BlockSpec((blk, LANES), lambda b, h, i: (b * nq + i, h)),
        out_shape=jax.ShapeDtypeStruct((B * L, MIX_HALF), F32),
        compiler_params=_params(("parallel", "parallel", "arbitrary"), 32),
        name="sb_attention_prompt",
    )(q, k, v, u)


N_PAGES = 16
DEC_SEQ = 8
BD_ROWS = DEC_SEQ * N_HEADS


def _block_diag_q(q, hm):
    rows = [jnp.broadcast_to(q[t:t + 1, :], (N_HEADS, MIX_HALF)) * hm for t in range(DEC_SEQ)]
    return (jnp.concatenate(rows, axis=0) * ATT_SCALE).astype(BF16)


def _pad_new(x):
    return jnp.concatenate([x, jnp.zeros((PAGE - DEC_SEQ, x.shape[1]), x.dtype)], axis=0)


def _merge_heads(o, hm):
    return jnp.concatenate(
        [jnp.sum(o[t * N_HEADS:(t + 1) * N_HEADS, :] * hm, axis=0, keepdims=True) for t in range(DEC_SEQ)], axis=0)


def _new_key_mask(strict):
    row = lax.broadcasted_iota(jnp.int32, (BD_ROWS, PAGE), 0) // N_HEADS
    col = lax.broadcasted_iota(jnp.int32, (BD_ROWS, PAGE), 1)
    return (col < row) if strict else (col <= row)


def _page_scores(qbd, k_pages, kn_ref, j):
    if j < N_PAGES:
        return _dot(qbd, k_pages[j][0].reshape(MIX_HALF, PAGE).astype(BF16))
    return _dot_nt(qbd, _pad_new(kn_ref[...]).astype(BF16))


def _page_values(w, v_pages, vn_ref, j):
    if j < N_PAGES:
        return _dot_nt(w, v_pages[j][0].reshape(MIX_HALF, PAGE).astype(BF16))
    return _dot(w, _pad_new(vn_ref[...]).astype(BF16))


def _cumsum_lanes(x):
    n = x.shape[1]
    lane = lax.broadcasted_iota(jnp.int32, x.shape, 1)
    k = 1
    while k < n:
        x = x + jnp.where(lane >= k, pltpu.roll(x, k, 1), 0.0)
        k *= 2
    return x


def _fox_decode_kernel(pt_ref, *refs):
    k_pages = refs[0:N_PAGES]
    v_pages = refs[N_PAGES:2 * N_PAGES]
    lf_pages = refs[2 * N_PAGES:3 * N_PAGES]
    q_ref, kn_ref, vn_ref, lftn_ref, hm_ref, o_ref = refs[3 * N_PAGES:]
    del pt_ref
    hm = hm_ref[...]
    qbd = _block_diag_q(q_ref[...], hm)
    parts = [lf_pages[j][0] for j in range(N_PAGES)]
    parts.append(jnp.concatenate([lftn_ref[0], jnp.zeros((N_HEADS, PAGE - DEC_SEQ), F32)], axis=1))
    fk = _cumsum_lanes(jnp.concatenate(parts, axis=1))
    fnew = fk[:, N_PAGES * PAGE:]
    lane = lax.broadcasted_iota(jnp.int32, (N_HEADS, PAGE), 1)
    fq = jnp.concatenate([jnp.sum(jnp.where(lane == t, fnew, 0.0), axis=1, keepdims=True)
                          for t in range(DEC_SEQ)], axis=0)
    scores = []
    for j in range(N_PAGES + 1):
        bias = jnp.concatenate([fk[:, j * PAGE:(j + 1) * PAGE]] * DEC_SEQ, axis=0)
        s = _page_scores(qbd, k_pages, kn_ref, j) + (fq - bias)
        if j == N_PAGES:
            s = jnp.where(_new_key_mask(False), s, -jnp.inf)
        scores.append(s)
    m = scores[0].max(axis=1, keepdims=True)
    for s in scores[1:]:
        m = jnp.maximum(m, s.max(axis=1, keepdims=True))
    l = jnp.zeros((BD_ROWS, 1), F32)
    acc = jnp.zeros((BD_ROWS, MIX_HALF), F32)
    for j in range(N_PAGES + 1):
        p = jnp.exp(scores[j] - m)
        l = l + jnp.sum(p, axis=1, keepdims=True)
        acc = acc + _page_values(p.astype(BF16), v_pages, vn_ref, j)
    o_ref[...] = _merge_heads(acc / l, hm)


def _sb_decode_kernel(pt_ref, *refs):
    k_pages = refs[0:N_PAGES]
    v_pages = refs[N_PAGES:2 * N_PAGES]
    q_ref, kn_ref, vn_ref, hm_ref, u_ref, o_ref = refs[2 * N_PAGES:]
    del pt_ref
    hm = hm_ref[...]
    qbd = _block_diag_q(q_ref[...], hm)
    u = u_ref[...]
    c = jnp.zeros((BD_ROWS, 1), F32)
    acc = jnp.zeros((BD_ROWS, MIX_HALF), F32)
    for j in range(N_PAGES, -1, -1):
        a, c = _sb_weights(_page_scores(qbd, k_pages, kn_ref, j), c, u, _new_key_mask(True) if j == N_PAGES else None)
        acc = acc + _page_values(a.astype(BF16), v_pages, vn_ref, j)
    o_ref[...] = _merge_heads(acc, hm)


def _page_specs(shape_tail, n):
    specs = []
    for j in range(n):
        specs.append(pl.BlockSpec((1,) + shape_tail, functools.partial(
            lambda b, pt, j: (pt[b, j],) + (0,) * len(shape_tail), j=j)))
    return specs


def _fox_decode(page_table, ck, cv, clf, q, kn, vn, lftn, hm):
    nb = page_table.shape[0]
    seq = lambda n: pl.BlockSpec((DEC_SEQ, n), lambda b, pt: (b, 0))
    const = lambda shape: pl.BlockSpec(shape, lambda b, pt: (0,) * len(shape))
    kv_page = (N_HEADS, HEAD_DIM, PAGE)
    in_specs = (_page_specs(kv_page, N_PAGES) + _page_specs(kv_page, N_PAGES) + _page_specs((N_HEADS, PAGE), N_PAGES)
                + [seq(MIX_HALF), seq(MIX_HALF), seq(MIX_HALF),
                   pl.BlockSpec((1, N_HEADS, DEC_SEQ), lambda b, pt: (b, 0, 0)), const((N_HEADS, MIX_HALF))])
    return pl.pallas_call(
        _fox_decode_kernel,
        grid_spec=pltpu.PrefetchScalarGridSpec(
            num_scalar_prefetch=1, grid=(nb,), in_specs=in_specs, out_specs=seq(MIX_HALF)),
        out_shape=jax.ShapeDtypeStruct((nb * DEC_SEQ, MIX_HALF), F32),
        compiler_params=_params(("arbitrary",), 48),
        name="fox_attention_decode",
    )(page_table, *([ck] * N_PAGES), *([cv] * N_PAGES), *([clf] * N_PAGES), q, kn, vn, lftn, hm)


def _sb_decode(page_table, ck, cv, q, kn, vn, hm, u):
    nb = page_table.shape[0]
    seq = lambda n: pl.BlockSpec((DEC_SEQ, n), lambda b, pt: (b, 0))
    const = lambda shape: pl.BlockSpec(shape, lambda b, pt: (0,) * len(shape))
    kv_page = (N_HEADS, HEAD_DIM, PAGE)
    in_specs = (_page_specs(kv_page, N_PAGES) + _page_specs(kv_page, N_PAGES)
                + [seq(MIX_HALF), seq(MIX_HALF), seq(MIX_HALF), const((N_HEADS, MIX_HALF)), const((PAGE, PAGE))])
    return pl.pallas_call(
        _sb_decode_kernel,
        grid_spec=pltpu.PrefetchScalarGridSpec(
            num_scalar_prefetch=1, grid=(nb,), in_specs=in_specs, out_specs=seq(MIX_HALF)),
        out_shape=jax.ShapeDtypeStruct((nb * DEC_SEQ, MIX_HALF), F32),
        compiler_params=_params(("arbitrary",), 48),
        name="sb_attention_decode",
    )(page_table, *([ck] * N_PAGES), *([cv] * N_PAGES), q, kn, vn, hm, u)


def _x_head_rms(a, g):
    outs = []
    for h in range(X_HEADS):
        ah = a[:, h * X_HEAD_DIM:(h + 1) * X_HEAD_DIM]
        outs.append(ah * lax.rsqrt(jnp.mean(ah * ah, axis=-1, keepdims=True) + EPS))
    return jnp.concatenate(outs, axis=-1) * g


def _mix_out_kernel(x_ref, ya_ref, yb_ref, wo_ref, g_ref, wq_ref, gq_ref, x1_ref, qc_ref):
    y = _dot(ya_ref[...].astype(BF16), wo_ref[0:MIX_HALF, :]) + _dot(yb_ref[...].astype(BF16), wo_ref[MIX_HALF:, :])
    x1 = x_ref[...] + y
    x1_ref[...] = x1
    hc = _rms(x1, g_ref[...]).astype(BF16)
    qc_ref[...] = _x_head_rms(_dot(hc, wq_ref[...]), gq_ref[...])


def _mix_out(x, ya, yb, wo, g, wq, gq, tm):
    T = x.shape[0]
    row = lambda n: pl.BlockSpec((tm, n), lambda i: (i, 0))
    return pl.pallas_call(
        _mix_out_kernel,
        grid=(T // tm,),
        in_specs=[row(D_MODEL), row(MIX_HALF), row(MIX_HALF), _full((D_MODEL, D_MODEL)), _full((1, D_MODEL)),
                  _full((D_MODEL, X_WIDTH)), _full((1, X_WIDTH))],
        out_specs=[row(D_MODEL), row(X_WIDTH)],
        out_shape=[jax.ShapeDtypeStruct((T, D_MODEL), F32), jax.ShapeDtypeStruct((T, X_WIDTH), F32)],
        compiler_params=_params(("parallel",), 40),
        name="mixer_out_proj",
    )(x, ya, yb, wo, g, wq, gq)


def _mem_kv_kernel(m_ref, wk_ref, wv_ref, gk_ref, k_ref, v_ref, *, nb):
    m = m_ref[...].astype(BF16)
    k = _x_head_rms(_dot(m, wk_ref[0]), gk_ref[0])
    v = _dot(m, wv_ref[0])
    for b in range(nb):
        for h in range(X_HEADS):
            ls = slice(h * X_HEAD_DIM, (h + 1) * X_HEAD_DIM)
            k_ref[0, b, :, h, :] = k[b * N_MEM:(b + 1) * N_MEM, ls]
            v_ref[0, b, :, h, :] = v[b * N_MEM:(b + 1) * N_MEM, ls]


def _mem_kv(mem, wk, wv, gk, nb):
    B = mem.shape[0] // N_MEM
    depth = wk.shape[0]
    out = pl.BlockSpec((1, nb, N_MEM, X_HEADS, X_HEAD_DIM), lambda l, i: (l, i, 0, 0, 0))
    return pl.pallas_call(
        functools.partial(_mem_kv_kernel, nb=nb),
        grid=(depth, B // nb),
        in_specs=[pl.BlockSpec((nb * N_MEM, D_MODEL), lambda l, i: (i, 0)),
                  pl.BlockSpec((1, D_MODEL, X_WIDTH), lambda l, i: (l, 0, 0)),
                  pl.BlockSpec((1, D_MODEL, X_WIDTH), lambda l, i: (l, 0, 0)),
                  pl.BlockSpec((1, 1, X_WIDTH), lambda l, i: (l, 0, 0))],
        out_specs=[out, out],
        out_shape=[jax.ShapeDtypeStruct((depth, B, N_MEM, X_HEADS, X_HEAD_DIM), F32)] * 2,
        compiler_params=_params(("parallel", "parallel"), 32),
        name="memory_kv",
    )(mem, wk, wv, gk)


def _cross_kernel(x_ref, q_ref, mk_ref, mv_ref, wo_ref, g_ref, x2_ref, hf_ref, *, ns, tq):
    outs = []
    for s in range(ns):
        q = q_ref[s * tq:(s + 1) * tq, :]
        heads = []
        for h in range(X_HEADS):
            ls = slice(h * X_HEAD_DIM, (h + 1) * X_HEAD_DIM)
            qh = (q[:, ls] * (X_HEAD_DIM ** -0.5)).astype(BF16)
            sc = _dot_nt(qh, mk_ref[0, s, :, h, :].astype(BF16))
            p = jnp.exp(sc - jnp.max(sc, axis=1, keepdims=True))
            o = _dot(p.astype(BF16), mv_ref[0, s, :, h, :].astype(BF16))
            heads.append(o / jnp.sum(p, axis=1, keepdims=True))
        outs.append(jnp.concatenate(heads, axis=-1))
    o = outs[0] if ns == 1 else jnp.concatenate(outs, axis=0)
    x2 = x_ref[...] + _dot(o.astype(BF16), wo_ref[...])
    x2_ref[...] = x2
    hf_ref[...] = _rms(x2, g_ref[...])


def _cross(x, q, mk, mv, layer, wo, g, ns, tq):
    T = x.shape[0]
    L = T // mk.shape[1]
    nq = L // tq
    rows = ns * tq
    row = lambda n: pl.BlockSpec((rows, n), lambda i: (i, 0))
    mem = pl.BlockSpec((1, ns, N_MEM, X_HEADS, X_HEAD_DIM), lambda i: (layer, i // nq, 0, 0, 0))
    kern = functools.partial(_cross_kernel, ns=ns, tq=tq)
    return pl.pallas_call(
        kern,
        grid=(T // rows,),
        in_specs=[row(D_MODEL), row(X_WIDTH), mem, mem, _full((X_WIDTH, D_MODEL)), _full((1, D_MODEL))],
        out_specs=[row(D_MODEL), row(D_MODEL)],
        out_shape=[jax.ShapeDtypeStruct((T, D_MODEL), F32)] * 2,
        compiler_params=_params(("parallel",), 48),
        name="cross_attention",
    )(x, q, mk, mv, wo, g)


def _swiglu_kernel(h_ref, x_ref, wg_ref, wu_ref, wd_ref, o_ref, hb, acc):
    f = pl.program_id(1)

    @pl.when(f == 0)
    def _():
        hb[...] = h_ref[...].astype(BF16)
        acc[...] = jnp.zeros_like(acc)

    h = hb[...]
    gte = _dot(h, wg_ref[...])
    act = gte * (1.0 / (1.0 + jnp.exp(-gte))) * _dot(h, wu_ref[...])
    acc[...] += _dot(act.astype(BF16), wd_ref[...])

    @pl.when(f == pl.num_programs(1) - 1)
    def _():
        o_ref[...] = x_ref[...] + acc[...]


def _swiglu(h, x, wg, wu, wd, tm, tf):
    T = h.shape[0]
    dff = wg.shape[1]
    row = pl.BlockSpec((tm, D_MODEL), lambda i, f: (i, 0))
    return pl.pallas_call(
        _swiglu_kernel,
        grid=(T // tm, dff // tf),
        in_specs=[row, row, pl.BlockSpec((D_MODEL, tf), lambda i, f: (0, f)),
                  pl.BlockSpec((D_MODEL, tf), lambda i, f: (0, f)), pl.BlockSpec((tf, D_MODEL), lambda i, f: (f, 0))],
        out_specs=row,
        out_shape=jax.ShapeDtypeStruct((T, D_MODEL), F32),
        scratch_shapes=[pltpu.VMEM((tm, D_MODEL), BF16), pltpu.VMEM((tm, D_MODEL), F32)],
        compiler_params=_params(("parallel", "arbitrary"), 56),
        name="dense_swiglu",
    )(h, x, wg, wu, wd)


def _router_kernel(h_ref, w_ref, b_ref, r_ref):
    h_hi, h_lo = _split2(h_ref[...])
    w_hi, w_lo = _split2(w_ref[...])
    logits = _dot(h_hi, w_hi) + _dot(h_hi, w_lo) + _dot(h_lo, w_hi) + b_ref[...]
    lane = lax.broadcasted_iota(jnp.int32, logits.shape, 1)
    logits = jnp.where(lane < N_EXPERTS, logits, -jnp.inf)
    m1 = jnp.max(logits, axis=1, keepdims=True)
    i1 = jnp.min(jnp.where(logits == m1, lane, LANES), axis=1, keepdims=True)
    rest = jnp.where(lane == i1, -jnp.inf, logits)
    m2 = jnp.max(rest, axis=1, keepdims=True)
    i2 = jnp.min(jnp.where(rest == m2, lane, LANES), axis=1, keepdims=True)
    e2 = jnp.exp(m2 - m1)
    g1 = 1.0 / (1.0 + e2)
    g2 = e2 / (1.0 + e2)
    r_ref[...] = jnp.where(lane == 0, i1.astype(F32), jnp.where(lane == 1, i2.astype(F32),
                           jnp.where(lane == 2, g1, jnp.where(lane == 3, g2, 0.0))))


def _router(h, w, b, tm):
    T = h.shape[0]
    return pl.pallas_call(
        _router_kernel,
        grid=(T // tm,),
        in_specs=[pl.BlockSpec((tm, D_MODEL), lambda i: (i, 0)), _full((D_MODEL, LANES)), _full((1, LANES))],
        out_specs=pl.BlockSpec((tm, LANES), lambda i: (i, 0)),
        out_shape=jax.ShapeDtypeStruct((T, LANES), F32),
        compiler_params=_params(("parallel",), 32),
        name="moe_router",
    )(h, w, b)


MOE_TM = 512
MOE_NF = 2
COMBINE_TM = 512


def _moe_plan(r, tm):
    T = r.shape[0]
    n_pairs = 2 * T
    e = r[:, 0:2].astype(jnp.int32).T.reshape(-1)
    onehot = (e[:, None] == jnp.arange(N_EXPERTS, dtype=jnp.int32)[None, :]).astype(jnp.int32)
    csum = jnp.cumsum(onehot, axis=0)
    rank = jnp.sum(csum * onehot, axis=1) - 1
    psz = (csum[-1] + tm - 1) // tm * tm
    gend = jnp.cumsum(psz)
    pos = (gend - psz)[e] + rank
    n_rows = n_pairs + N_EXPERTS * tm
    n_tiles = n_rows // tm
    src = jnp.zeros((n_rows,), jnp.int32).at[pos].set(jnp.arange(n_pairs, dtype=jnp.int32) % T)
    t0 = jnp.arange(n_tiles, dtype=jnp.int32) * tm
    tile_e = jnp.minimum(jnp.sum((t0[:, None] >= gend[None, :]).astype(jnp.int32), axis=1), N_EXPERTS - 1)
    n_valid = (gend[-1] // tm).astype(jnp.int32).reshape(1)
    return src.reshape(n_tiles, 1, tm), pos.astype(jnp.int32), tile_e, n_valid


def _row_gather(src_hbm, idx_ref, dst, sem, r0, n, unrolled):
    def start(r):
        pltpu.make_async_copy(src_hbm.at[pl.ds(idx_ref[0, 0, r0 + r], 1)], dst.at[pl.ds(r0 + r, 1)], sem).start()

    if unrolled:
        for r in range(n):
            start(r)
    else:
        def body(r, carry):
            start(r)
            return carry
        lax.fori_loop(0, n, body, 0, unroll=8)


def _moe_kernel(te_ref, nv_ref, src_ref, srcn_ref, h_hbm, wg_ref, wu_ref, wd_ref, y_ref, xin, hb, acc, gsem, *, tm):
    del te_ref
    i = pl.program_id(0)
    f = pl.program_id(1)
    nt = pl.num_programs(0)
    nf = pl.num_programs(1)
    slot = i % 2
    is_valid = i < nv_ref[0]
    part = tm // MOE_NF
    r0 = f * part

    def wait_tile(s):
        pltpu.make_async_copy(h_hbm.at[pl.ds(0, tm)], xin.at[s], gsem.at[s]).wait()

    @pl.when((i == 0) & (f == 0))
    def _():
        _row_gather(h_hbm, src_ref, xin.at[0], gsem.at[0], 0, tm, False)

    @pl.when(f == 0)
    def _():
        wait_tile(slot)
        hb[...] = xin[slot].astype(BF16)
        acc[...] = jnp.zeros_like(acc)

    @pl.when(is_valid)
    def _():
        _row_gather(h_hbm, srcn_ref, xin.at[1 - slot], gsem.at[1 - slot], r0, part, True)
        h = hb[...]
        gte = _dot(h, wg_ref[0])
        act = gte * (1.0 / (1.0 + jnp.exp(-gte))) * _dot(h, wu_ref[0])
        acc[...] += _dot(act.astype(BF16), wd_ref[0])

    @pl.when(jnp.logical_not(is_valid))
    def _():
        _row_gather(h_hbm, srcn_ref, xin.at[1 - slot], gsem.at[1 - slot], r0, part, False)

    @pl.when(f == nf - 1)
    def _():
        y_ref[...] = acc[...]

    @pl.when((i == nt - 1) & (f == nf - 1))
    def _():
        wait_tile(1 - slot)


def _moe(h, src, tile_e, n_valid, wg, wu, wd, tm):
    n_tiles = src.shape[0]
    nf = MOE_NF
    tf = wg.shape[2] // nf
    fser = lambda i, f: jnp.where(i % 2 == 0, f, nf - 1 - f)
    smem_tile = lambda imap: pl.BlockSpec((1, 1, tm), imap, memory_space=pltpu.SMEM)
    kern = functools.partial(_moe_kernel, tm=tm)
    return pl.pallas_call(
        kern,
        grid_spec=pltpu.PrefetchScalarGridSpec(
            num_scalar_prefetch=2,
            grid=(n_tiles, nf),
            in_specs=[smem_tile(lambda i, f, te, nv: (i, 0, 0)),
                      smem_tile(lambda i, f, te, nv: (jnp.minimum(i + 1, n_tiles - 1), 0, 0)),
                      pl.BlockSpec(memory_space=pl.ANY),
                      pl.BlockSpec((1, D_MODEL, tf), lambda i, f, te, nv: (te[i], 0, fser(i, f))),
                      pl.BlockSpec((1, D_MODEL, tf), lambda i, f, te, nv: (te[i], 0, fser(i, f))),
                      pl.BlockSpec((1, tf, D_MODEL), lambda i, f, te, nv: (te[i], fser(i, f), 0))],
            out_specs=pl.BlockSpec((tm, D_MODEL), lambda i, f, te, nv: (i, 0)),
            scratch_shapes=[pltpu.VMEM((2, tm, D_MODEL), F32), pltpu.VMEM((tm, D_MODEL), BF16),
                            pltpu.VMEM((tm, D_MODEL), F32), pltpu.SemaphoreType.DMA((2,))]),
        out_shape=jax.ShapeDtypeStruct((n_tiles * tm, D_MODEL), F32),
        compiler_params=_params(("arbitrary", "arbitrary"), 56),
        name="moe_swiglu",
    )(tile_e, n_valid, src, src, h, wg, wu, wd)


def _combine_kernel(pos_ref, posn_ref, x_ref, r_ref, y_hbm, o_ref, buf, sem, *, tm):
    i = pl.program_id(0)
    nt = pl.num_programs(0)
    slot = i % 2

    @pl.when(i == 0)
    def _():
        _row_gather(y_hbm, pos_ref, buf.at[0], sem.at[0], 0, 2 * tm, False)

    @pl.when(i + 1 < nt)
    def _():
        _row_gather(y_hbm, posn_ref, buf.at[1 - slot], sem.at[1 - slot], 0, 2 * tm, False)

    pltpu.make_async_copy(y_hbm.at[pl.ds(0, 2 * tm)], buf.at[slot], sem.at[slot]).wait()
    r = r_ref[...]
    o_ref[...] = x_ref[...] + (r[:, 2:3] * buf[slot, 0:tm, :] + r[:, 3:4] * buf[slot, tm:2 * tm, :])


def _combine(x, r, pos, y, tm):
    T = x.shape[0]
    nt = T // tm
    smem_tile = lambda imap: pl.BlockSpec((1, 1, 2 * tm), imap, memory_space=pltpu.SMEM)
    return pl.pallas_call(
        functools.partial(_combine_kernel, tm=tm),
        grid=(nt,),
        in_specs=[smem_tile(lambda i: (i, 0, 0)), smem_tile(lambda i: (jnp.minimum(i + 1, nt - 1), 0, 0)),
                  pl.BlockSpec((tm, D_MODEL), lambda i: (i, 0)), pl.BlockSpec((tm, LANES), lambda i: (i, 0)),
                  pl.BlockSpec(memory_space=pl.ANY)],
        out_specs=pl.BlockSpec((tm, D_MODEL), lambda i: (i, 0)),
        out_shape=jax.ShapeDtypeStruct((T, D_MODEL), F32),
        scratch_shapes=[pltpu.VMEM((2, 2 * tm, D_MODEL), F32), pltpu.SemaphoreType.DMA((2,))],
        compiler_params=_params(("arbitrary",), 40),
        name="moe_combine",
    )(pos, pos, x, r, y)


def _tri(n, kind):
    r = lax.broadcasted_iota(jnp.int32, (n, n), 0)
    c = lax.broadcasted_iota(jnp.int32, (n, n), 1)
    m = {"lower_incl": r >= c, "upper_incl": r <= c, "lower_strict": r > c}[kind]
    return m.astype(BF16)


def _head_mask(n_heads, head_dim):
    h = lax.broadcasted_iota(jnp.int32, (n_heads, n_heads * head_dim), 0)
    c = lax.broadcasted_iota(jnp.int32, (n_heads, n_heads * head_dim), 1) // head_dim
    return (h == c).astype(F32)


def _stack(x, mem_k, mem_v, W, group):
    T = x.shape[0]
    tm = 512
    B, L = group["B"], group["L"]
    prompt = group["kind"] == "prompt"
    new = {}

    p, q, k32, kb, v32, vb, lf, lft = _in_even(x, W["g_mix0"], W["w_in_e"], W["wf"], W["wft"], W["bf"], W["bft"],
                                               W["g_fq"], W["g_fk"], W["bd64"], tm)
    ya, pool_state = _pool(group["pool_prefix"], p.reshape(B, L, MIX_HALF), W["w_pool"], W["pool_scale"],
                           ns=group["seq_per_step"], ch=group["chunk"], start_pos=group["start_pos"])
    if prompt:
        f, ft = _cumsum_prompt(lf, lft, W["tril256"], W["triu256"], B, L, 256)
        yb = _fox_prompt(q.astype(BF16), kb, vb, f, ft.reshape(N_HEADS // 2, 2, T), B, L, 256)
    else:
        lftn = lft.reshape(N_HEADS, B, L).transpose(1, 0, 2)
        yb = _fox_decode(group["page_table"], group["fox_k"], group["fox_v"], group["fox_lf"], q, k32, v32, lftn,
                         W["hm64"])
    new.update(fox_k=k32, fox_v=v32, fox_logf=lf, pool=pool_state)
    x, qc = _mix_out(x, ya.reshape(T, MIX_HALF), yb, W["w_out_e"], W["g_cross0"], W["w_cq0"], W["g_cq0"], tm)
    x, hf = _cross(x, qc, mem_k, mem_v, 0, W["w_co0"], W["g_ffn0"], ns=group["x_ns"], tq=group["x_tq"])
    x = _swiglu(hf, x, W["w_gate"], W["w_up"], W["w_down"], tm, 1408)

    u, q, k32, kb, v32, vb = _in_odd(x, W["g_mix1"], W["w_in_o"], tm)
    yc, conv_state = _conv(group["conv_prefix"], u.reshape(B, L, MIX_HALF), W["conv_w"], W["conv_b"], W["ln_g"],
                           W["ln_b"], ns=group["seq_per_step"], ch=group["conv_chunk"])
    if prompt:
        yd = _sb_prompt(q.astype(BF16), kb, vb, W["ustrict256"], B, L, 256)
    else:
        yd = _sb_decode(group["page_table"], group["sb_k"], group["sb_v"], q, k32, v32, W["hm64"], W["ustrict128"])
    new.update(sb_k=k32, sb_v=v32, conv=conv_state)
    x, qc = _mix_out(x, yc.reshape(T, MIX_HALF), yd, W["w_out_o"], W["g_cross1"], W["w_cq1"], W["g_cq1"], tm)
    x, hf = _cross(x, qc, mem_k, mem_v, 1, W["w_co1"], W["g_ffn1"], ns=group["x_ns"], tq=group["x_tq"])
    r = _router(hf, W["w_router"], W["b_router"], tm)
    return x, hf, r, new


def kernel(x_prompt, x_sample, cache_fox_k, cache_fox_v, cache_fox_logf, cache_sb_k, cache_sb_v, cache_mem_k, cache_mem_v, state_pool, state_conv, page_table, mem_prompt, norm_mix, norm_cross, norm_ffn, w_cq, w_ck, w_cv, w_co, g_cq, g_ck, w_in_e, b_f, w_pool_grp, pool_scale, g_fq, g_fk, w_out_e, w_gate, w_up, w_down, w_in_o, conv_w, conv_b, ln_g, ln_b, w_out_o, w_router, b_router, we_gate, we_up, we_down):
    B, L, D = x_prompt.shape
    SB, SL, _ = x_sample.shape
    n_pool = cache_fox_k.shape[1]
    bf = lambda a: a.astype(BF16)
    row = lambda a: a.reshape(1, -1)

    wf = jnp.pad(w_in_e[0][:, 4 * MIX_HALF:], ((0, 0), (0, LANES - N_HEADS)))
    W = dict(
        g_mix0=row(norm_mix[0]), g_mix1=row(norm_mix[1]), g_cross0=row(norm_cross[0]), g_cross1=row(norm_cross[1]),
        g_ffn0=row(norm_ffn[0]), g_ffn1=row(norm_ffn[1]),
        w_in_e=bf(w_in_e[0][:, :4 * MIX_HALF]), wf=bf(wf), wft=bf(w_in_e[0][:, 4 * MIX_HALF:].T),
        bf=jnp.pad(row(b_f[0]), ((0, 0), (0, LANES - N_HEADS))), bft=b_f[0].reshape(N_HEADS, 1),
        g_fq=row(jnp.tile(g_fq[0], N_HEADS)), g_fk=row(jnp.tile(g_fk[0], N_HEADS)),
        w_pool=bf(w_pool_grp[0]), pool_scale=row(pool_scale[0]), w_out_e=bf(w_out_e[0]),
        w_cq0=bf(w_cq[0]), w_cq1=bf(w_cq[1]), w_co0=bf(w_co[0]), w_co1=bf(w_co[1]),
        g_cq0=row(jnp.tile(g_cq[0], X_HEADS)), g_cq1=row(jnp.tile(g_cq[1], X_HEADS)),
        w_gate=bf(w_gate[0]), w_up=bf(w_up[0]), w_down=bf(w_down[0]),
        w_in_o=bf(w_in_o[0]), conv_w=conv_w[0], conv_b=row(conv_b[0]), ln_g=row(ln_g[0]), ln_b=row(ln_b[0]),
        w_out_o=bf(w_out_o[0]),
        w_router=jnp.pad(w_router[0], ((0, 0), (0, LANES - N_EXPERTS))),
        b_router=jnp.pad(row(b_router[0]), ((0, 0), (0, LANES - N_EXPERTS))),
        we_gate=bf(we_gate[0]), we_up=bf(we_up[0]), we_down=bf(we_down[0]),
        bd64=jnp.kron(jnp.eye(N_HEADS, dtype=F32), jnp.ones((HEAD_DIM, HEAD_DIM), F32)).astype(BF16),
        hm64=_head_mask(N_HEADS, HEAD_DIM), eye8=jnp.eye(N_HEADS, dtype=BF16),
        tril256=_tri(256, "lower_incl"), triu256=_tri(256, "upper_incl"),
        ustrict256=_tri(256, "lower_strict"), ustrict128=_tri(128, "lower_strict"),
    )

    mk, mv = _mem_kv(mem_prompt.reshape(B * N_MEM, D), bf(w_ck), bf(w_cv),
                     jnp.tile(g_ck, (1, X_HEADS)).reshape(-1, 1, X_WIDTH), 2)

    prompt = dict(kind="prompt", B=B, L=L, start_pos=0, seq_per_step=1, chunk=256, conv_chunk=128,
                  pool_prefix=jnp.zeros((B, POOL_BUF, MIX_HALF), F32),
                  conv_prefix=jnp.zeros((B, CONV_BUF, MIX_HALF), F32), x_ns=1, x_tq=512)
    xp, hp, rp, sp = _stack(x_prompt.reshape(B * L, D), mk, mv, W, prompt)

    kv_view = lambda c: jnp.transpose(c[0], (0, 2, 3, 1))
    sample = dict(kind="sample", B=SB, L=SL, start_pos=page_table.shape[1] * PAGE, seq_per_step=16,
                  chunk=SL, conv_chunk=SL, pool_prefix=state_pool[0], conv_prefix=state_conv[0], x_ns=8, x_tq=SL,
                  page_table=page_table,
                  fox_k=kv_view(cache_fox_k), fox_v=kv_view(cache_fox_v),
                  fox_lf=jnp.transpose(cache_fox_logf[0], (0, 2, 1)),
                  sb_k=kv_view(cache_sb_k), sb_v=kv_view(cache_sb_v))
    xs, hs, rs, ss = _stack(x_sample.reshape(SB * SL, D), cache_mem_k, cache_mem_v, W, sample)

    n_tok = B * L + SB * SL
    src, pos, tile_e, n_valid = _moe_plan(jnp.concatenate([rp, rs], axis=0), MOE_TM)
    ye = _moe(jnp.concatenate([hp, hs], axis=0), src, tile_e, n_valid, W["we_gate"], W["we_up"], W["we_down"], MOE_TM)

    def pair_rows(row0, n):
        both = [pos[s * n_tok + row0:s * n_tok + row0 + n].reshape(n // COMBINE_TM, COMBINE_TM) for s in range(2)]
        return jnp.concatenate(both, axis=1).reshape(n // COMBINE_TM, 1, 2 * COMBINE_TM)

    yp = _combine(xp, rp, pair_rows(0, B * L), ye, COMBINE_TM)
    ys = _combine(xs, rs, pair_rows(B * L, SB * SL), ye, COMBINE_TM)

    def outs(new, b, l):
        kv = lambda a: a.reshape(1, b, l, N_HEADS, HEAD_DIM)
        return (kv(new["fox_k"]), kv(new["fox_v"]), new["fox_logf"].reshape(1, b, l, N_HEADS),
                kv(new["sb_k"]), kv(new["sb_v"]), new["pool"][None], new["conv"][None])

    op = outs(sp, B, L)
    os_ = outs(ss, SB, SL)
    return (yp.reshape(B, L, D), ys.reshape(SB, SL, D), *op, mk, mv, *os_)
```

```python
import functools

import jax
import jax.numpy as jnp
from jax import lax
from jax.experimental import pallas as pl
from jax.experimental.pallas import tpu as pltpu

F32 = jnp.float32
BF16 = jnp.bfloat16

D_MODEL = 1024
MIX_HALF = 512
HEAD_DIM = 64
N_HEADS = 8
POOL_WINDOWS = (2, 4, 8, 16)
POOL_GROUP = 128
POOL_BUF = 15
CONV_WIDTH = 31
CONV_BUF = 30
N_MEM = 256
X_HEADS = 4
X_HEAD_DIM = 128
X_WIDTH = 512
N_EXPERTS = 8
PAGE = 128
EPS = 1e-6
ATT_SCALE = HEAD_DIM ** -0.5
NEG_BIG = -1e30

LANES = 128
SUBLANES = 8
MIB = 1024 * 1024


def _params(sem, vmem_mib):
    return pltpu.CompilerParams(dimension_semantics=sem, vmem_limit_bytes=vmem_mib * MIB)


def _full(shape):
    return pl.BlockSpec(shape, lambda *_: (0,) * len(shape))


def _rms(x, g):
    return x * lax.rsqrt(jnp.mean(x * x, axis=-1, keepdims=True) + EPS) * g


def _split2(x):
    hi = x.astype(BF16)
    lo = (x - hi.astype(F32)).astype(BF16)
    return hi, lo


def _split3(x):
    hi = x.astype(BF16)
    r = x - hi.astype(F32)
    mid = r.astype(BF16)
    lo = (r - mid.astype(F32)).astype(BF16)
    return hi, mid, lo


def _dot(a, b):
    return jnp.dot(a, b, preferred_element_type=F32)


def _dot_nt(a, b):
    return lax.dot_general(a, b, (((1,), (1,)), ((), ())), preferred_element_type=F32)


def _softplus(z):
    return jnp.maximum(z, 0.0) + jnp.log1p(jnp.exp(-jnp.abs(z)))


def _head_rms(a, g, bd, head_dim):
    hi, lo = _split2(a * a)
    ssq = _dot(hi, bd) + _dot(lo, bd)
    return a * lax.rsqrt(ssq * (1.0 / head_dim) + EPS) * g


def _in_even_kernel(x_ref, g_ref, w_ref, wf_ref, wft_ref, bf_ref, bft_ref, gq_ref, gk_ref, bd_ref,
                    p_ref, q_ref, k32_ref, kb_ref, v32_ref, vb_ref, lf_ref, lft_ref):
    h = _rms(x_ref[...], g_ref[...]).astype(BF16)
    bd = bd_ref[...]
    p_ref[...] = _dot(h, w_ref[:, 0:MIX_HALF])
    q = _head_rms(_dot(h, w_ref[:, MIX_HALF:2 * MIX_HALF]), gq_ref[...], bd, HEAD_DIM)
    q_ref[...] = q
    k = _head_rms(_dot(h, w_ref[:, 2 * MIX_HALF:3 * MIX_HALF]), gk_ref[...], bd, HEAD_DIM)
    k32_ref[...] = k
    kb_ref[...] = k.astype(BF16)
    v = _dot(h, w_ref[:, 3 * MIX_HALF:4 * MIX_HALF])
    v32_ref[...] = v
    vb_ref[...] = v.astype(BF16)
    fl = _dot(h, wf_ref[...]) + bf_ref[...]
    lf_ref[...] = (-_softplus(-fl))[:, 0:N_HEADS]
    flt = _dot_nt(wft_ref[...], h) + bft_ref[...]
    lft_ref[...] = -_softplus(-flt)


def _in_even(x, g, w, wf, wft, bf, bft, gq, gk, bd, tm):
    T = x.shape[0]
    row = lambda n: pl.BlockSpec((tm, n), lambda i: (i, 0))
    sds = jax.ShapeDtypeStruct
    return pl.pallas_call(
        _in_even_kernel,
        grid=(T // tm,),
        in_specs=[row(D_MODEL), _full((1, D_MODEL)), _full((D_MODEL, 4 * MIX_HALF)), _full((D_MODEL, LANES)),
                  _full((N_HEADS, D_MODEL)), _full((1, LANES)), _full((N_HEADS, 1)), _full((1, MIX_HALF)),
                  _full((1, MIX_HALF)), _full((MIX_HALF, MIX_HALF))],
        out_specs=[row(MIX_HALF), row(MIX_HALF), row(MIX_HALF), row(MIX_HALF), row(MIX_HALF), row(MIX_HALF),
                   row(N_HEADS), pl.BlockSpec((N_HEADS, tm), lambda i: (0, i))],
        out_shape=[sds((T, MIX_HALF), F32), sds((T, MIX_HALF), F32), sds((T, MIX_HALF), F32),
                   sds((T, MIX_HALF), BF16), sds((T, MIX_HALF), F32), sds((T, MIX_HALF), BF16),
                   sds((T, N_HEADS), F32), sds((N_HEADS, T), F32)],
        compiler_params=_params(("parallel",), 48),
        name="in_even",
    )(x, g, w, wf, wft, bf, bft, gq, gk, bd)


def _in_odd_kernel(x_ref, g_ref, w_ref, u_ref, q_ref, k32_ref, kb_ref, v32_ref, vb_ref):
    h = _rms(x_ref[...], g_ref[...]).astype(BF16)
    a = _dot(h, w_ref[:, 0:MIX_HALF])
    gate = _dot(h, w_ref[:, MIX_HALF:2 * MIX_HALF])
    u_ref[...] = a * (1.0 / (1.0 + jnp.exp(-gate)))
    q_ref[...] = _dot(h, w_ref[:, 2 * MIX_HALF:3 * MIX_HALF])
    k = _dot(h, w_ref[:, 3 * MIX_HALF:4 * MIX_HALF])
    k32_ref[...] = k
    kb_ref[...] = k.astype(BF16)
    v = _dot(h, w_ref[:, 4 * MIX_HALF:5 * MIX_HALF])
    v32_ref[...] = v
    vb_ref[...] = v.astype(BF16)


def _in_odd(x, g, w, tm):
    T = x.shape[0]
    row = lambda n: pl.BlockSpec((tm, n), lambda i: (i, 0))
    sds = jax.ShapeDtypeStruct
    return pl.pallas_call(
        _in_odd_kernel,
        grid=(T // tm,),
        in_specs=[row(D_MODEL), _full((1, D_MODEL)), _full((D_MODEL, 5 * MIX_HALF))],
        out_specs=[row(MIX_HALF)] * 6,
        out_shape=[sds((T, MIX_HALF), F32), sds((T, MIX_HALF), F32), sds((T, MIX_HALF), F32),
                   sds((T, MIX_HALF), BF16), sds((T, MIX_HALF), F32), sds((T, MIX_HALF), BF16)],
        compiler_params=_params(("parallel",), 48),
        name="in_odd",
    )(x, g, w)


POOL_PAD = 16
CONV_PAD = 32


def _pool_kernel(pre_ref, p_ref, w_ref, sc_ref, y_ref, st_ref, buf, *, ns, L, ch, start_pos):
    for s in range(ns):
        buf[0:POOL_PAD - POOL_BUF, :] = jnp.zeros((POOL_PAD - POOL_BUF, MIX_HALF), F32)
        buf[POOL_PAD - POOL_BUF:POOL_PAD, :] = pre_ref[s]
        buf[POOL_PAD:POOL_PAD + L, :] = p_ref[s]
        st_ref[s] = buf[POOL_PAD + L - POOL_BUF:POOL_PAD + L, :]
        for c0 in range(0, L, ch):
            pos = start_pos + c0 + lax.broadcasted_iota(jnp.int32, (ch, POOL_GROUP), 0)
            outs = []
            for gi, w in enumerate(POOL_WINDOWS):
                ls = slice(gi * POOL_GROUP, (gi + 1) * POOL_GROUP)
                cur = buf[POOL_PAD + c0:POOL_PAD + c0 + ch, ls]
                acc = cur
                for i in range(1, w):
                    acc = acc + buf[POOL_PAD + c0 - i:POOL_PAD + c0 - i + ch, ls]
                cnt = jnp.minimum(pos + 1, w).astype(F32)
                d = acc / cnt - cur
                outs.append(_dot(d.astype(BF16), w_ref[gi]) * sc_ref[:, ls])
            y_ref[s, c0:c0 + ch, :] = jnp.concatenate(outs, axis=-1)


def _pool(prefix, p, w_grp, scale, ns, ch, start_pos):
    B, L, _ = p.shape
    kern = functools.partial(_pool_kernel, ns=ns, L=L, ch=ch, start_pos=start_pos)
    return pl.pallas_call(
        kern,
        grid=(B // ns,),
        in_specs=[pl.BlockSpec((ns, POOL_BUF, MIX_HALF), lambda i: (i, 0, 0)),
                  pl.BlockSpec((ns, L, MIX_HALF), lambda i: (i, 0, 0)),
                  _full((len(POOL_WINDOWS), POOL_GROUP, POOL_GROUP)), _full((1, MIX_HALF))],
        out_specs=[pl.BlockSpec((ns, L, MIX_HALF), lambda i: (i, 0, 0)),
                   pl.BlockSpec((ns, POOL_BUF, MIX_HALF), lambda i: (i, 0, 0))],
        out_shape=[jax.ShapeDtypeStruct((B, L, MIX_HALF), F32),
                   jax.ShapeDtypeStruct((B, POOL_BUF, MIX_HALF), F32)],
        scratch_shapes=[pltpu.VMEM((POOL_PAD + L, MIX_HALF), F32)],
        compiler_params=_params(("parallel",), 40),
        name="pool_mixer",
    )(prefix, p, w_grp, scale)


def _conv_kernel(pre_ref, u_ref, w_ref, b_ref, lg_ref, lb_ref, y_ref, st_ref, buf, *, ns, L, ch):
    for s in range(ns):
        buf[0:CONV_PAD - CONV_BUF, :] = jnp.zeros((CONV_PAD - CONV_BUF, MIX_HALF), F32)
        buf[CONV_PAD - CONV_BUF:CONV_PAD, :] = pre_ref[s]
        buf[CONV_PAD:CONV_PAD + L, :] = u_ref[s]
        st_ref[s] = buf[CONV_PAD + L - CONV_BUF:CONV_PAD + L, :]
        base = CONV_PAD - CONV_BUF
        for c0 in range(0, L, ch):
            acc = buf[base + c0:base + c0 + ch, :] * w_ref[0:1, :]
            for j in range(1, CONV_WIDTH):
                acc = acc + buf[base + c0 + j:base + c0 + j + ch, :] * w_ref[j:j + 1, :]
            y = acc + b_ref[...]
            mu = jnp.mean(y, axis=-1, keepdims=True)
            yc = y - mu
            var = jnp.mean(yc * yc, axis=-1, keepdims=True)
            yn = yc * lax.rsqrt(var + EPS) * lg_ref[...] + lb_ref[...]
            y_ref[s, c0:c0 + ch, :] = yn * (1.0 / (1.0 + jnp.exp(-yn)))


def _conv(prefix, u, conv_w, conv_b, ln_g, ln_b, ns, ch):
    B, L, _ = u.shape
    kern = functools.partial(_conv_kernel, ns=ns, L=L, ch=ch)
    return pl.pallas_call(
        kern,
        grid=(B // ns,),
        in_specs=[pl.BlockSpec((ns, CONV_BUF, MIX_HALF), lambda i: (i, 0, 0)),
                  pl.BlockSpec((ns, L, MIX_HALF), lambda i: (i, 0, 0)),
                  _full((CONV_WIDTH, MIX_HALF)), _full((1, MIX_HALF)), _full((1, MIX_HALF)), _full((1, MIX_HALF))],
        out_specs=[pl.BlockSpec((ns, L, MIX_HALF), lambda i: (i, 0, 0)),
                   pl.BlockSpec((ns, CONV_BUF, MIX_HALF), lambda i: (i, 0, 0))],
        out_shape=[jax.ShapeDtypeStruct((B, L, MIX_HALF), F32),
                   jax.ShapeDtypeStruct((B, CONV_BUF, MIX_HALF), F32)],
        scratch_shapes=[pltpu.VMEM((CONV_PAD + L, MIX_HALF), F32)],
        compiler_params=_params(("parallel",), 40),
        name="conv_module",
    )(prefix, u, conv_w, conv_b, ln_g, ln_b)


def _cumsum_prompt_kernel(lf_ref, lft_ref, tril_ref, triu_ref, f_ref, ft_ref, *, L, blk):
    c_col = jnp.zeros((1, N_HEADS), F32)
    c_row = jnp.zeros((N_HEADS, 1), F32)
    for b0 in range(0, L, blk):
        x = lf_ref[b0:b0 + blk, :]
        xt = lft_ref[:, b0:b0 + blk]
        f = c_col
        for part in _split3(x):
            f = f + _dot(tril_ref[...], part)
        ft = c_row
        for part in _split3(xt):
            ft = ft + _dot(part, triu_ref[...])
        f_ref[b0:b0 + blk, :] = f
        ft_ref[:, b0:b0 + blk] = ft
        c_col = f[blk - 1:blk, :]
        c_row = ft[:, blk - 1:blk]


def _cumsum_prompt(lf, lft, tril, triu, B, L, blk):
    kern = functools.partial(_cumsum_prompt_kernel, L=L, blk=blk)
    return pl.pallas_call(
        kern,
        grid=(B,),
        in_specs=[pl.BlockSpec((L, N_HEADS), lambda b: (b, 0)), pl.BlockSpec((N_HEADS, L), lambda b: (0, b)),
                  _full((blk, blk)), _full((blk, blk))],
        out_specs=[pl.BlockSpec((L, N_HEADS), lambda b: (b, 0)), pl.BlockSpec((N_HEADS, L), lambda b: (0, b))],
        out_shape=[jax.ShapeDtypeStruct((B * L, N_HEADS), F32), jax.ShapeDtypeStruct((N_HEADS, B * L), F32)],
        compiler_params=_params(("parallel",), 32),
        name="fox_cumsum_prompt",
    )(lf, lft, tril, triu)


def _head_pair(q):
    lane = lax.broadcasted_iota(jnp.int32, q.shape, 1)
    zero = jnp.zeros_like(q)
    scale = jnp.asarray(ATT_SCALE, q.dtype)
    return [jnp.where(lane < HEAD_DIM, q, zero) * scale, jnp.where(lane >= HEAD_DIM, q, zero) * scale]


def _stack_heads(q):
    return jnp.concatenate(_head_pair(q), axis=0)


SB_DEAD = -105.0


def _fox_prompt_kernel(q_ref, k_ref, v_ref, f_ref, ft_ref, o_ref, *, blk):
    hp = pl.program_id(1)
    qi = pl.program_id(2)
    qs = _stack_heads(q_ref[...])
    f = f_ref[...]
    lane8 = lax.broadcasted_iota(jnp.int32, f.shape, 1)
    fq = [jnp.sum(jnp.where(lane8 == 2 * hp + i, f, 0.0), axis=1, keepdims=True) for i in range(2)]
    row = lax.broadcasted_iota(jnp.int32, (2 * blk, blk), 0) & (blk - 1)
    col = lax.broadcasted_iota(jnp.int32, (2 * blk, blk), 1)

    def block(ki, carry, diagonal):
        m, l, acc = carry
        ks = pl.multiple_of(ki * blk, blk)
        s = _dot_nt(qs, k_ref[pl.ds(ks, blk), :])
        s = jnp.concatenate([s[i * blk:(i + 1) * blk] + (fq[i] - ft_ref[0, i:i + 1, pl.ds(ks, blk)])
                             for i in range(2)], axis=0)
        if diagonal:
            s = jnp.where(col <= row, s, -jnp.inf)
        m_new = jnp.maximum(m, jnp.max(s, axis=1, keepdims=True))
        alpha = jnp.exp(m - m_new)
        p = jnp.exp(s - m_new)
        l = alpha * l + jnp.sum(p, axis=1, keepdims=True)
        acc = alpha * acc + _dot(p.astype(BF16), v_ref[pl.ds(ks, blk), :])
        return m_new, l, acc

    init = (jnp.full((2 * blk, 1), NEG_BIG, F32), jnp.zeros((2 * blk, 1), F32), jnp.zeros((2 * blk, LANES), F32))
    carry = lax.fori_loop(0, qi, lambda ki, c: block(ki, c, False), init)
    _, l, acc = block(qi, carry, True)
    o = acc / l
    lane = lax.broadcasted_iota(jnp.int32, (blk, LANES), 1)
    o_ref[...] = jnp.where(lane < HEAD_DIM, o[0:blk], o[blk:2 * blk])


def _fox_prompt(q, k, v, f, ft, B, L, blk):
    nq = L // blk
    kern = functools.partial(_fox_prompt_kernel, blk=blk)
    return pl.pallas_call(
        kern,
        grid=(B, N_HEADS // 2, nq),
        in_specs=[pl.BlockSpec((blk, LANES), lambda b, h, i: (b * nq + i, h)),
                  pl.BlockSpec((L, LANES), lambda b, h, i: (b, h)),
                  pl.BlockSpec((L, LANES), lambda b, h, i: (b, h)),
                  pl.BlockSpec((blk, N_HEADS), lambda b, h, i: (b * nq + i, 0)),
                  pl.BlockSpec((1, 2, L), lambda b, h, i: (h, 0, b))],
        out_specs=pl.BlockSpec((blk, LANES), lambda b, h, i: (b * nq + i, h)),
        out_shape=jax.ShapeDtypeStruct((B * L, MIX_HALF), F32),
        compiler_params=_params(("parallel", "parallel", "arbitrary"), 32),
        name="fox_attention_prompt",
    )(q, k, v, f, ft)


def _sb_weights(z, c, u, mask):
    sp = _softplus(z)
    ln = -sp
    if mask is not None:
        ln = jnp.where(mask, ln, 0.0)
    hi, lo = _split2(ln)
    e = _dot(hi, u) + _dot(lo, u)
    a = jnp.exp((z - sp) + e + c)
    if mask is not None:
        a = jnp.where(mask, a, 0.0)
    return a, c + jnp.sum(ln, axis=1, keepdims=True)


def _sb_block(z, c, v, u, mask):
    a, c = _sb_weights(z, c, u, mask)
    return _dot(a.astype(BF16), v), c


def _sb_prompt_kernel(q_ref, k_ref, v_ref, u_ref, o_ref, *, blk):
    qi = pl.program_id(2)
    qs = _stack_heads(q_ref[...])
    row = lax.broadcasted_iota(jnp.int32, (2 * blk, blk), 0) & (blk - 1)
    col = lax.broadcasted_iota(jnp.int32, (2 * blk, blk), 1)
    u = u_ref[...]

    def block(ki, c, acc, diagonal):
        ks = pl.multiple_of(ki * blk, blk)
        z = _dot_nt(qs, k_ref[pl.ds(ks, blk), :])
        pv, c = _sb_block(z, c, v_ref[pl.ds(ks, blk), :], u, (col < row) if diagonal else None)
        return c, acc + pv

    c, acc = block(qi, jnp.zeros((2 * blk, 1), F32), jnp.zeros((2 * blk, LANES), F32), True)

    def cond(state):
        j, alive, _, _ = state
        return (j < qi) & (alive > 0)

    def body(state):
        j, _, c, acc = state
        c, acc = block(qi - 1 - j, c, acc, False)
        return j + 1, (jnp.max(c) > SB_DEAD).astype(jnp.int32), c, acc

    _, _, _, acc = lax.while_loop(cond, body, (jnp.int32(0), jnp.int32(1), c, acc))
    lane = lax.broadcasted_iota(jnp.int32, (blk, LANES), 1)
    o_ref[...] = jnp.where(lane < HEAD_DIM, acc[0:blk], acc[blk:2 * blk])


def _sb_prompt(q, k, v, u, B, L, blk):
    nq = L // blk
    kern = functools.partial(_sb_prompt_kernel, blk=blk)
    return pl.pallas_call(
        kern,
        grid=(B, N_HEADS // 2, nq),
        in_specs=[pl.BlockSpec((blk, LANES), lambda b, h, i: (b * nq + i, h)),
                  pl.BlockSpec((L, LANES), lambda b, h, i: (b, h)),
                  pl.BlockSpec((L, LANES), lambda b, h, i: (b, h)),
                  _full((blk, blk))],
        out_specs=pl.BlockSpec((blk, LANES), lambda b, h, i: (b * nq + i, h)),
        out_shape=jax.ShapeDtypeStruct((B * L, MIX_HALF), F32),
        compiler_params=_params(("parallel", "parallel", "arbitrary"), 32),
        name="sb_attention_prompt",
    )(q, k, v, u)


N_PAGES = 16
DEC_SEQ = 8
BD_ROWS = DEC_SEQ * N_HEADS


def _block_diag_q(q, hm):
    rows = [jnp.broadcast_to(q[t:t + 1, :], (N_HEADS, MIX_HALF)) * hm for t in range(DEC_SEQ)]
    return (jnp.concatenate(rows, axis=0) * ATT_SCALE).astype(BF16)


def _pad_new(x):
    return jnp.concatenate([x, jnp.zeros((PAGE - DEC_SEQ, x.shape[1]), x.dtype)], axis=0)


def _merge_heads(o, hm):
    return jnp.concatenate(
        [jnp.sum(o[t * N_HEADS:(t + 1) * N_HEADS, :] * hm, axis=0, keepdims=True) for t in range(DEC_SEQ)], axis=0)


def _new_key_mask(strict):
    row = lax.broadcasted_iota(jnp.int32, (BD_ROWS, PAGE), 0) // N_HEADS
    col = lax.broadcasted_iota(jnp.int32, (BD_ROWS, PAGE), 1)
    return (col < row) if strict else (col <= row)


def _page_scores(qbd, k_pages, kn_ref, j):
    if j < N_PAGES:
        return _dot(qbd, k_pages[j][0].reshape(MIX_HALF, PAGE).astype(BF16))
    return _dot_nt(qbd, _pad_new(kn_ref[...]).astype(BF16))


def _page_values(w, v_pages, vn_ref, j):
    if j < N_PAGES:
        return _dot_nt(w, v_pages[j][0].reshape(MIX_HALF, PAGE).astype(BF16))
    return _dot(w, _pad_new(vn_ref[...]).astype(BF16))


def _cumsum_lanes(x):
    n = x.shape[1]
    lane = lax.broadcasted_iota(jnp.int32, x.shape, 1)
    k = 1
    while k < n:
        x = x + jnp.where(lane >= k, pltpu.roll(x, k, 1), 0.0)
        k *= 2
    return x


def _fox_decode_kernel(pt_ref, *refs):
    k_pages = refs[0:N_PAGES]
    v_pages = refs[N_PAGES:2 * N_PAGES]
    lf_pages = refs[2 * N_PAGES:3 * N_PAGES]
    q_ref, kn_ref, vn_ref, lftn_ref, hm_ref, o_ref = refs[3 * N_PAGES:]
    del pt_ref
    hm = hm_ref[...]
    qbd = _block_diag_q(q_ref[...], hm)
    parts = [lf_pages[j][0] for j in range(N_PAGES)]
    parts.append(jnp.concatenate([lftn_ref[0], jnp.zeros((N_HEADS, PAGE - DEC_SEQ), F32)], axis=1))
    fk = _cumsum_lanes(jnp.concatenate(parts, axis=1))
    fnew = fk[:, N_PAGES * PAGE:]
    lane = lax.broadcasted_iota(jnp.int32, (N_HEADS, PAGE), 1)
    fq = jnp.concatenate([jnp.sum(jnp.where(lane == t, fnew, 0.0), axis=1, keepdims=True)
                          for t in range(DEC_SEQ)], axis=0)
    scores = []
    for j in range(N_PAGES + 1):
        bias = jnp.concatenate([fk[:, j * PAGE:(j + 1) * PAGE]] * DEC_SEQ, axis=0)
        s = _page_scores(qbd, k_pages, kn_ref, j) + (fq - bias)
        if j == N_PAGES:
            s = jnp.where(_new_key_mask(False), s, -jnp.inf)
        scores.append(s)
    m = scores[0].max(axis=1, keepdims=True)
    for s in scores[1:]:
        m = jnp.maximum(m, s.max(axis=1, keepdims=True))
    l = jnp.zeros((BD_ROWS, 1), F32)
    acc = jnp.zeros((BD_ROWS, MIX_HALF), F32)
    for j in range(N_PAGES + 1):
        p = jnp.exp(scores[j] - m)
        l = l + jnp.sum(p, axis=1, keepdims=True)
        acc = acc + _page_values(p.astype(BF16), v_pages, vn_ref, j)
    o_ref[...] = _merge_heads(acc / l, hm)


def _sb_decode_kernel(pt_ref, q_ref, kn_ref, vn_ref, hm_ref, u_ref, ck_hbm, cv_hbm, o_ref, kbuf, vbuf, ksem, vsem):
    b = pl.program_id(0)
    nb = pl.num_programs(0)
    page_slot = lambda j: (N_PAGES - 1 - j) % 2

    def page_copies(seq, j, slot):
        page = pt_ref[seq, j]
        return (pltpu.make_async_copy(ck_hbm.at[page], kbuf.at[slot], ksem.at[slot]),
                pltpu.make_async_copy(cv_hbm.at[page], vbuf.at[slot], vsem.at[slot]))

    def start(seq, j, slot):
        for cp in page_copies(seq, j, slot):
            cp.start()

    def wait(slot):
        for cp in page_copies(b, 0, slot):
            cp.wait()

    @pl.when(b == 0)
    def _():
        start(0, N_PAGES - 1, 0)

    hm = hm_ref[...]
    qbd = _block_diag_q(q_ref[...], hm)
    u = u_ref[...]
    a, c = _sb_weights(_dot_nt(qbd, _pad_new(kn_ref[...]).astype(BF16)), jnp.zeros((BD_ROWS, 1), F32), u,
                       _new_key_mask(True))
    acc = _dot(a.astype(BF16), _pad_new(vn_ref[...]).astype(BF16))
    is_alive = lambda c: (jnp.max(c) > SB_DEAD).astype(jnp.int32)

    def cond(state):
        j, alive, _, _ = state
        return (j >= 0) & (alive > 0)

    def body(state):
        j, _, c, acc = state
        slot = page_slot(j)
        wait(slot)

        @pl.when(j > 0)
        def _():
            start(b, j - 1, 1 - slot)

        a, c = _sb_weights(_dot(qbd, kbuf[slot].reshape(MIX_HALF, PAGE).astype(BF16)), c, u, None)
        acc = acc + _dot_nt(a.astype(BF16), vbuf[slot].reshape(MIX_HALF, PAGE).astype(BF16))
        return j - 1, is_alive(c), c, acc

    j, _, _, acc = lax.while_loop(cond, body, (jnp.int32(N_PAGES - 1), is_alive(c), c, acc))

    @pl.when(j >= 0)
    def _():
        wait(page_slot(j))

    @pl.when(b + 1 < nb)
    def _():
        start(b + 1, N_PAGES - 1, 0)

    o_ref[...] = _merge_heads(acc, hm)


def _page_specs(shape_tail, n):
    specs = []
    for j in range(n):
        specs.append(pl.BlockSpec((1,) + shape_tail, functools.partial(
            lambda b, pt, j: (pt[b, j],) + (0,) * len(shape_tail), j=j)))
    return specs


def _fox_decode(page_table, ck, cv, clf, q, kn, vn, lftn, hm):
    nb = page_table.shape[0]
    seq = lambda n: pl.BlockSpec((DEC_SEQ, n), lambda b, pt: (b, 0))
    const = lambda shape: pl.BlockSpec(shape, lambda b, pt: (0,) * len(shape))
    kv_page = (N_HEADS, HEAD_DIM, PAGE)
    in_specs = (_page_specs(kv_page, N_PAGES) + _page_specs(kv_page, N_PAGES) + _page_specs((N_HEADS, PAGE), N_PAGES)
                + [seq(MIX_HALF), seq(MIX_HALF), seq(MIX_HALF),
                   pl.BlockSpec((1, N_HEADS, DEC_SEQ), lambda b, pt: (b, 0, 0)), const((N_HEADS, MIX_HALF))])
    return pl.pallas_call(
        _fox_decode_kernel,
        grid_spec=pltpu.PrefetchScalarGridSpec(
            num_scalar_prefetch=1, grid=(nb,), in_specs=in_specs, out_specs=seq(MIX_HALF)),
        out_shape=jax.ShapeDtypeStruct((nb * DEC_SEQ, MIX_HALF), F32),
        compiler_params=_params(("arbitrary",), 48),
        name="fox_attention_decode",
    )(page_table, *([ck] * N_PAGES), *([cv] * N_PAGES), *([clf] * N_PAGES), q, kn, vn, lftn, hm)


def _sb_decode(page_table, ck, cv, q, kn, vn, hm, u):
    nb = page_table.shape[0]
    seq = lambda n: pl.BlockSpec((DEC_SEQ, n), lambda b, pt: (b, 0))
    const = lambda shape: pl.BlockSpec(shape, lambda b, pt: (0,) * len(shape))
    kv_page = (N_HEADS, HEAD_DIM, PAGE)
    in_specs = [seq(MIX_HALF), seq(MIX_HALF), seq(MIX_HALF), const((N_HEADS, MIX_HALF)), const((PAGE, PAGE)),
                pl.BlockSpec(memory_space=pl.ANY), pl.BlockSpec(memory_space=pl.ANY)]
    return pl.pallas_call(
        _sb_decode_kernel,
        grid_spec=pltpu.PrefetchScalarGridSpec(
            num_scalar_prefetch=1, grid=(nb,), in_specs=in_specs, out_specs=seq(MIX_HALF),
            scratch_shapes=[pltpu.VMEM((2,) + kv_page, F32), pltpu.VMEM((2,) + kv_page, F32),
                            pltpu.SemaphoreType.DMA((2,)), pltpu.SemaphoreType.DMA((2,))]),
        out_shape=jax.ShapeDtypeStruct((nb * DEC_SEQ, MIX_HALF), F32),
        compiler_params=_params(("arbitrary",), 32),
        name="sb_attention_decode",
    )(page_table, q, kn, vn, hm, u, ck, cv)


def _x_head_rms(a, g):
    outs = []
    for h in range(X_HEADS):
        ah = a[:, h * X_HEAD_DIM:(h + 1) * X_HEAD_DIM]
        outs.append(ah * lax.rsqrt(jnp.mean(ah * ah, axis=-1, keepdims=True) + EPS))
    return jnp.concatenate(outs, axis=-1) * g


def _mix_out_kernel(x_ref, ya_ref, yb_ref, wo_ref, g_ref, wq_ref, gq_ref, x1_ref, qc_ref):
    y = _dot(ya_ref[...].astype(BF16), wo_ref[0:MIX_HALF, :]) + _dot(yb_ref[...].astype(BF16), wo_ref[MIX_HALF:, :])
    x1 = x_ref[...] + y
    x1_ref[...] = x1
    hc = _rms(x1, g_ref[...]).astype(BF16)
    qc_ref[...] = _x_head_rms(_dot(hc, wq_ref[...]), gq_ref[...])


def _mix_out(x, ya, yb, wo, g, wq, gq, tm):
    T = x.shape[0]
    row = lambda n: pl.BlockSpec((tm, n), lambda i: (i, 0))
    return pl.pallas_call(
        _mix_out_kernel,
        grid=(T // tm,),
        in_specs=[row(D_MODEL), row(MIX_HALF), row(MIX_HALF), _full((D_MODEL, D_MODEL)), _full((1, D_MODEL)),
                  _full((D_MODEL, X_WIDTH)), _full((1, X_WIDTH))],
        out_specs=[row(D_MODEL), row(X_WIDTH)],
        out_shape=[jax.ShapeDtypeStruct((T, D_MODEL), F32), jax.ShapeDtypeStruct((T, X_WIDTH), F32)],
        compiler_params=_params(("parallel",), 40),
        name="mixer_out_proj",
    )(x, ya, yb, wo, g, wq, gq)


def _mem_kv_kernel(m_ref, wk_ref, wv_ref, gk_ref, k_ref, v_ref, *, nb):
    m = m_ref[...].astype(BF16)
    k = _x_head_rms(_dot(m, wk_ref[0]), gk_ref[0])
    v = _dot(m, wv_ref[0])
    for b in range(nb):
        for h in range(X_HEADS):
            ls = slice(h * X_HEAD_DIM, (h + 1) * X_HEAD_DIM)
            k_ref[0, b, :, h, :] = k[b * N_MEM:(b + 1) * N_MEM, ls]
            v_ref[0, b, :, h, :] = v[b * N_MEM:(b + 1) * N_MEM, ls]


def _mem_kv(mem, wk, wv, gk, nb):
    B = mem.shape[0] // N_MEM
    depth = wk.shape[0]
    out = pl.BlockSpec((1, nb, N_MEM, X_HEADS, X_HEAD_DIM), lambda l, i: (l, i, 0, 0, 0))
    return pl.pallas_call(
        functools.partial(_mem_kv_kernel, nb=nb),
        grid=(depth, B // nb),
        in_specs=[pl.BlockSpec((nb * N_MEM, D_MODEL), lambda l, i: (i, 0)),
                  pl.BlockSpec((1, D_MODEL, X_WIDTH), lambda l, i: (l, 0, 0)),
                  pl.BlockSpec((1, D_MODEL, X_WIDTH), lambda l, i: (l, 0, 0)),
                  pl.BlockSpec((1, 1, X_WIDTH), lambda l, i: (l, 0, 0))],
        out_specs=[out, out],
        out_shape=[jax.ShapeDtypeStruct((depth, B, N_MEM, X_HEADS, X_HEAD_DIM), F32)] * 2,
        compiler_params=_params(("parallel", "parallel"), 32),
        name="memory_kv",
    )(mem, wk, wv, gk)


def _cross_kernel(x_ref, q_ref, mk_ref, mv_ref, wo_ref, g_ref, x2_ref, hf_ref, *, ns, tq):
    outs = []
    for s in range(ns):
        q = q_ref[s * tq:(s + 1) * tq, :]
        heads = []
        for h in range(X_HEADS):
            ls = slice(h * X_HEAD_DIM, (h + 1) * X_HEAD_DIM)
            qh = (q[:, ls] * (X_HEAD_DIM ** -0.5)).astype(BF16)
            sc = _dot_nt(qh, mk_ref[0, s, :, h, :].astype(BF16))
            p = jnp.exp(sc - jnp.max(sc, axis=1, keepdims=True))
            o = _dot(p.astype(BF16), mv_ref[0, s, :, h, :].astype(BF16))
            heads.append(o / jnp.sum(p, axis=1, keepdims=True))
        outs.append(jnp.concatenate(heads, axis=-1))
    o = outs[0] if ns == 1 else jnp.concatenate(outs, axis=0)
    x2 = x_ref[...] + _dot(o.astype(BF16), wo_ref[...])
    x2_ref[...] = x2
    hf_ref[...] = _rms(x2, g_ref[...])


def _cross(x, q, mk, mv, layer, wo, g, ns, tq):
    T = x.shape[0]
    L = T // mk.shape[1]
    nq = L // tq
    rows = ns * tq
    row = lambda n: pl.BlockSpec((rows, n), lambda i: (i, 0))
    mem = pl.BlockSpec((1, ns, N_MEM, X_HEADS, X_HEAD_DIM), lambda i: (layer, i // nq, 0, 0, 0))
    kern = functools.partial(_cross_kernel, ns=ns, tq=tq)
    return pl.pallas_call(
        kern,
        grid=(T // rows,),
        in_specs=[row(D_MODEL), row(X_WIDTH), mem, mem, _full((X_WIDTH, D_MODEL)), _full((1, D_MODEL))],
        out_specs=[row(D_MODEL), row(D_MODEL)],
        out_shape=[jax.ShapeDtypeStruct((T, D_MODEL), F32)] * 2,
        compiler_params=_params(("parallel",), 48),
        name="cross_attention",
    )(x, q, mk, mv, wo, g)


def _swiglu_kernel(h_ref, x_ref, wg_ref, wu_ref, wd_ref, o_ref, hb, acc):
    f = pl.program_id(1)

    @pl.when(f == 0)
    def _():
        hb[...] = h_ref[...].astype(BF16)
        acc[...] = jnp.zeros_like(acc)

    h = hb[...]
    gte = _dot(h, wg_ref[...])
    act = gte * (1.0 / (1.0 + jnp.exp(-gte))) * _dot(h, wu_ref[...])
    acc[...] += _dot(act.astype(BF16), wd_ref[...])

    @pl.when(f == pl.num_programs(1) - 1)
    def _():
        o_ref[...] = x_ref[...] + acc[...]


def _swiglu(h, x, wg, wu, wd, tm, tf):
    T = h.shape[0]
    dff = wg.shape[1]
    row = pl.BlockSpec((tm, D_MODEL), lambda i, f: (i, 0))
    return pl.pallas_call(
        _swiglu_kernel,
        grid=(T // tm, dff // tf),
        in_specs=[row, row, pl.BlockSpec((D_MODEL, tf), lambda i, f: (0, f)),
                  pl.BlockSpec((D_MODEL, tf), lambda i, f: (0, f)), pl.BlockSpec((tf, D_MODEL), lambda i, f: (f, 0))],
        out_specs=row,
        out_shape=jax.ShapeDtypeStruct((T, D_MODEL), F32),
        scratch_shapes=[pltpu.VMEM((tm, D_MODEL), BF16), pltpu.VMEM((tm, D_MODEL), F32)],
        compiler_params=_params(("parallel", "arbitrary"), 56),
        name="dense_swiglu",
    )(h, x, wg, wu, wd)


def _router_kernel(h_ref, w_ref, b_ref, r_ref):
    h_hi, h_lo = _split2(h_ref[...])
    w_hi, w_lo = _split2(w_ref[...])
    logits = _dot(h_hi, w_hi) + _dot(h_hi, w_lo) + _dot(h_lo, w_hi) + b_ref[...]
    lane = lax.broadcasted_iota(jnp.int32, logits.shape, 1)
    logits = jnp.where(lane < N_EXPERTS, logits, -jnp.inf)
    m1 = jnp.max(logits, axis=1, keepdims=True)
    i1 = jnp.min(jnp.where(logits == m1, lane, LANES), axis=1, keepdims=True)
    rest = jnp.where(lane == i1, -jnp.inf, logits)
    m2 = jnp.max(rest, axis=1, keepdims=True)
    i2 = jnp.min(jnp.where(rest == m2, lane, LANES), axis=1, keepdims=True)
    e2 = jnp.exp(m2 - m1)
    g1 = 1.0 / (1.0 + e2)
    g2 = e2 / (1.0 + e2)
    r_ref[...] = jnp.where(lane == 0, i1.astype(F32), jnp.where(lane == 1, i2.astype(F32),
                           jnp.where(lane == 2, g1, jnp.where(lane == 3, g2, 0.0))))


def _router(h, w, b, tm):
    T = h.shape[0]
    return pl.pallas_call(
        _router_kernel,
        grid=(T // tm,),
        in_specs=[pl.BlockSpec((tm, D_MODEL), lambda i: (i, 0)), _full((D_MODEL, LANES)), _full((1, LANES))],
        out_specs=pl.BlockSpec((tm, LANES), lambda i: (i, 0)),
        out_shape=jax.ShapeDtypeStruct((T, LANES), F32),
        compiler_params=_params(("parallel",), 32),
        name="moe_router",
    )(h, w, b)


MOE_TM = 512
MOE_NF = 2
COMBINE_TM = 512


def _moe_plan(r, tm):
    T = r.shape[0]
    n_pairs = 2 * T
    e = r[:, 0:2].astype(jnp.int32).T.reshape(-1)
    onehot = (e[:, None] == jnp.arange(N_EXPERTS, dtype=jnp.int32)[None, :]).astype(jnp.int32)
    csum = jnp.cumsum(onehot, axis=0)
    rank = jnp.sum(csum * onehot, axis=1) - 1
    psz = (csum[-1] + tm - 1) // tm * tm
    gend = jnp.cumsum(psz)
    pos = (gend - psz)[e] + rank
    n_rows = n_pairs + N_EXPERTS * tm
    n_tiles = n_rows // tm
    src = jnp.zeros((n_rows,), jnp.int32).at[pos].set(jnp.arange(n_pairs, dtype=jnp.int32) % T)
    t0 = jnp.arange(n_tiles, dtype=jnp.int32) * tm
    tile_e = jnp.minimum(jnp.sum((t0[:, None] >= gend[None, :]).astype(jnp.int32), axis=1), N_EXPERTS - 1)
    n_valid = (gend[-1] // tm).astype(jnp.int32).reshape(1)
    return src.reshape(n_tiles, 1, tm), pos.astype(jnp.int32), tile_e, n_valid


def _row_gather(src_hbm, idx_ref, dst, sem, r0, n, unrolled):
    def start(r):
        pltpu.make_async_copy(src_hbm.at[pl.ds(idx_ref[0, 0, r0 + r], 1)], dst.at[pl.ds(r0 + r, 1)], sem).start()

    if unrolled:
        for r in range(n):
            start(r)
    else:
        def body(r, carry):
            start(r)
            return carry
        lax.fori_loop(0, n, body, 0, unroll=8)


def _moe_kernel(te_ref, nv_ref, src_ref, srcn_ref, h_hbm, wg_ref, wu_ref, wd_ref, y_ref, xin, hb, acc, gsem, *, tm):
    del te_ref
    i = pl.program_id(0)
    f = pl.program_id(1)
    nt = pl.num_programs(0)
    nf = pl.num_programs(1)
    slot = i % 2
    is_valid = i < nv_ref[0]
    part = tm // MOE_NF
    r0 = f * part

    def wait_tile(s):
        pltpu.make_async_copy(h_hbm.at[pl.ds(0, tm)], xin.at[s], gsem.at[s]).wait()

    @pl.when((i == 0) & (f == 0))
    def _():
        _row_gather(h_hbm, src_ref, xin.at[0], gsem.at[0], 0, tm, False)

    @pl.when(f == 0)
    def _():
        wait_tile(slot)
        hb[...] = xin[slot].astype(BF16)
        acc[...] = jnp.zeros_like(acc)

    @pl.when(is_valid)
    def _():
        _row_gather(h_hbm, srcn_ref, xin.at[1 - slot], gsem.at[1 - slot], r0, part, True)
        h = hb[...]
        gte = _dot(h, wg_ref[0])
        act = gte * (1.0 / (1.0 + jnp.exp(-gte))) * _dot(h, wu_ref[0])
        acc[...] += _dot(act.astype(BF16), wd_ref[0])

    @pl.when(jnp.logical_not(is_valid))
    def _():
        _row_gather(h_hbm, srcn_ref, xin.at[1 - slot], gsem.at[1 - slot], r0, part, False)

    @pl.when(f == nf - 1)
    def _():
        y_ref[...] = acc[...]

    @pl.when((i == nt - 1) & (f == nf - 1))
    def _():
        wait_tile(1 - slot)


def _moe(h, src, tile_e, n_valid, wg, wu, wd, tm):
    n_tiles = src.shape[0]
    nf = MOE_NF
    tf = wg.shape[2] // nf
    fser = lambda i, f: jnp.where(i % 2 == 0, f, nf - 1 - f)
    smem_tile = lambda imap: pl.BlockSpec((1, 1, tm), imap, memory_space=pltpu.SMEM)
    kern = functools.partial(_moe_kernel, tm=tm)
    return pl.pallas_call(
        kern,
        grid_spec=pltpu.PrefetchScalarGridSpec(
            num_scalar_prefetch=2,
            grid=(n_tiles, nf),
            in_specs=[smem_tile(lambda i, f, te, nv: (i, 0, 0)),
                      smem_tile(lambda i, f, te, nv: (jnp.minimum(i + 1, n_tiles - 1), 0, 0)),
                      pl.BlockSpec(memory_space=pl.ANY),
                      pl.BlockSpec((1, D_MODEL, tf), lambda i, f, te, nv: (te[i], 0, fser(i, f))),
                      pl.BlockSpec((1, D_MODEL, tf), lambda i, f, te, nv: (te[i], 0, fser(i, f))),
                      pl.BlockSpec((1, tf, D_MODEL), lambda i, f, te, nv: (te[i], fser(i, f), 0))],
            out_specs=pl.BlockSpec((tm, D_MODEL), lambda i, f, te, nv: (i, 0)),
            scratch_shapes=[pltpu.VMEM((2, tm, D_MODEL), F32), pltpu.VMEM((tm, D_MODEL), BF16),
                            pltpu.VMEM((tm, D_MODEL), F32), pltpu.SemaphoreType.DMA((2,))]),
        out_shape=jax.ShapeDtypeStruct((n_tiles * tm, D_MODEL), F32),
        compiler_params=_params(("arbitrary", "arbitrary"), 56),
        name="moe_swiglu",
    )(tile_e, n_valid, src, src, h, wg, wu, wd)


def _combine_kernel(pos_ref, posn_ref, x_ref, r_ref, y_hbm, o_ref, buf, sem, *, tm):
    i = pl.program_id(0)
    nt = pl.num_programs(0)
    slot = i % 2

    @pl.when(i == 0)
    def _():
        _row_gather(y_hbm, pos_ref, buf.at[0], sem.at[0], 0, 2 * tm, False)

    @pl.when(i + 1 < nt)
    def _():
        _row_gather(y_hbm, posn_ref, buf.at[1 - slot], sem.at[1 - slot], 0, 2 * tm, False)

    pltpu.make_async_copy(y_hbm.at[pl.ds(0, 2 * tm)], buf.at[slot], sem.at[slot]).wait()
    r = r_ref[...]
    o_ref[...] = x_ref[...] + (r[:, 2:3] * buf[slot, 0:tm, :] + r[:, 3:4] * buf[slot, tm:2 * tm, :])


def _combine(x, r, pos, y, tm):
    T = x.shape[0]
    nt = T // tm
    smem_tile = lambda imap: pl.BlockSpec((1, 1, 2 * tm), imap, memory_space=pltpu.SMEM)
    return pl.pallas_call(
        functools.partial(_combine_kernel, tm=tm),
        grid=(nt,),
        in_specs=[smem_tile(lambda i: (i, 0, 0)), smem_tile(lambda i: (jnp.minimum(i + 1, nt - 1), 0, 0)),
                  pl.BlockSpec((tm, D_MODEL), lambda i: (i, 0)), pl.BlockSpec((tm, LANES), lambda i: (i, 0)),
                  pl.BlockSpec(memory_space=pl.ANY)],
        out_specs=pl.BlockSpec((tm, D_MODEL), lambda i: (i, 0)),
        out_shape=jax.ShapeDtypeStruct((T, D_MODEL), F32),
        scratch_shapes=[pltpu.VMEM((2, 2 * tm, D_MODEL), F32), pltpu.SemaphoreType.DMA((2,))],
        compiler_params=_params(("arbitrary",), 40),
        name="moe_combine",
    )(pos, pos, x, r, y)


def _tri(n, kind):
    r = lax.broadcasted_iota(jnp.int32, (n, n), 0)
    c = lax.broadcasted_iota(jnp.int32, (n, n), 1)
    m = {"lower_incl": r >= c, "upper_incl": r <= c, "lower_strict": r > c}[kind]
    return m.astype(BF16)


def _head_mask(n_heads, head_dim):
    h = lax.broadcasted_iota(jnp.int32, (n_heads, n_heads * head_dim), 0)
    c = lax.broadcasted_iota(jnp.int32, (n_heads, n_heads * head_dim), 1) // head_dim
    return (h == c).astype(F32)


def _stack(x, mem_k, mem_v, W, group):
    T = x.shape[0]
    tm = 512
    B, L = group["B"], group["L"]
    prompt = group["kind"] == "prompt"
    new = {}

    p, q, k32, kb, v32, vb, lf, lft = _in_even(x, W["g_mix0"], W["w_in_e"], W["wf"], W["wft"], W["bf"], W["bft"],
                                               W["g_fq"], W["g_fk"], W["bd64"], tm)
    ya, pool_state = _pool(group["pool_prefix"], p.reshape(B, L, MIX_HALF), W["w_pool"], W["pool_scale"],
                           ns=group["seq_per_step"], ch=group["chunk"], start_pos=group["start_pos"])
    if prompt:
        f, ft = _cumsum_prompt(lf, lft, W["tril256"], W["triu256"], B, L, 256)
        yb = _fox_prompt(q.astype(BF16), kb, vb, f, ft.reshape(N_HEADS // 2, 2, T), B, L, 256)
    else:
        lftn = lft.reshape(N_HEADS, B, L).transpose(1, 0, 2)
        yb = _fox_decode(group["page_table"], group["fox_k"], group["fox_v"], group["fox_lf"], q, k32, v32, lftn,
                         W["hm64"])
    new.update(fox_k=k32, fox_v=v32, fox_logf=lf, pool=pool_state)
    x, qc = _mix_out(x, ya.reshape(T, MIX_HALF), yb, W["w_out_e"], W["g_cross0"], W["w_cq0"], W["g_cq0"], tm)
    x, hf = _cross(x, qc, mem_k, mem_v, 0, W["w_co0"], W["g_ffn0"], ns=group["x_ns"], tq=group["x_tq"])
    x = _swiglu(hf, x, W["w_gate"], W["w_up"], W["w_down"], tm, 1408)

    u, q, k32, kb, v32, vb = _in_odd(x, W["g_mix1"], W["w_in_o"], tm)
    yc, conv_state = _conv(group["conv_prefix"], u.reshape(B, L, MIX_HALF), W["conv_w"], W["conv_b"], W["ln_g"],
                           W["ln_b"], ns=group["seq_per_step"], ch=group["conv_chunk"])
    if prompt:
        yd = _sb_prompt(q.astype(BF16), kb, vb, W["ustrict256"], B, L, 256)
    else:
        yd = _sb_decode(group["page_table"], group["sb_k"], group["sb_v"], q, k32, v32, W["hm64"], W["ustrict128"])
    new.update(sb_k=k32, sb_v=v32, conv=conv_state)
    x, qc = _mix_out(x, yc.reshape(T, MIX_HALF), yd, W["w_out_o"], W["g_cross1"], W["w_cq1"], W["g_cq1"], tm)
    x, hf = _cross(x, qc, mem_k, mem_v, 1, W["w_co1"], W["g_ffn1"], ns=group["x_ns"], tq=group["x_tq"])
    r = _router(hf, W["w_router"], W["b_router"], tm)
    return x, hf, r, new


def kernel(x_prompt, x_sample, cache_fox_k, cache_fox_v, cache_fox_logf, cache_sb_k, cache_sb_v, cache_mem_k, cache_mem_v, state_pool, state_conv, page_table, mem_prompt, norm_mix, norm_cross, norm_ffn, w_cq, w_ck, w_cv, w_co, g_cq, g_ck, w_in_e, b_f, w_pool_grp, pool_scale, g_fq, g_fk, w_out_e, w_gate, w_up, w_down, w_in_o, conv_w, conv_b, ln_g, ln_b, w_out_o, w_router, b_router, we_gate, we_up, we_down):
    B, L, D = x_prompt.shape
    SB, SL, _ = x_sample.shape
    n_pool = cache_fox_k.shape[1]
    bf = lambda a: a.astype(BF16)
    row = lambda a: a.reshape(1, -1)

    wf = jnp.pad(w_in_e[0][:, 4 * MIX_HALF:], ((0, 0), (0, LANES - N_HEADS)))
    W = dict(
        g_mix0=row(norm_mix[0]), g_mix1=row(norm_mix[1]), g_cross0=row(norm_cross[0]), g_cross1=row(norm_cross[1]),
        g_ffn0=row(norm_ffn[0]), g_ffn1=row(norm_ffn[1]),
        w_in_e=bf(w_in_e[0][:, :4 * MIX_HALF]), wf=bf(wf), wft=bf(w_in_e[0][:, 4 * MIX_HALF:].T),
        bf=jnp.pad(row(b_f[0]), ((0, 0), (0, LANES - N_HEADS))), bft=b_f[0].reshape(N_HEADS, 1),
        g_fq=row(jnp.tile(g_fq[0], N_HEADS)), g_fk=row(jnp.tile(g_fk[0], N_HEADS)),
        w_pool=bf(w_pool_grp[0]), pool_scale=row(pool_scale[0]), w_out_e=bf(w_out_e[0]),
        w_cq0=bf(w_cq[0]), w_cq1=bf(w_cq[1]), w_co0=bf(w_co[0]), w_co1=bf(w_co[1]),
        g_cq0=row(jnp.tile(g_cq[0], X_HEADS)), g_cq1=row(jnp.tile(g_cq[1], X_HEADS)),
        w_gate=bf(w_gate[0]), w_up=bf(w_up[0]), w_down=bf(w_down[0]),
        w_in_o=bf(w_in_o[0]), conv_w=conv_w[0], conv_b=row(conv_b[0]), ln_g=row(ln_g[0]), ln_b=row(ln_b[0]),
        w_out_o=bf(w_out_o[0]),
        w_router=jnp.pad(w_router[0], ((0, 0), (0, LANES - N_EXPERTS))),
        b_router=jnp.pad(row(b_router[0]), ((0, 0), (0, LANES - N_EXPERTS))),
        we_gate=bf(we_gate[0]), we_up=bf(we_up[0]), we_down=bf(we_down[0]),
        bd64=jnp.kron(jnp.eye(N_HEADS, dtype=F32), jnp.ones((HEAD_DIM, HEAD_DIM), F32)).astype(BF16),
        hm64=_head_mask(N_HEADS, HEAD_DIM), eye8=jnp.eye(N_HEADS, dtype=BF16),
        tril256=_tri(256, "lower_incl"), triu256=_tri(256, "upper_incl"),
        ustrict256=_tri(256, "lower_strict"), ustrict128=_tri(128, "lower_strict"),
    )

    mk, mv = _mem_kv(mem_prompt.reshape(B * N_MEM, D), bf(w_ck), bf(w_cv),
                     jnp.tile(g_ck, (1, X_HEADS)).reshape(-1, 1, X_WIDTH), 2)

    prompt = dict(kind="prompt", B=B, L=L, start_pos=0, seq_per_step=1, chunk=256, conv_chunk=128,
                  pool_prefix=jnp.zeros((B, POOL_BUF, MIX_HALF), F32),
                  conv_prefix=jnp.zeros((B, CONV_BUF, MIX_HALF), F32), x_ns=1, x_tq=512)
    xp, hp, rp, sp = _stack(x_prompt.reshape(B * L, D), mk, mv, W, prompt)

    kv_view = lambda c: jnp.transpose(c[0], (0, 2, 3, 1))
    sample = dict(kind="sample", B=SB, L=SL, start_pos=page_table.shape[1] * PAGE, seq_per_step=16,
                  chunk=SL, conv_chunk=SL, pool_prefix=state_pool[0], conv_prefix=state_conv[0], x_ns=8, x_tq=SL,
                  page_table=page_table,
                  fox_k=kv_view(cache_fox_k), fox_v=kv_view(cache_fox_v),
                  fox_lf=jnp.transpose(cache_fox_logf[0], (0, 2, 1)),
                  sb_k=kv_view(cache_sb_k), sb_v=kv_view(cache_sb_v))
    xs, hs, rs, ss = _stack(x_sample.reshape(SB * SL, D), cache_mem_k, cache_mem_v, W, sample)

    n_tok = B * L + SB * SL
    src, pos, tile_e, n_valid = _moe_plan(jnp.concatenate([rp, rs], axis=0), MOE_TM)
    ye = _moe(jnp.concatenate([hp, hs], axis=0), src, tile_e, n_valid, W["we_gate"], W["we_up"], W["we_down"], MOE_TM)

    def pair_rows(row0, n):
        both = [pos[s * n_tok + row0:s * n_tok + row0 + n].reshape(n // COMBINE_TM, COMBINE_TM) for s in range(2)]
        return jnp.concatenate(both, axis=1).reshape(n // COMBINE_TM, 1, 2 * COMBINE_TM)

    yp = _combine(xp, rp, pair_rows(0, B * L), ye, COMBINE_TM)
    ys = _combine(xs, rs, pair_rows(B * L, SB * SL), ye, COMBINE_TM)

    def outs(new, b, l):
        kv = lambda a: a.reshape(1, b, l, N_HEADS, HEAD_DIM)
        return (kv(new["fox_k"]), kv(new["fox_v"]), new["fox_logf"].reshape(1, b, l, N_HEADS),
                kv(new["sb_k"]), kv(new["sb_v"]), new["pool"][None], new["conv"][None])

    op = outs(sp, B, L)
    os_ = outs(ss, SB, SL)
    return (yp.reshape(B, L, D), ys.reshape(SB, SL, D), *op, mk, mv, *os_)
```

```python
import functools

import jax
import jax.numpy as jnp
from jax import lax
from jax.experimental import pallas as pl
from jax.experimental.pallas import tpu as pltpu

F32 = jnp.float32
BF16 = jnp.bfloat16

D_MODEL = 1024
MIX_HALF = 512
HEAD_DIM = 64
N_HEADS = 8
POOL_WINDOWS = (2, 4, 8, 16)
POOL_GROUP = 128
POOL_BUF = 15
CONV_WIDTH = 31
CONV_BUF = 30
N_MEM = 256
X_HEADS = 4
X_HEAD_DIM = 128
X_WIDTH = 512
N_EXPERTS = 8
PAGE = 128
EPS = 1e-6
ATT_SCALE = HEAD_DIM ** -0.5
NEG_BIG = -1e30

LANES = 128
SUBLANES = 8
MIB = 1024 * 1024


def _params(sem, vmem_mib):
    return pltpu.CompilerParams(dimension_semantics=sem, vmem_limit_bytes=vmem_mib * MIB)


def _full(shape):
    return pl.BlockSpec(shape, lambda *_: (0,) * len(shape))


def _rms(x, g):
    return x * lax.rsqrt(jnp.mean(x * x, axis=-1, keepdims=True) + EPS) * g


def _split2(x):
    hi = x.astype(BF16)
    lo = (x - hi.astype(F32)).astype(BF16)
    return hi, lo


def _split3(x):
    hi = x.astype(BF16)
    r = x - hi.astype(F32)
    mid = r.astype(BF16)
    lo = (r - mid.astype(F32)).astype(BF16)
    return hi, mid, lo


def _dot(a, b):
    return jnp.dot(a, b, preferred_element_type=F32)


def _dot_nt(a, b):
    return lax.dot_general(a, b, (((1,), (1,)), ((), ())), preferred_element_type=F32)


def _softplus(z):
    return jnp.maximum(z, 0.0) + jnp.log1p(jnp.exp(-jnp.abs(z)))


def _head_rms(a, g, bd, head_dim):
    hi, lo = _split2(a * a)
    ssq = _dot(hi, bd) + _dot(lo, bd)
    return a * lax.rsqrt(ssq * (1.0 / head_dim) + EPS) * g


def _in_even_kernel(x_ref, g_ref, w_ref, wf_ref, wft_ref, bf_ref, bft_ref, gq_ref, gk_ref, bd_ref,
                    p_ref, q_ref, k32_ref, kb_ref, v32_ref, vb_ref, lf_ref, lft_ref):
    h = _rms(x_ref[...], g_ref[...]).astype(BF16)
    bd = bd_ref[...]
    p_ref[...] = _dot(h, w_ref[:, 0:MIX_HALF])
    q = _head_rms(_dot(h, w_ref[:, MIX_HALF:2 * MIX_HALF]), gq_ref[...], bd, HEAD_DIM)
    q_ref[...] = q.astype(q_ref.dtype)
    k = _head_rms(_dot(h, w_ref[:, 2 * MIX_HALF:3 * MIX_HALF]), gk_ref[...], bd, HEAD_DIM)
    k32_ref[...] = k
    kb_ref[...] = k.astype(BF16)
    v = _dot(h, w_ref[:, 3 * MIX_HALF:4 * MIX_HALF])
    v32_ref[...] = v
    vb_ref[...] = v.astype(BF16)
    fl = _dot(h, wf_ref[...]) + bf_ref[...]
    lf_ref[...] = (-_softplus(-fl))[:, 0:N_HEADS]
    flt = _dot_nt(wft_ref[...], h) + bft_ref[...]
    lft_ref[...] = -_softplus(-flt)


def _in_even(x, g, w, wf, wft, bf, bft, gq, gk, bd, tm, q_dtype):
    T = x.shape[0]
    row = lambda n: pl.BlockSpec((tm, n), lambda i: (i, 0))
    sds = jax.ShapeDtypeStruct
    return pl.pallas_call(
        _in_even_kernel,
        grid=(T // tm,),
        in_specs=[row(D_MODEL), _full((1, D_MODEL)), _full((D_MODEL, 4 * MIX_HALF)), _full((D_MODEL, LANES)),
                  _full((N_HEADS, D_MODEL)), _full((1, LANES)), _full((N_HEADS, 1)), _full((1, MIX_HALF)),
                  _full((1, MIX_HALF)), _full((MIX_HALF, MIX_HALF))],
        out_specs=[row(MIX_HALF), row(MIX_HALF), row(MIX_HALF), row(MIX_HALF), row(MIX_HALF), row(MIX_HALF),
                   row(N_HEADS), pl.BlockSpec((N_HEADS, tm), lambda i: (0, i))],
        out_shape=[sds((T, MIX_HALF), F32), sds((T, MIX_HALF), q_dtype), sds((T, MIX_HALF), F32),
                   sds((T, MIX_HALF), BF16), sds((T, MIX_HALF), F32), sds((T, MIX_HALF), BF16),
                   sds((T, N_HEADS), F32), sds((N_HEADS, T), F32)],
        compiler_params=_params(("parallel",), 48),
        name="in_even",
    )(x, g, w, wf, wft, bf, bft, gq, gk, bd)


def _in_odd_kernel(x_ref, g_ref, w_ref, u_ref, q_ref, k32_ref, kb_ref, v32_ref, vb_ref):
    h = _rms(x_ref[...], g_ref[...]).astype(BF16)
    a = _dot(h, w_ref[:, 0:MIX_HALF])
    gate = _dot(h, w_ref[:, MIX_HALF:2 * MIX_HALF])
    u_ref[...] = a * (1.0 / (1.0 + jnp.exp(-gate)))
    q_ref[...] = _dot(h, w_ref[:, 2 * MIX_HALF:3 * MIX_HALF]).astype(q_ref.dtype)
    k = _dot(h, w_ref[:, 3 * MIX_HALF:4 * MIX_HALF])
    k32_ref[...] = k
    kb_ref[...] = k.astype(BF16)
    v = _dot(h, w_ref[:, 4 * MIX_HALF:5 * MIX_HALF])
    v32_ref[...] = v
    vb_ref[...] = v.astype(BF16)


def _in_odd(x, g, w, tm, q_dtype):
    T = x.shape[0]
    row = lambda n: pl.BlockSpec((tm, n), lambda i: (i, 0))
    sds = jax.ShapeDtypeStruct
    return pl.pallas_call(
        _in_odd_kernel,
        grid=(T // tm,),
        in_specs=[row(D_MODEL), _full((1, D_MODEL)), _full((D_MODEL, 5 * MIX_HALF))],
        out_specs=[row(MIX_HALF)] * 6,
        out_shape=[sds((T, MIX_HALF), F32), sds((T, MIX_HALF), q_dtype), sds((T, MIX_HALF), F32),
                   sds((T, MIX_HALF), BF16), sds((T, MIX_HALF), F32), sds((T, MIX_HALF), BF16)],
        compiler_params=_params(("parallel",), 48),
        name="in_odd",
    )(x, g, w)


POOL_PAD = 16
CONV_PAD = 32


def _pool_kernel(pre_ref, p_ref, w_ref, sc_ref, y_ref, st_ref, buf, *, ns, L, ch, start_pos):
    for s in range(ns):
        buf[0:POOL_PAD - POOL_BUF, :] = jnp.zeros((POOL_PAD - POOL_BUF, MIX_HALF), F32)
        buf[POOL_PAD - POOL_BUF:POOL_PAD, :] = pre_ref[s]
        buf[POOL_PAD:POOL_PAD + L, :] = p_ref[s]
        st_ref[s] = buf[POOL_PAD + L - POOL_BUF:POOL_PAD + L, :]
        for c0 in range(0, L, ch):
            pos = start_pos + c0 + lax.broadcasted_iota(jnp.int32, (ch, POOL_GROUP), 0)
            outs = []
            for gi, w in enumerate(POOL_WINDOWS):
                ls = slice(gi * POOL_GROUP, (gi + 1) * POOL_GROUP)
                cur = buf[POOL_PAD + c0:POOL_PAD + c0 + ch, ls]
                acc = cur
                for i in range(1, w):
                    acc = acc + buf[POOL_PAD + c0 - i:POOL_PAD + c0 - i + ch, ls]
                cnt = jnp.minimum(pos + 1, w).astype(F32)
                d = acc / cnt - cur
                outs.append(_dot(d.astype(BF16), w_ref[gi]) * sc_ref[:, ls])
            y_ref[s, c0:c0 + ch, :] = jnp.concatenate(outs, axis=-1)


def _pool(prefix, p, w_grp, scale, ns, ch, start_pos):
    B, L, _ = p.shape
    kern = functools.partial(_pool_kernel, ns=ns, L=L, ch=ch, start_pos=start_pos)
    return pl.pallas_call(
        kern,
        grid=(B // ns,),
        in_specs=[pl.BlockSpec((ns, POOL_BUF, MIX_HALF), lambda i: (i, 0, 0)),
                  pl.BlockSpec((ns, L, MIX_HALF), lambda i: (i, 0, 0)),
                  _full((len(POOL_WINDOWS), POOL_GROUP, POOL_GROUP)), _full((1, MIX_HALF))],
        out_specs=[pl.BlockSpec((ns, L, MIX_HALF), lambda i: (i, 0, 0)),
                   pl.BlockSpec((ns, POOL_BUF, MIX_HALF), lambda i: (i, 0, 0))],
        out_shape=[jax.ShapeDtypeStruct((B, L, MIX_HALF), F32),
                   jax.ShapeDtypeStruct((B, POOL_BUF, MIX_HALF), F32)],
        scratch_shapes=[pltpu.VMEM((POOL_PAD + L, MIX_HALF), F32)],
        compiler_params=_params(("parallel",), 40),
        name="pool_mixer",
    )(prefix, p, w_grp, scale)


def _conv_kernel(pre_ref, u_ref, w_ref, b_ref, lg_ref, lb_ref, y_ref, st_ref, buf, *, ns, L, ch):
    for s in range(ns):
        buf[0:CONV_PAD - CONV_BUF, :] = jnp.zeros((CONV_PAD - CONV_BUF, MIX_HALF), F32)
        buf[CONV_PAD - CONV_BUF:CONV_PAD, :] = pre_ref[s]
        buf[CONV_PAD:CONV_PAD + L, :] = u_ref[s]
        st_ref[s] = buf[CONV_PAD + L - CONV_BUF:CONV_PAD + L, :]
        base = CONV_PAD - CONV_BUF
        for c0 in range(0, L, ch):
            acc = buf[base + c0:base + c0 + ch, :] * w_ref[0:1, :]
            for j in range(1, CONV_WIDTH):
                acc = acc + buf[base + c0 + j:base + c0 + j + ch, :] * w_ref[j:j + 1, :]
            y = acc + b_ref[...]
            mu = jnp.mean(y, axis=-1, keepdims=True)
            yc = y - mu
            var = jnp.mean(yc * yc, axis=-1, keepdims=True)
            yn = yc * lax.rsqrt(var + EPS) * lg_ref[...] + lb_ref[...]
            y_ref[s, c0:c0 + ch, :] = yn * (1.0 / (1.0 + jnp.exp(-yn)))


def _conv(prefix, u, conv_w, conv_b, ln_g, ln_b, ns, ch):
    B, L, _ = u.shape
    kern = functools.partial(_conv_kernel, ns=ns, L=L, ch=ch)
    return pl.pallas_call(
        kern,
        grid=(B // ns,),
        in_specs=[pl.BlockSpec((ns, CONV_BUF, MIX_HALF), lambda i: (i, 0, 0)),
                  pl.BlockSpec((ns, L, MIX_HALF), lambda i: (i, 0, 0)),
                  _full((CONV_WIDTH, MIX_HALF)), _full((1, MIX_HALF)), _full((1, MIX_HALF)), _full((1, MIX_HALF))],
        out_specs=[pl.BlockSpec((ns, L, MIX_HALF), lambda i: (i, 0, 0)),
                   pl.BlockSpec((ns, CONV_BUF, MIX_HALF), lambda i: (i, 0, 0))],
        out_shape=[jax.ShapeDtypeStruct((B, L, MIX_HALF), F32),
                   jax.ShapeDtypeStruct((B, CONV_BUF, MIX_HALF), F32)],
        scratch_shapes=[pltpu.VMEM((CONV_PAD + L, MIX_HALF), F32)],
        compiler_params=_params(("parallel",), 40),
        name="conv_module",
    )(prefix, u, conv_w, conv_b, ln_g, ln_b)


def _cumsum_prompt_kernel(lf_ref, lft_ref, tril_ref, triu_ref, f_ref, ft_ref, *, L, blk):
    c_col = jnp.zeros((1, N_HEADS), F32)
    c_row = jnp.zeros((N_HEADS, 1), F32)
    for b0 in range(0, L, blk):
        x = lf_ref[b0:b0 + blk, :]
        xt = lft_ref[:, b0:b0 + blk]
        f = c_col
        for part in _split3(x):
            f = f + _dot(tril_ref[...], part)
        ft = c_row
        for part in _split3(xt):
            ft = ft + _dot(part, triu_ref[...])
        f_ref[b0:b0 + blk, :] = f
        ft_ref[:, b0:b0 + blk] = ft
        c_col = f[blk - 1:blk, :]
        c_row = ft[:, blk - 1:blk]


def _cumsum_prompt(lf, lft, tril, triu, B, L, blk):
    kern = functools.partial(_cumsum_prompt_kernel, L=L, blk=blk)
    return pl.pallas_call(
        kern,
        grid=(B,),
        in_specs=[pl.BlockSpec((L, N_HEADS), lambda b: (b, 0)), pl.BlockSpec((N_HEADS, L), lambda b: (0, b)),
                  _full((blk, blk)), _full((blk, blk))],
        out_specs=[pl.BlockSpec((L, N_HEADS), lambda b: (b, 0)), pl.BlockSpec((N_HEADS, L), lambda b: (0, b))],
        out_shape=[jax.ShapeDtypeStruct((B * L, N_HEADS), F32), jax.ShapeDtypeStruct((N_HEADS, B * L), F32)],
        compiler_params=_params(("parallel",), 32),
        name="fox_cumsum_prompt",
    )(lf, lft, tril, triu)


def _head_pair(q):
    lane = lax.broadcasted_iota(jnp.int32, q.shape, 1)
    zero = jnp.zeros_like(q)
    scale = jnp.asarray(ATT_SCALE, q.dtype)
    return [jnp.where(lane < HEAD_DIM, q, zero) * scale, jnp.where(lane >= HEAD_DIM, q, zero) * scale]


def _stack_heads(q):
    return jnp.concatenate(_head_pair(q), axis=0)


SB_DEAD = -105.0


def _fox_prompt_kernel(q_ref, k_ref, v_ref, f_ref, ft_ref, o_ref, *, blk):
    hp = pl.program_id(1)
    qi = pl.program_id(2)
    qs = _stack_heads(q_ref[...])
    f = f_ref[...]
    lane8 = lax.broadcasted_iota(jnp.int32, f.shape, 1)
    fq = [jnp.sum(jnp.where(lane8 == 2 * hp + i, f, 0.0), axis=1, keepdims=True) for i in range(2)]
    row = lax.broadcasted_iota(jnp.int32, (2 * blk, blk), 0) & (blk - 1)
    col = lax.broadcasted_iota(jnp.int32, (2 * blk, blk), 1)

    def block(ki, carry, diagonal):
        m, l, acc = carry
        ks = pl.multiple_of(ki * blk, blk)
        s = _dot_nt(qs, k_ref[pl.ds(ks, blk), :])
        s = jnp.concatenate([s[i * blk:(i + 1) * blk] + (fq[i] - ft_ref[0, i:i + 1, pl.ds(ks, blk)])
                             for i in range(2)], axis=0)
        if diagonal:
            s = jnp.where(col <= row, s, -jnp.inf)
        m_new = jnp.maximum(m, jnp.max(s, axis=1, keepdims=True))
        alpha = jnp.exp(m - m_new)
        p = jnp.exp(s - m_new)
        l = alpha * l + jnp.sum(p, axis=1, keepdims=True)
        acc = alpha * acc + _dot(p.astype(BF16), v_ref[pl.ds(ks, blk), :])
        return m_new, l, acc

    init = (jnp.full((2 * blk, 1), NEG_BIG, F32), jnp.zeros((2 * blk, 1), F32), jnp.zeros((2 * blk, LANES), F32))
    carry = lax.fori_loop(0, qi, lambda ki, c: block(ki, c, False), init)
    _, l, acc = block(qi, carry, True)
    o = acc / l
    lane = lax.broadcasted_iota(jnp.int32, (blk, LANES), 1)
    o_ref[...] = jnp.where(lane < HEAD_DIM, o[0:blk], o[blk:2 * blk])


def _fox_prompt(q, k, v, f, ft, B, L, blk):
    nq = L // blk
    kern = functools.partial(_fox_prompt_kernel, blk=blk)
    return pl.pallas_call(
        kern,
        grid=(B, N_HEADS // 2, nq),
        in_specs=[pl.BlockSpec((blk, LANES), lambda b, h, i: (b * nq + i, h)),
                  pl.BlockSpec((L, LANES), lambda b, h, i: (b, h)),
                  pl.BlockSpec((L, LANES), lambda b, h, i: (b, h)),
                  pl.BlockSpec((blk, N_HEADS), lambda b, h, i: (b * nq + i, 0)),
                  pl.BlockSpec((1, 2, L), lambda b, h, i: (h, 0, b))],
        out_specs=pl.BlockSpec((blk, LANES), lambda b, h, i: (b * nq + i, h)),
        out_shape=jax.ShapeDtypeStruct((B * L, MIX_HALF), F32),
        compiler_params=_params(("parallel", "parallel", "arbitrary"), 32),
        name="fox_attention_prompt",
    )(q, k, v, f, ft)


def _sb_weights(z, c, u, mask):
    sp = _softplus(z)
    ln = -sp
    if mask is not None:
        ln = jnp.where(mask, ln, 0.0)
    hi, lo = _split2(ln)
    e = _dot(hi, u) + _dot(lo, u)
    a = jnp.exp((z - sp) + e + c)
    if mask is not None:
        a = jnp.where(mask, a, 0.0)
    return a, c + jnp.sum(ln, axis=1, keepdims=True)


def _sb_block(z, c, v, u, mask):
    a, c = _sb_weights(z, c, u, mask)
    return _dot(a.astype(BF16), v), c


def _sb_prompt_kernel(q_ref, k_ref, v_ref, u_ref, o_ref, *, blk):
    qi = pl.program_id(2)
    qs = _stack_heads(q_ref[...])
    row = lax.broadcasted_iota(jnp.int32, (2 * blk, blk), 0) & (blk - 1)
    col = lax.broadcasted_iota(jnp.int32, (2 * blk, blk), 1)
    u = u_ref[...]

    def block(ki, c, acc, diagonal):
        ks = pl.multiple_of(ki * blk, blk)
        z = _dot_nt(qs, k_ref[pl.ds(ks, blk), :])
        pv, c = _sb_block(z, c, v_ref[pl.ds(ks, blk), :], u, (col < row) if diagonal else None)
        return c, acc + pv

    c, acc = block(qi, jnp.zeros((2 * blk, 1), F32), jnp.zeros((2 * blk, LANES), F32), True)

    def cond(state):
        j, alive, _, _ = state
        return (j < qi) & (alive > 0)

    def body(state):
        j, _, c, acc = state
        c, acc = block(qi - 1 - j, c, acc, False)
        return j + 1, (jnp.max(c) > SB_DEAD).astype(jnp.int32), c, acc

    _, _, _, acc = lax.while_loop(cond, body, (jnp.int32(0), jnp.int32(1), c, acc))
    lane = lax.broadcasted_iota(jnp.int32, (blk, LANES), 1)
    o_ref[...] = jnp.where(lane < HEAD_DIM, acc[0:blk], acc[blk:2 * blk])


def _sb_prompt(q, k, v, u, B, L, blk):
    nq = L // blk
    kern = functools.partial(_sb_prompt_kernel, blk=blk)
    return pl.pallas_call(
        kern,
        grid=(B, N_HEADS // 2, nq),
        in_specs=[pl.BlockSpec((blk, LANES), lambda b, h, i: (b * nq + i, h)),
                  pl.BlockSpec((L, LANES), lambda b, h, i: (b, h)),
                  pl.BlockSpec((L, LANES), lambda b, h, i: (b, h)),
                  _full((blk, blk))],
        out_specs=pl.BlockSpec((blk, LANES), lambda b, h, i: (b * nq + i, h)),
        out_shape=jax.ShapeDtypeStruct((B * L, MIX_HALF), F32),
        compiler_params=_params(("parallel", "parallel", "arbitrary"), 32),
        name="sb_attention_prompt",
    )(q, k, v, u)


N_PAGES = 16
DEC_SEQ = 8
BD_ROWS = DEC_SEQ * N_HEADS
SB_AHEAD = 2
SB_SLOTS = SB_AHEAD + 1


def _block_diag_q(q, hm):
    rows = [jnp.broadcast_to(q[t:t + 1, :], (N_HEADS, MIX_HALF)) * hm for t in range(DEC_SEQ)]
    return (jnp.concatenate(rows, axis=0) * ATT_SCALE).astype(BF16)


def _pad_new(x):
    return jnp.concatenate([x, jnp.zeros((PAGE - DEC_SEQ, x.shape[1]), x.dtype)], axis=0)


def _merge_heads(o, hm):
    return jnp.concatenate(
        [jnp.sum(o[t * N_HEADS:(t + 1) * N_HEADS, :] * hm, axis=0, keepdims=True) for t in range(DEC_SEQ)], axis=0)


def _new_key_mask(strict):
    row = lax.broadcasted_iota(jnp.int32, (BD_ROWS, PAGE), 0) // N_HEADS
    col = lax.broadcasted_iota(jnp.int32, (BD_ROWS, PAGE), 1)
    return (col < row) if strict else (col <= row)


def _page_scores(qbd, k_pages, kn_ref, j):
    if j < N_PAGES:
        return _dot(qbd, k_pages[j][0].reshape(MIX_HALF, PAGE).astype(BF16))
    return _dot_nt(qbd, _pad_new(kn_ref[...]).astype(BF16))


def _page_values(w, v_pages, vn_ref, j):
    if j < N_PAGES:
        return _dot_nt(w, v_pages[j][0].reshape(MIX_HALF, PAGE).astype(BF16))
    return _dot(w, _pad_new(vn_ref[...]).astype(BF16))


def _cumsum_lanes(x):
    n = x.shape[1]
    lane = lax.broadcasted_iota(jnp.int32, x.shape, 1)
    k = 1
    while k < n:
        x = x + jnp.where(lane >= k, pltpu.roll(x, k, 1), 0.0)
        k *= 2
    return x


def _fox_decode_kernel(pt_ref, *refs):
    k_pages = refs[0:N_PAGES]
    v_pages = refs[N_PAGES:2 * N_PAGES]
    lf_pages = refs[2 * N_PAGES:3 * N_PAGES]
    q_ref, kn_ref, vn_ref, lftn_ref, hm_ref, o_ref = refs[3 * N_PAGES:]
    del pt_ref
    hm = hm_ref[...]
    qbd = _block_diag_q(q_ref[...], hm)
    parts = [lf_pages[j][0] for j in range(N_PAGES)]
    parts.append(jnp.concatenate([lftn_ref[0], jnp.zeros((N_HEADS, PAGE - DEC_SEQ), F32)], axis=1))
    fk = _cumsum_lanes(jnp.concatenate(parts, axis=1))
    fnew = fk[:, N_PAGES * PAGE:]
    lane = lax.broadcasted_iota(jnp.int32, (N_HEADS, PAGE), 1)
    fq = jnp.concatenate([jnp.sum(jnp.where(lane == t, fnew, 0.0), axis=1, keepdims=True)
                          for t in range(DEC_SEQ)], axis=0)
    scores = []
    for j in range(N_PAGES + 1):
        bias = jnp.concatenate([fk[:, j * PAGE:(j + 1) * PAGE]] * DEC_SEQ, axis=0)
        s = _page_scores(qbd, k_pages, kn_ref, j) + (fq - bias)
        if j == N_PAGES:
            s = jnp.where(_new_key_mask(False), s, -jnp.inf)
        scores.append(s)
    m = scores[0].max(axis=1, keepdims=True)
    for s in scores[1:]:
        m = jnp.maximum(m, s.max(axis=1, keepdims=True))
    l = jnp.zeros((BD_ROWS, 1), F32)
    acc = jnp.zeros((BD_ROWS, MIX_HALF), F32)
    for j in range(N_PAGES + 1):
        p = jnp.exp(scores[j] - m)
        l = l + jnp.sum(p, axis=1, keepdims=True)
        acc = acc + _page_values(p.astype(BF16), v_pages, vn_ref, j)
    o_ref[...] = _merge_heads(acc / l, hm)


def _sb_decode_kernel(pt_ref, q_ref, kn_ref, vn_ref, hm_ref, u_ref, ck_hbm, cv_hbm, o_ref, kbuf, vbuf, ksem, vsem):
    b = pl.program_id(0)
    nb = pl.num_programs(0)
    page_slot = lambda j: (N_PAGES - 1 - j) % SB_SLOTS

    def page_copies(seq, j, slot):
        page = pt_ref[seq, j]
        return (pltpu.make_async_copy(ck_hbm.at[page], kbuf.at[slot], ksem.at[slot]),
                pltpu.make_async_copy(cv_hbm.at[page], vbuf.at[slot], vsem.at[slot]))

    def start(seq, j, slot):
        for cp in page_copies(seq, j, slot):
            cp.start()

    def wait(slot):
        for cp in page_copies(b, 0, slot):
            cp.wait()

    def start_newest(seq):
        for d in range(SB_AHEAD):
            start(seq, N_PAGES - 1 - d, d)

    @pl.when(b == 0)
    def _():
        start_newest(0)

    hm = hm_ref[...]
    qbd = _block_diag_q(q_ref[...], hm)
    u = u_ref[...]
    a, c = _sb_weights(_dot_nt(qbd, _pad_new(kn_ref[...]).astype(BF16)), jnp.zeros((BD_ROWS, 1), F32), u,
                       _new_key_mask(True))
    acc = _dot(a.astype(BF16), _pad_new(vn_ref[...]).astype(BF16))
    is_alive = lambda c: (jnp.max(c) > SB_DEAD).astype(jnp.int32)

    def cond(state):
        j, alive, _, _ = state
        return (j >= 0) & (alive > 0)

    def body(state):
        j, _, c, acc = state
        slot = page_slot(j)
        wait(slot)

        @pl.when(j >= SB_AHEAD)
        def _():
            start(b, j - SB_AHEAD, page_slot(j - SB_AHEAD))

        a, c = _sb_weights(_dot(qbd, kbuf[slot].reshape(MIX_HALF, PAGE).astype(BF16)), c, u, None)
        acc = acc + _dot_nt(a.astype(BF16), vbuf[slot].reshape(MIX_HALF, PAGE).astype(BF16))
        return j - 1, is_alive(c), c, acc

    j, _, _, acc = lax.while_loop(cond, body, (jnp.int32(N_PAGES - 1), is_alive(c), c, acc))

    for d in range(SB_AHEAD):
        @pl.when(j - d >= 0)
        def _():
            wait(page_slot(j - d))

    @pl.when(b + 1 < nb)
    def _():
        start_newest(b + 1)

    o_ref[...] = _merge_heads(acc, hm)


def _page_specs(shape_tail, n):
    specs = []
    for j in range(n):
        specs.append(pl.BlockSpec((1,) + shape_tail, functools.partial(
            lambda b, pt, j: (pt[b, j],) + (0,) * len(shape_tail), j=j)))
    return specs


def _fox_decode(page_table, ck, cv, clf, q, kn, vn, lftn, hm):
    nb = page_table.shape[0]
    seq = lambda n: pl.BlockSpec((DEC_SEQ, n), lambda b, pt: (b, 0))
    const = lambda shape: pl.BlockSpec(shape, lambda b, pt: (0,) * len(shape))
    kv_page = (N_HEADS, HEAD_DIM, PAGE)
    in_specs = (_page_specs(kv_page, N_PAGES) + _page_specs(kv_page, N_PAGES) + _page_specs((N_HEADS, PAGE), N_PAGES)
                + [seq(MIX_HALF), seq(MIX_HALF), seq(MIX_HALF),
                   pl.BlockSpec((1, N_HEADS, DEC_SEQ), lambda b, pt: (b, 0, 0)), const((N_HEADS, MIX_HALF))])
    return pl.pallas_call(
        _fox_decode_kernel,
        grid_spec=pltpu.PrefetchScalarGridSpec(
            num_scalar_prefetch=1, grid=(nb,), in_specs=in_specs, out_specs=seq(MIX_HALF)),
        out_shape=jax.ShapeDtypeStruct((nb * DEC_SEQ, MIX_HALF), F32),
        compiler_params=_params(("arbitrary",), 48),
        name="fox_attention_decode",
    )(page_table, *([ck] * N_PAGES), *([cv] * N_PAGES), *([clf] * N_PAGES), q, kn, vn, lftn, hm)


def _sb_decode(page_table, ck, cv, q, kn, vn, hm, u):
    nb = page_table.shape[0]
    seq = lambda n: pl.BlockSpec((DEC_SEQ, n), lambda b, pt: (b, 0))
    const = lambda shape: pl.BlockSpec(shape, lambda b, pt: (0,) * len(shape))
    kv_page = (N_HEADS, HEAD_DIM, PAGE)
    in_specs = [seq(MIX_HALF), seq(MIX_HALF), seq(MIX_HALF), const((N_HEADS, MIX_HALF)), const((PAGE, PAGE)),
                pl.BlockSpec(memory_space=pl.ANY), pl.BlockSpec(memory_space=pl.ANY)]
    return pl.pallas_call(
        _sb_decode_kernel,
        grid_spec=pltpu.PrefetchScalarGridSpec(
            num_scalar_prefetch=1, grid=(nb,), in_specs=in_specs, out_specs=seq(MIX_HALF),
            scratch_shapes=[pltpu.VMEM((SB_SLOTS,) + kv_page, F32), pltpu.VMEM((SB_SLOTS,) + kv_page, F32),
                            pltpu.SemaphoreType.DMA((SB_SLOTS,)), pltpu.SemaphoreType.DMA((SB_SLOTS,))]),
        out_shape=jax.ShapeDtypeStruct((nb * DEC_SEQ, MIX_HALF), F32),
        compiler_params=_params(("arbitrary",), 32),
        name="sb_attention_decode",
    )(page_table, q, kn, vn, hm, u, ck, cv)


def _x_head_rms(a, g):
    outs = []
    for h in range(X_HEADS):
        ah = a[:, h * X_HEAD_DIM:(h + 1) * X_HEAD_DIM]
        outs.append(ah * lax.rsqrt(jnp.mean(ah * ah, axis=-1, keepdims=True) + EPS))
    return jnp.concatenate(outs, axis=-1) * g


def _mix_out_kernel(x_ref, ya_ref, yb_ref, wo_ref, g_ref, wq_ref, gq_ref, x1_ref, qc_ref):
    y = _dot(ya_ref[...].astype(BF16), wo_ref[0:MIX_HALF, :]) + _dot(yb_ref[...].astype(BF16), wo_ref[MIX_HALF:, :])
    x1 = x_ref[...] + y
    x1_ref[...] = x1
    hc = _rms(x1, g_ref[...]).astype(BF16)
    qc_ref[...] = _x_head_rms(_dot(hc, wq_ref[...]), gq_ref[...])


def _mix_out(x, ya, yb, wo, g, wq, gq, tm):
    T = x.shape[0]
    row = lambda n: pl.BlockSpec((tm, n), lambda i: (i, 0))
    return pl.pallas_call(
        _mix_out_kernel,
        grid=(T // tm,),
        in_specs=[row(D_MODEL), row(MIX_HALF), row(MIX_HALF), _full((D_MODEL, D_MODEL)), _full((1, D_MODEL)),
                  _full((D_MODEL, X_WIDTH)), _full((1, X_WIDTH))],
        out_specs=[row(D_MODEL), row(X_WIDTH)],
        out_shape=[jax.ShapeDtypeStruct((T, D_MODEL), F32), jax.ShapeDtypeStruct((T, X_WIDTH), F32)],
        compiler_params=_params(("parallel",), 40),
        name="mixer_out_proj",
    )(x, ya, yb, wo, g, wq, gq)


def _mem_kv_kernel(m_ref, wk_ref, wv_ref, gk_ref, k_ref, v_ref, *, nb):
    m = m_ref[...].astype(BF16)
    k = _x_head_rms(_dot(m, wk_ref[0]), gk_ref[0])
    v = _dot(m, wv_ref[0])
    for b in range(nb):
        for h in range(X_HEADS):
            ls = slice(h * X_HEAD_DIM, (h + 1) * X_HEAD_DIM)
            k_ref[0, b, :, h, :] = k[b * N_MEM:(b + 1) * N_MEM, ls]
            v_ref[0, b, :, h, :] = v[b * N_MEM:(b + 1) * N_MEM, ls]


def _mem_kv(mem, wk, wv, gk, nb):
    B = mem.shape[0] // N_MEM
    depth = wk.shape[0]
    out = pl.BlockSpec((1, nb, N_MEM, X_HEADS, X_HEAD_DIM), lambda l, i: (l, i, 0, 0, 0))
    return pl.pallas_call(
        functools.partial(_mem_kv_kernel, nb=nb),
        grid=(depth, B // nb),
        in_specs=[pl.BlockSpec((nb * N_MEM, D_MODEL), lambda l, i: (i, 0)),
                  pl.BlockSpec((1, D_MODEL, X_WIDTH), lambda l, i: (l, 0, 0)),
                  pl.BlockSpec((1, D_MODEL, X_WIDTH), lambda l, i: (l, 0, 0)),
                  pl.BlockSpec((1, 1, X_WIDTH), lambda l, i: (l, 0, 0))],
        out_specs=[out, out],
        out_shape=[jax.ShapeDtypeStruct((depth, B, N_MEM, X_HEADS, X_HEAD_DIM), F32)] * 2,
        compiler_params=_params(("parallel", "parallel"), 32),
        name="memory_kv",
    )(mem, wk, wv, gk)


def _cross_kernel(x_ref, q_ref, mk_ref, mv_ref, wo_ref, g_ref, x2_ref, hf_ref, *, ns, tq):
    outs = []
    for s in range(ns):
        q = q_ref[s * tq:(s + 1) * tq, :]
        heads = []
        for h in range(X_HEADS):
            ls = slice(h * X_HEAD_DIM, (h + 1) * X_HEAD_DIM)
            qh = (q[:, ls] * (X_HEAD_DIM ** -0.5)).astype(BF16)
            sc = _dot_nt(qh, mk_ref[0, s, :, h, :].astype(BF16))
            p = jnp.exp(sc - jnp.max(sc, axis=1, keepdims=True))
            o = _dot(p.astype(BF16), mv_ref[0, s, :, h, :].astype(BF16))
            heads.append(o / jnp.sum(p, axis=1, keepdims=True))
        outs.append(jnp.concatenate(heads, axis=-1))
    o = outs[0] if ns == 1 else jnp.concatenate(outs, axis=0)
    x2 = x_ref[...] + _dot(o.astype(BF16), wo_ref[...])
    x2_ref[...] = x2
    hf_ref[...] = _rms(x2, g_ref[...])


def _cross(x, q, mk, mv, layer, wo, g, ns, tq):
    T = x.shape[0]
    L = T // mk.shape[1]
    nq = L // tq
    rows = ns * tq
    row = lambda n: pl.BlockSpec((rows, n), lambda i: (i, 0))
    mem = pl.BlockSpec((1, ns, N_MEM, X_HEADS, X_HEAD_DIM), lambda i: (layer, i // nq, 0, 0, 0))
    kern = functools.partial(_cross_kernel, ns=ns, tq=tq)
    return pl.pallas_call(
        kern,
        grid=(T // rows,),
        in_specs=[row(D_MODEL), row(X_WIDTH), mem, mem, _full((X_WIDTH, D_MODEL)), _full((1, D_MODEL))],
        out_specs=[row(D_MODEL), row(D_MODEL)],
        out_shape=[jax.ShapeDtypeStruct((T, D_MODEL), F32)] * 2,
        compiler_params=_params(("parallel",), 48),
        name="cross_attention",
    )(x, q, mk, mv, wo, g)


def _swiglu_kernel(h_ref, x_ref, wg_ref, wu_ref, wd_ref, o_ref, hb, acc):
    f = pl.program_id(1)

    @pl.when(f == 0)
    def _():
        hb[...] = h_ref[...].astype(BF16)
        acc[...] = jnp.zeros_like(acc)

    h = hb[...]
    gte = _dot(h, wg_ref[...])
    act = gte * (1.0 / (1.0 + jnp.exp(-gte))) * _dot(h, wu_ref[...])
    acc[...] += _dot(act.astype(BF16), wd_ref[...])

    @pl.when(f == pl.num_programs(1) - 1)
    def _():
        o_ref[...] = x_ref[...] + acc[...]


def _swiglu(h, x, wg, wu, wd, tm, tf):
    T = h.shape[0]
    dff = wg.shape[1]
    row = pl.BlockSpec((tm, D_MODEL), lambda i, f: (i, 0))
    return pl.pallas_call(
        _swiglu_kernel,
        grid=(T // tm, dff // tf),
        in_specs=[row, row, pl.BlockSpec((D_MODEL, tf), lambda i, f: (0, f)),
                  pl.BlockSpec((D_MODEL, tf), lambda i, f: (0, f)), pl.BlockSpec((tf, D_MODEL), lambda i, f: (f, 0))],
        out_specs=row,
        out_shape=jax.ShapeDtypeStruct((T, D_MODEL), F32),
        scratch_shapes=[pltpu.VMEM((tm, D_MODEL), BF16), pltpu.VMEM((tm, D_MODEL), F32)],
        compiler_params=_params(("parallel", "arbitrary"), 56),
        name="dense_swiglu",
    )(h, x, wg, wu, wd)


def _router_kernel(h_ref, w_ref, b_ref, r_ref):
    h_hi, h_lo = _split2(h_ref[...])
    w_hi, w_lo = _split2(w_ref[...])
    logits = _dot(h_hi, w_hi) + _dot(h_hi, w_lo) + _dot(h_lo, w_hi) + b_ref[...]
    lane = lax.broadcasted_iota(jnp.int32, logits.shape, 1)
    logits = jnp.where(lane < N_EXPERTS, logits, -jnp.inf)
    m1 = jnp.max(logits, axis=1, keepdims=True)
    i1 = jnp.min(jnp.where(logits == m1, lane, LANES), axis=1, keepdims=True)
    rest = jnp.where(lane == i1, -jnp.inf, logits)
    m2 = jnp.max(rest, axis=1, keepdims=True)
    i2 = jnp.min(jnp.where(rest == m2, lane, LANES), axis=1, keepdims=True)
    e2 = jnp.exp(m2 - m1)
    g1 = 1.0 / (1.0 + e2)
    g2 = e2 / (1.0 + e2)
    r_ref[...] = jnp.where(lane == 0, i1.astype(F32), jnp.where(lane == 1, i2.astype(F32),
                           jnp.where(lane == 2, g1, jnp.where(lane == 3, g2, 0.0))))


def _router(h, w, b, tm):
    T = h.shape[0]
    return pl.pallas_call(
        _router_kernel,
        grid=(T // tm,),
        in_specs=[pl.BlockSpec((tm, D_MODEL), lambda i: (i, 0)), _full((D_MODEL, LANES)), _full((1, LANES))],
        out_specs=pl.BlockSpec((tm, LANES), lambda i: (i, 0)),
        out_shape=jax.ShapeDtypeStruct((T, LANES), F32),
        compiler_params=_params(("parallel",), 32),
        name="moe_router",
    )(h, w, b)


MOE_TM = 512
MOE_NF = 2
COMBINE_TM = 512


def _moe_plan(r, tm):
    T = r.shape[0]
    n_pairs = 2 * T
    e = r[:, 0:2].astype(jnp.int32).T.reshape(-1)
    onehot = (e[:, None] == jnp.arange(N_EXPERTS, dtype=jnp.int32)[None, :]).astype(jnp.int32)
    csum = jnp.cumsum(onehot, axis=0)
    rank = jnp.sum(csum * onehot, axis=1) - 1
    psz = (csum[-1] + tm - 1) // tm * tm
    gend = jnp.cumsum(psz)
    pos = (gend - psz)[e] + rank
    n_rows = n_pairs + N_EXPERTS * tm
    n_tiles = n_rows // tm
    src = jnp.zeros((n_rows,), jnp.int32).at[pos].set(jnp.arange(n_pairs, dtype=jnp.int32) % T)
    t0 = jnp.arange(n_tiles, dtype=jnp.int32) * tm
    tile_e = jnp.minimum(jnp.sum((t0[:, None] >= gend[None, :]).astype(jnp.int32), axis=1), N_EXPERTS - 1)
    n_valid = (gend[-1] // tm).astype(jnp.int32).reshape(1)
    return src.reshape(n_tiles, 1, tm), pos.astype(jnp.int32), tile_e, n_valid


def _row_gather(src_hbm, idx_ref, dst, sem, r0, n, unrolled):
    def start(r):
        pltpu.make_async_copy(src_hbm.at[pl.ds(idx_ref[0, 0, r0 + r], 1)], dst.at[pl.ds(r0 + r, 1)], sem).start()

    if unrolled:
        for r in range(n):
            start(r)
    else:
        def body(r, carry):
            start(r)
            return carry
        lax.fori_loop(0, n, body, 0, unroll=8)


def _moe_kernel(te_ref, nv_ref, src_ref, srcn_ref, h_hbm, wg_ref, wu_ref, wd_ref, y_ref, xin, hb, gsem, *, tm):
    del te_ref
    i = pl.program_id(0)
    f = pl.program_id(1)
    nt = pl.num_programs(0)
    nf = pl.num_programs(1)
    slot = i % 2
    is_valid = i < nv_ref[0]
    part = tm // MOE_NF
    r0 = f * part

    def wait_tile(s):
        pltpu.make_async_copy(h_hbm.at[pl.ds(0, tm)], xin.at[s], gsem.at[s]).wait()

    @pl.when((i == 0) & (f == 0))
    def _():
        _row_gather(h_hbm, src_ref, xin.at[0], gsem.at[0], 0, tm, False)

    @pl.when(f == 0)
    def _():
        wait_tile(slot)
        hb[...] = xin[slot].astype(BF16)
        y_ref[...] = jnp.zeros_like(y_ref)

    @pl.when(is_valid)
    def _():
        _row_gather(h_hbm, srcn_ref, xin.at[1 - slot], gsem.at[1 - slot], r0, part, True)
        h = hb[...]
        gte = _dot(h, wg_ref[0])
        act = gte * (1.0 / (1.0 + jnp.exp(-gte))) * _dot(h, wu_ref[0])
        y_ref[...] += _dot(act.astype(BF16), wd_ref[0])

    @pl.when(jnp.logical_not(is_valid))
    def _():
        _row_gather(h_hbm, srcn_ref, xin.at[1 - slot], gsem.at[1 - slot], r0, part, False)

    @pl.when((i == nt - 1) & (f == nf - 1))
    def _():
        wait_tile(1 - slot)


def _moe(h, src, tile_e, n_valid, wg, wu, wd, tm):
    n_tiles = src.shape[0]
    nf = MOE_NF
    tf = wg.shape[2] // nf
    fser = lambda i, f: jnp.where(i % 2 == 0, f, nf - 1 - f)
    smem_tile = lambda imap: pl.BlockSpec((1, 1, tm), imap, memory_space=pltpu.SMEM)
    kern = functools.partial(_moe_kernel, tm=tm)
    return pl.pallas_call(
        kern,
        grid_spec=pltpu.PrefetchScalarGridSpec(
            num_scalar_prefetch=2,
            grid=(n_tiles, nf),
            in_specs=[smem_tile(lambda i, f, te, nv: (i, 0, 0)),
                      smem_tile(lambda i, f, te, nv: (jnp.minimum(i + 1, n_tiles - 1), 0, 0)),
                      pl.BlockSpec(memory_space=pl.ANY),
                      pl.BlockSpec((1, D_MODEL, tf), lambda i, f, te, nv: (te[i], 0, fser(i, f))),
                      pl.BlockSpec((1, D_MODEL, tf), lambda i, f, te, nv: (te[i], 0, fser(i, f))),
                      pl.BlockSpec((1, tf, D_MODEL), lambda i, f, te, nv: (te[i], fser(i, f), 0))],
            out_specs=pl.BlockSpec((tm, D_MODEL), lambda i, f, te, nv: (i, 0)),
            scratch_shapes=[pltpu.VMEM((2, tm, D_MODEL), F32), pltpu.VMEM((tm, D_MODEL), BF16),
                            pltpu.SemaphoreType.DMA((2,))]),
        out_shape=jax.ShapeDtypeStruct((n_tiles * tm, D_MODEL), F32),
        compiler_params=_params(("arbitrary", "arbitrary"), 56),
        name="moe_swiglu",
    )(tile_e, n_valid, src, src, h, wg, wu, wd)


def _combine_kernel(pos_ref, posn_ref, x_ref, r_ref, y_hbm, o_ref, buf, sem, *, tm):
    i = pl.program_id(0)
    nt = pl.num_programs(0)
    slot = i % 2

    @pl.when(i == 0)
    def _():
        _row_gather(y_hbm, pos_ref, buf.at[0], sem.at[0], 0, 2 * tm, False)

    @pl.when(i + 1 < nt)
    def _():
        _row_gather(y_hbm, posn_ref, buf.at[1 - slot], sem.at[1 - slot], 0, 2 * tm, False)

    pltpu.make_async_copy(y_hbm.at[pl.ds(0, 2 * tm)], buf.at[slot], sem.at[slot]).wait()
    r = r_ref[...]
    o_ref[...] = x_ref[...] + (r[:, 2:3] * buf[slot, 0:tm, :] + r[:, 3:4] * buf[slot, tm:2 * tm, :])


def _combine(x, r, pos, y, tm):
    T = x.shape[0]
    nt = T // tm
    smem_tile = lambda imap: pl.BlockSpec((1, 1, 2 * tm), imap, memory_space=pltpu.SMEM)
    return pl.pallas_call(
        functools.partial(_combine_kernel, tm=tm),
        grid=(nt,),
        in_specs=[smem_tile(lambda i: (i, 0, 0)), smem_tile(lambda i: (jnp.minimum(i + 1, nt - 1), 0, 0)),
                  pl.BlockSpec((tm, D_MODEL), lambda i: (i, 0)), pl.BlockSpec((tm, LANES), lambda i: (i, 0)),
                  pl.BlockSpec(memory_space=pl.ANY)],
        out_specs=pl.BlockSpec((tm, D_MODEL), lambda i: (i, 0)),
        out_shape=jax.ShapeDtypeStruct((T, D_MODEL), F32),
        scratch_shapes=[pltpu.VMEM((2, 2 * tm, D_MODEL), F32), pltpu.SemaphoreType.DMA((2,))],
        compiler_params=_params(("arbitrary",), 40),
        name="moe_combine",
    )(pos, pos, x, r, y)


def _tri(n, kind):
    r = lax.broadcasted_iota(jnp.int32, (n, n), 0)
    c = lax.broadcasted_iota(jnp.int32, (n, n), 1)
    m = {"lower_incl": r >= c, "upper_incl": r <= c, "lower_strict": r > c}[kind]
    return m.astype(BF16)


def _head_mask(n_heads, head_dim):
    h = lax.broadcasted_iota(jnp.int32, (n_heads, n_heads * head_dim), 0)
    c = lax.broadcasted_iota(jnp.int32, (n_heads, n_heads * head_dim), 1) // head_dim
    return (h == c).astype(F32)


def _stack(x, mem_k, mem_v, W, group):
    T = x.shape[0]
    tm = 512
    B, L = group["B"], group["L"]
    prompt = group["kind"] == "prompt"
    q_dtype = BF16 if prompt else F32
    new = {}

    p, q, k32, kb, v32, vb, lf, lft = _in_even(x, W["g_mix0"], W["w_in_e"], W["wf"], W["wft"], W["bf"], W["bft"],
                                               W["g_fq"], W["g_fk"], W["bd64"], tm, q_dtype)
    ya, pool_state = _pool(group["pool_prefix"], p.reshape(B, L, MIX_HALF), W["w_pool"], W["pool_scale"],
                           ns=group["seq_per_step"], ch=group["chunk"], start_pos=group["start_pos"])
    if prompt:
        f, ft = _cumsum_prompt(lf, lft, W["tril256"], W["triu256"], B, L, 256)
        yb = _fox_prompt(q, kb, vb, f, ft.reshape(N_HEADS // 2, 2, T), B, L, 256)
    else:
        lftn = lft.reshape(N_HEADS, B, L).transpose(1, 0, 2)
        yb = _fox_decode(group["page_table"], group["fox_k"], group["fox_v"], group["fox_lf"], q, k32, v32, lftn,
                         W["hm64"])
    new.update(fox_k=k32, fox_v=v32, fox_logf=lf, pool=pool_state)
    x, qc = _mix_out(x, ya.reshape(T, MIX_HALF), yb, W["w_out_e"], W["g_cross0"], W["w_cq0"], W["g_cq0"], tm)
    x, hf = _cross(x, qc, mem_k, mem_v, 0, W["w_co0"], W["g_ffn0"], ns=group["x_ns"], tq=group["x_tq"])
    x = _swiglu(hf, x, W["w_gate"], W["w_up"], W["w_down"], tm, 1408)

    u, q, k32, kb, v32, vb = _in_odd(x, W["g_mix1"], W["w_in_o"], tm, q_dtype)
    yc, conv_state = _conv(group["conv_prefix"], u.reshape(B, L, MIX_HALF), W["conv_w"], W["conv_b"], W["ln_g"],
                           W["ln_b"], ns=group["seq_per_step"], ch=group["conv_chunk"])
    if prompt:
        yd = _sb_prompt(q, kb, vb, W["ustrict256"], B, L, 256)
    else:
        yd = _sb_decode(group["page_table"], group["sb_k"], group["sb_v"], q, k32, v32, W["hm64"], W["ustrict128"])
    new.update(sb_k=k32, sb_v=v32, conv=conv_state)
    x, qc = _mix_out(x, yc.reshape(T, MIX_HALF), yd, W["w_out_o"], W["g_cross1"], W["w_cq1"], W["g_cq1"], tm)
    x, hf = _cross(x, qc, mem_k, mem_v, 1, W["w_co1"], W["g_ffn1"], ns=group["x_ns"], tq=group["x_tq"])
    r = _router(hf, W["w_router"], W["b_router"], tm)
    return x, hf, r, new


def kernel(x_prompt, x_sample, cache_fox_k, cache_fox_v, cache_fox_logf, cache_sb_k, cache_sb_v, cache_mem_k, cache_mem_v, state_pool, state_conv, page_table, mem_prompt, norm_mix, norm_cross, norm_ffn, w_cq, w_ck, w_cv, w_co, g_cq, g_ck, w_in_e, b_f, w_pool_grp, pool_scale, g_fq, g_fk, w_out_e, w_gate, w_up, w_down, w_in_o, conv_w, conv_b, ln_g, ln_b, w_out_o, w_router, b_router, we_gate, we_up, we_down):
    B, L, D = x_prompt.shape
    SB, SL, _ = x_sample.shape
    n_pool = cache_fox_k.shape[1]
    bf = lambda a: a.astype(BF16)
    row = lambda a: a.reshape(1, -1)

    wf = jnp.pad(w_in_e[0][:, 4 * MIX_HALF:], ((0, 0), (0, LANES - N_HEADS)))
    W = dict(
        g_mix0=row(norm_mix[0]), g_mix1=row(norm_mix[1]), g_cross0=row(norm_cross[0]), g_cross1=row(norm_cross[1]),
        g_ffn0=row(norm_ffn[0]), g_ffn1=row(norm_ffn[1]),
        w_in_e=bf(w_in_e[0][:, :4 * MIX_HALF]), wf=bf(wf), wft=bf(w_in_e[0][:, 4 * MIX_HALF:].T),
        bf=jnp.pad(row(b_f[0]), ((0, 0), (0, LANES - N_HEADS))), bft=b_f[0].reshape(N_HEADS, 1),
        g_fq=row(jnp.tile(g_fq[0], N_HEADS)), g_fk=row(jnp.tile(g_fk[0], N_HEADS)),
        w_pool=bf(w_pool_grp[0]), pool_scale=row(pool_scale[0]), w_out_e=bf(w_out_e[0]),
        w_cq0=bf(w_cq[0]), w_cq1=bf(w_cq[1]), w_co0=bf(w_co[0]), w_co1=bf(w_co[1]),
        g_cq0=row(jnp.tile(g_cq[0], X_HEADS)), g_cq1=row(jnp.tile(g_cq[1], X_HEADS)),
        w_gate=bf(w_gate[0]), w_up=bf(w_up[0]), w_down=bf(w_down[0]),
        w_in_o=bf(w_in_o[0]), conv_w=conv_w[0], conv_b=row(conv_b[0]), ln_g=row(ln_g[0]), ln_b=row(ln_b[0]),
        w_out_o=bf(w_out_o[0]),
        w_router=jnp.pad(w_router[0], ((0, 0), (0, LANES - N_EXPERTS))),
        b_router=jnp.pad(row(b_router[0]), ((0, 0), (0, LANES - N_EXPERTS))),
        we_gate=bf(we_gate[0]), we_up=bf(we_up[0]), we_down=bf(we_down[0]),
        bd64=jnp.kron(jnp.eye(N_HEADS, dtype=F32), jnp.ones((HEAD_DIM, HEAD_DIM), F32)).astype(BF16),
        hm64=_head_mask(N_HEADS, HEAD_DIM), eye8=jnp.eye(N_HEADS, dtype=BF16),
        tril256=_tri(256, "lower_incl"), triu256=_tri(256, "upper_incl"),
        ustrict256=_tri(256, "lower_strict"), ustrict128=_tri(128, "lower_strict"),
    )

    mk, mv = _mem_kv(mem_prompt.reshape(B * N_MEM, D), bf(w_ck), bf(w_cv),
                     jnp.tile(g_ck, (1, X_HEADS)).reshape(-1, 1, X_WIDTH), 2)

    prompt = dict(kind="prompt", B=B, L=L, start_pos=0, seq_per_step=1, chunk=256, conv_chunk=128,
                  pool_prefix=jnp.zeros((B, POOL_BUF, MIX_HALF), F32),
                  conv_prefix=jnp.zeros((B, CONV_BUF, MIX_HALF), F32), x_ns=1, x_tq=512)
    xp, hp, rp, sp = _stack(x_prompt.reshape(B * L, D), mk, mv, W, prompt)

    kv_view = lambda c: jnp.transpose(c[0], (0, 2, 3, 1))
    sample = dict(kind="sample", B=SB, L=SL, start_pos=page_table.shape[1] * PAGE, seq_per_step=16,
                  chunk=SL, conv_chunk=SL, pool_prefix=state_pool[0], conv_prefix=state_conv[0], x_ns=8, x_tq=SL,
                  page_table=page_table,
                  fox_k=kv_view(cache_fox_k), fox_v=kv_view(cache_fox_v),
                  fox_lf=jnp.transpose(cache_fox_logf[0], (0, 2, 1)),
                  sb_k=kv_view(cache_sb_k), sb_v=kv_view(cache_sb_v))
    xs, hs, rs, ss = _stack(x_sample.reshape(SB * SL, D), cache_mem_k, cache_mem_v, W, sample)

    n_tok = B * L + SB * SL
    src, pos, tile_e, n_valid = _moe_plan(jnp.concatenate([rp, rs], axis=0), MOE_TM)
    ye = _moe(jnp.concatenate([hp, hs], axis=0), src, tile_e, n_valid, W["we_gate"], W["we_up"], W["we_down"], MOE_TM)

    def pair_rows(row0, n):
        both = [pos[s * n_tok + row0:s * n_tok + row0 + n].reshape(n // COMBINE_TM, COMBINE_TM) for s in range(2)]
        return jnp.concatenate(both, axis=1).reshape(n // COMBINE_TM, 1, 2 * COMBINE_TM)

    yp = _combine(xp, rp, pair_rows(0, B * L), ye, COMBINE_TM)
    ys = _combine(xs, rs, pair_rows(B * L, SB * SL), ye, COMBINE_TM)

    def outs(new, b, l):
        kv = lambda a: a.reshape(1, b, l, N_HEADS, HEAD_DIM)
        return (kv(new["fox_k"]), kv(new["fox_v"]), new["fox_logf"].reshape(1, b, l, N_HEADS),
                kv(new["sb_k"]), kv(new["sb_v"]), new["pool"][None], new["conv"][None])

    op = outs(sp, B, L)
    os_ = outs(ss, SB, SL)
    return (yp.reshape(B, L, D), ys.reshape(SB, SL, D), *op, mk, mv, *os_)
```

```python
import functools

import jax
import jax.numpy as jnp
from jax import lax
from jax.experimental import pallas as pl
from jax.experimental.pallas import tpu as pltpu

F32 = jnp.float32
BF16 = jnp.bfloat16

D_MODEL = 1024
MIX_HALF = 512
HEAD_DIM = 64
N_HEADS = 8
POOL_WINDOWS = (2, 4, 8, 16)
POOL_GROUP = 128
POOL_BUF = 15
CONV_WIDTH = 31
CONV_BUF = 30
N_MEM = 256
X_HEADS = 4
X_HEAD_DIM = 128
X_WIDTH = 512
N_EXPERTS = 8
PAGE = 128
EPS = 1e-6
ATT_SCALE = HEAD_DIM ** -0.5
NEG_BIG = -1e30

LANES = 128
SUBLANES = 8
MIB = 1024 * 1024


def _params(sem, vmem_mib):
    return pltpu.CompilerParams(dimension_semantics=sem, vmem_limit_bytes=vmem_mib * MIB)


def _full(shape):
    return pl.BlockSpec(shape, lambda *_: (0,) * len(shape))


def _rms(x, g):
    return x * lax.rsqrt(jnp.mean(x * x, axis=-1, keepdims=True) + EPS) * g


def _split2(x):
    hi = x.astype(BF16)
    lo = (x - hi.astype(F32)).astype(BF16)
    return hi, lo


def _split3(x):
    hi = x.astype(BF16)
    r = x - hi.astype(F32)
    mid = r.astype(BF16)
    lo = (r - mid.astype(F32)).astype(BF16)
    return hi, mid, lo


def _dot(a, b):
    return jnp.dot(a, b, preferred_element_type=F32)


def _dot_nt(a, b):
    return lax.dot_general(a, b, (((1,), (1,)), ((), ())), preferred_element_type=F32)


def _softplus(z):
    return jnp.maximum(z, 0.0) + jnp.log1p(jnp.exp(-jnp.abs(z)))


def _head_rms(a, g, bd, head_dim):
    hi, lo = _split2(a * a)
    ssq = _dot(hi, bd) + _dot(lo, bd)
    return a * lax.rsqrt(ssq * (1.0 / head_dim) + EPS) * g


def _in_even_kernel(x_ref, g_ref, w_ref, wf_ref, wft_ref, bf_ref, bft_ref, gq_ref, gk_ref, bd_ref,
                    p_ref, q_ref, k32_ref, kb_ref, v32_ref, vb_ref, lf_ref, lft_ref):
    h = _rms(x_ref[...], g_ref[...]).astype(BF16)
    bd = bd_ref[...]
    p_ref[...] = _dot(h, w_ref[:, 0:MIX_HALF])
    q = _head_rms(_dot(h, w_ref[:, MIX_HALF:2 * MIX_HALF]), gq_ref[...], bd, HEAD_DIM)
    q_ref[...] = q.astype(q_ref.dtype)
    k = _head_rms(_dot(h, w_ref[:, 2 * MIX_HALF:3 * MIX_HALF]), gk_ref[...], bd, HEAD_DIM)
    k32_ref[...] = k
    kb_ref[...] = k.astype(BF16)
    v = _dot(h, w_ref[:, 3 * MIX_HALF:4 * MIX_HALF])
    v32_ref[...] = v
    vb_ref[...] = v.astype(BF16)
    fl = _dot(h, wf_ref[...]) + bf_ref[...]
    lf_ref[...] = (-_softplus(-fl))[:, 0:N_HEADS]
    flt = _dot_nt(wft_ref[...], h) + bft_ref[...]
    lft_ref[...] = -_softplus(-flt)


def _in_even(x, g, w, wf, wft, bf, bft, gq, gk, bd, tm, q_dtype):
    T = x.shape[0]
    row = lambda n: pl.BlockSpec((tm, n), lambda i: (i, 0))
    sds = jax.ShapeDtypeStruct
    return pl.pallas_call(
        _in_even_kernel,
        grid=(T // tm,),
        in_specs=[row(D_MODEL), _full((1, D_MODEL)), _full((D_MODEL, 4 * MIX_HALF)), _full((D_MODEL, LANES)),
                  _full((N_HEADS, D_MODEL)), _full((1, LANES)), _full((N_HEADS, 1)), _full((1, MIX_HALF)),
                  _full((1, MIX_HALF)), _full((MIX_HALF, MIX_HALF))],
        out_specs=[row(MIX_HALF), row(MIX_HALF), row(MIX_HALF), row(MIX_HALF), row(MIX_HALF), row(MIX_HALF),
                   row(N_HEADS), pl.BlockSpec((N_HEADS, tm), lambda i: (0, i))],
        out_shape=[sds((T, MIX_HALF), F32), sds((T, MIX_HALF), q_dtype), sds((T, MIX_HALF), F32),
                   sds((T, MIX_HALF), BF16), sds((T, MIX_HALF), F32), sds((T, MIX_HALF), BF16),
                   sds((T, N_HEADS), F32), sds((N_HEADS, T), F32)],
        compiler_params=_params(("parallel",), 48),
        name="in_even",
    )(x, g, w, wf, wft, bf, bft, gq, gk, bd)


def _in_odd_kernel(x_ref, g_ref, w_ref, u_ref, q_ref, k32_ref, kb_ref, v32_ref, vb_ref):
    h = _rms(x_ref[...], g_ref[...]).astype(BF16)
    a = _dot(h, w_ref[:, 0:MIX_HALF])
    gate = _dot(h, w_ref[:, MIX_HALF:2 * MIX_HALF])
    u_ref[...] = a * (1.0 / (1.0 + jnp.exp(-gate)))
    q_ref[...] = _dot(h, w_ref[:, 2 * MIX_HALF:3 * MIX_HALF]).astype(q_ref.dtype)
    k = _dot(h, w_ref[:, 3 * MIX_HALF:4 * MIX_HALF])
    k32_ref[...] = k
    kb_ref[...] = k.astype(BF16)
    v = _dot(h, w_ref[:, 4 * MIX_HALF:5 * MIX_HALF])
    v32_ref[...] = v
    vb_ref[...] = v.astype(BF16)


def _in_odd(x, g, w, tm, q_dtype):
    T = x.shape[0]
    row = lambda n: pl.BlockSpec((tm, n), lambda i: (i, 0))
    sds = jax.ShapeDtypeStruct
    return pl.pallas_call(
        _in_odd_kernel,
        grid=(T // tm,),
        in_specs=[row(D_MODEL), _full((1, D_MODEL)), _full((D_MODEL, 5 * MIX_HALF))],
        out_specs=[row(MIX_HALF)] * 6,
        out_shape=[sds((T, MIX_HALF), F32), sds((T, MIX_HALF), q_dtype), sds((T, MIX_HALF), F32),
                   sds((T, MIX_HALF), BF16), sds((T, MIX_HALF), F32), sds((T, MIX_HALF), BF16)],
        compiler_params=_params(("parallel",), 48),
        name="in_odd",
    )(x, g, w)


POOL_PAD = 16
CONV_PAD = 32


def _pool_kernel(pre_ref, p_ref, w_ref, sc_ref, y_ref, st_ref, buf, *, ns, L, ch, start_pos):
    for s in range(ns):
        buf[0:POOL_PAD - POOL_BUF, :] = jnp.zeros((POOL_PAD - POOL_BUF, MIX_HALF), F32)
        buf[POOL_PAD - POOL_BUF:POOL_PAD, :] = pre_ref[s]
        buf[POOL_PAD:POOL_PAD + L, :] = p_ref[s]
        st_ref[s] = buf[POOL_PAD + L - POOL_BUF:POOL_PAD + L, :]
        for c0 in range(0, L, ch):
            pos = start_pos + c0 + lax.broadcasted_iota(jnp.int32, (ch, POOL_GROUP), 0)
            outs = []
            for gi, w in enumerate(POOL_WINDOWS):
                ls = slice(gi * POOL_GROUP, (gi + 1) * POOL_GROUP)
                cur = buf[POOL_PAD + c0:POOL_PAD + c0 + ch, ls]
                acc = cur
                for i in range(1, w):
                    acc = acc + buf[POOL_PAD + c0 - i:POOL_PAD + c0 - i + ch, ls]
                cnt = jnp.minimum(pos + 1, w).astype(F32)
                d = acc / cnt - cur
                outs.append(_dot(d.astype(BF16), w_ref[gi]) * sc_ref[:, ls])
            y_ref[s, c0:c0 + ch, :] = jnp.concatenate(outs, axis=-1)


def _pool(prefix, p, w_grp, scale, ns, ch, start_pos):
    B, L, _ = p.shape
    kern = functools.partial(_pool_kernel, ns=ns, L=L, ch=ch, start_pos=start_pos)
    return pl.pallas_call(
        kern,
        grid=(B // ns,),
        in_specs=[pl.BlockSpec((ns, POOL_BUF, MIX_HALF), lambda i: (i, 0, 0)),
                  pl.BlockSpec((ns, L, MIX_HALF), lambda i: (i, 0, 0)),
                  _full((len(POOL_WINDOWS), POOL_GROUP, POOL_GROUP)), _full((1, MIX_HALF))],
        out_specs=[pl.BlockSpec((ns, L, MIX_HALF), lambda i: (i, 0, 0)),
                   pl.BlockSpec((ns, POOL_BUF, MIX_HALF), lambda i: (i, 0, 0))],
        out_shape=[jax.ShapeDtypeStruct((B, L, MIX_HALF), F32),
                   jax.ShapeDtypeStruct((B, POOL_BUF, MIX_HALF), F32)],
        scratch_shapes=[pltpu.VMEM((POOL_PAD + L, MIX_HALF), F32)],
        compiler_params=_params(("parallel",), 40),
        name="pool_mixer",
    )(prefix, p, w_grp, scale)


def _conv_kernel(pre_ref, u_ref, w_ref, b_ref, lg_ref, lb_ref, y_ref, st_ref, buf, *, ns, L, ch):
    for s in range(ns):
        buf[0:CONV_PAD - CONV_BUF, :] = jnp.zeros((CONV_PAD - CONV_BUF, MIX_HALF), F32)
        buf[CONV_PAD - CONV_BUF:CONV_PAD, :] = pre_ref[s]
        buf[CONV_PAD:CONV_PAD + L, :] = u_ref[s]
        st_ref[s] = buf[CONV_PAD + L - CONV_BUF:CONV_PAD + L, :]
        base = CONV_PAD - CONV_BUF
        for c0 in range(0, L, ch):
            acc = buf[base + c0:base + c0 + ch, :] * w_ref[0:1, :]
            for j in range(1, CONV_WIDTH):
                acc = acc + buf[base + c0 + j:base + c0 + j + ch, :] * w_ref[j:j + 1, :]
            y = acc + b_ref[...]
            mu = jnp.mean(y, axis=-1, keepdims=True)
            yc = y - mu
            var = jnp.mean(yc * yc, axis=-1, keepdims=True)
            yn = yc * lax.rsqrt(var + EPS) * lg_ref[...] + lb_ref[...]
            y_ref[s, c0:c0 + ch, :] = yn * (1.0 / (1.0 + jnp.exp(-yn)))


def _conv(prefix, u, conv_w, conv_b, ln_g, ln_b, ns, ch):
    B, L, _ = u.shape
    kern = functools.partial(_conv_kernel, ns=ns, L=L, ch=ch)
    return pl.pallas_call(
        kern,
        grid=(B // ns,),
        in_specs=[pl.BlockSpec((ns, CONV_BUF, MIX_HALF), lambda i: (i, 0, 0)),
                  pl.BlockSpec((ns, L, MIX_HALF), lambda i: (i, 0, 0)),
                  _full((CONV_WIDTH, MIX_HALF)), _full((1, MIX_HALF)), _full((1, MIX_HALF)), _full((1, MIX_HALF))],
        out_specs=[pl.BlockSpec((ns, L, MIX_HALF), lambda i: (i, 0, 0)),
                   pl.BlockSpec((ns, CONV_BUF, MIX_HALF), lambda i: (i, 0, 0))],
        out_shape=[jax.ShapeDtypeStruct((B, L, MIX_HALF), F32),
                   jax.ShapeDtypeStruct((B, CONV_BUF, MIX_HALF), F32)],
        scratch_shapes=[pltpu.VMEM((CONV_PAD + L, MIX_HALF), F32)],
        compiler_params=_params(("parallel",), 40),
        name="conv_module",
    )(prefix, u, conv_w, conv_b, ln_g, ln_b)


def _cumsum_prompt_kernel(lf_ref, lft_ref, tril_ref, triu_ref, f_ref, ft_ref, *, L, blk):
    c_col = jnp.zeros((1, N_HEADS), F32)
    c_row = jnp.zeros((N_HEADS, 1), F32)
    for b0 in range(0, L, blk):
        x = lf_ref[b0:b0 + blk, :]
        xt = lft_ref[:, b0:b0 + blk]
        f = c_col
        for part in _split3(x):
            f = f + _dot(tril_ref[...], part)
        ft = c_row
        for part in _split3(xt):
            ft = ft + _dot(part, triu_ref[...])
        f_ref[b0:b0 + blk, :] = f
        ft_ref[:, b0:b0 + blk] = ft
        c_col = f[blk - 1:blk, :]
        c_row = ft[:, blk - 1:blk]


def _cumsum_prompt(lf, lft, tril, triu, B, L, blk):
    kern = functools.partial(_cumsum_prompt_kernel, L=L, blk=blk)
    return pl.pallas_call(
        kern,
        grid=(B,),
        in_specs=[pl.BlockSpec((L, N_HEADS), lambda b: (b, 0)), pl.BlockSpec((N_HEADS, L), lambda b: (0, b)),
                  _full((blk, blk)), _full((blk, blk))],
        out_specs=[pl.BlockSpec((L, N_HEADS), lambda b: (b, 0)), pl.BlockSpec((N_HEADS, L), lambda b: (0, b))],
        out_shape=[jax.ShapeDtypeStruct((B * L, N_HEADS), F32), jax.ShapeDtypeStruct((N_HEADS, B * L), F32)],
        compiler_params=_params(("parallel",), 32),
        name="fox_cumsum_prompt",
    )(lf, lft, tril, triu)


def _head_pair(q):
    lane = lax.broadcasted_iota(jnp.int32, q.shape, 1)
    zero = jnp.zeros_like(q)
    scale = jnp.asarray(ATT_SCALE, q.dtype)
    return [jnp.where(lane < HEAD_DIM, q, zero) * scale, jnp.where(lane >= HEAD_DIM, q, zero) * scale]


def _stack_heads(q):
    return jnp.concatenate(_head_pair(q), axis=0)


SB_DEAD = -105.0


def _fox_prompt_kernel(q_ref, k_ref, v_ref, f_ref, ft_ref, o_ref, *, blk):
    hp = pl.program_id(1)
    qi = pl.program_id(2)
    qs = _stack_heads(q_ref[...])
    f = f_ref[...]
    lane8 = lax.broadcasted_iota(jnp.int32, f.shape, 1)
    fq = [jnp.sum(jnp.where(lane8 == 2 * hp + i, f, 0.0), axis=1, keepdims=True) for i in range(2)]
    row = lax.broadcasted_iota(jnp.int32, (2 * blk, blk), 0) & (blk - 1)
    col = lax.broadcasted_iota(jnp.int32, (2 * blk, blk), 1)

    def block(ki, carry, diagonal):
        m, l, acc = carry
        ks = pl.multiple_of(ki * blk, blk)
        s = _dot_nt(qs, k_ref[pl.ds(ks, blk), :])
        s = jnp.concatenate([s[i * blk:(i + 1) * blk] + (fq[i] - ft_ref[0, i:i + 1, pl.ds(ks, blk)])
                             for i in range(2)], axis=0)
        if diagonal:
            s = jnp.where(col <= row, s, -jnp.inf)
        m_new = jnp.maximum(m, jnp.max(s, axis=1, keepdims=True))
        alpha = jnp.exp(m - m_new)
        p = jnp.exp(s - m_new)
        l = alpha * l + jnp.sum(p, axis=1, keepdims=True)
        acc = alpha * acc + _dot(p.astype(BF16), v_ref[pl.ds(ks, blk), :])
        return m_new, l, acc

    init = (jnp.full((2 * blk, 1), NEG_BIG, F32), jnp.zeros((2 * blk, 1), F32), jnp.zeros((2 * blk, LANES), F32))
    carry = lax.fori_loop(0, qi, lambda ki, c: block(ki, c, False), init)
    _, l, acc = block(qi, carry, True)
    o = acc / l
    lane = lax.broadcasted_iota(jnp.int32, (blk, LANES), 1)
    o_ref[...] = jnp.where(lane < HEAD_DIM, o[0:blk], o[blk:2 * blk])


def _fox_prompt(q, k, v, f, ft, B, L, blk):
    nq = L // blk
    kern = functools.partial(_fox_prompt_kernel, blk=blk)
    return pl.pallas_call(
        kern,
        grid=(B, N_HEADS // 2, nq),
        in_specs=[pl.BlockSpec((blk, LANES), lambda b, h, i: (b * nq + i, h)),
                  pl.BlockSpec((L, LANES), lambda b, h, i: (b, h)),
                  pl.BlockSpec((L, LANES), lambda b, h, i: (b, h)),
                  pl.BlockSpec((blk, N_HEADS), lambda b, h, i: (b * nq + i, 0)),
                  pl.BlockSpec((1, 2, L), lambda b, h, i: (h, 0, b))],
        out_specs=pl.BlockSpec((blk, LANES), lambda b, h, i: (b * nq + i, h)),
        out_shape=jax.ShapeDtypeStruct((B * L, MIX_HALF), F32),
        compiler_params=_params(("parallel", "parallel", "arbitrary"), 32),
        name="fox_attention_prompt",
    )(q, k, v, f, ft)


def _sb_weights(z, c, u, mask):
    sp = _softplus(z)
    ln = -sp
    if mask is not None:
        ln = jnp.where(mask, ln, 0.0)
    hi, lo = _split2(ln)
    e = _dot(hi, u) + _dot(lo, u)
    a = jnp.exp((z - sp) + e + c)
    if mask is not None:
        a = jnp.where(mask, a, 0.0)
    return a, c + jnp.sum(ln, axis=1, keepdims=True)


def _sb_block(z, c, v, u, mask):
    a, c = _sb_weights(z, c, u, mask)
    return _dot(a.astype(BF16), v), c


def _sb_prompt_kernel(q_ref, k_ref, v_ref, u_ref, o_ref, *, blk):
    qi = pl.program_id(2)
    qs = _stack_heads(q_ref[...])
    row = lax.broadcasted_iota(jnp.int32, (2 * blk, blk), 0) & (blk - 1)
    col = lax.broadcasted_iota(jnp.int32, (2 * blk, blk), 1)
    u = u_ref[...]

    def block(ki, c, acc, diagonal):
        ks = pl.multiple_of(ki * blk, blk)
        z = _dot_nt(qs, k_ref[pl.ds(ks, blk), :])
        pv, c = _sb_block(z, c, v_ref[pl.ds(ks, blk), :], u, (col < row) if diagonal else None)
        return c, acc + pv

    c, acc = block(qi, jnp.zeros((2 * blk, 1), F32), jnp.zeros((2 * blk, LANES), F32), True)

    def cond(state):
        j, alive, _, _ = state
        return (j < qi) & (alive > 0)

    def body(state):
        j, _, c, acc = state
        c, acc = block(qi - 1 - j, c, acc, False)
        return j + 1, (jnp.max(c) > SB_DEAD).astype(jnp.int32), c, acc

    _, _, _, acc = lax.while_loop(cond, body, (jnp.int32(0), jnp.int32(1), c, acc))
    lane = lax.broadcasted_iota(jnp.int32, (blk, LANES), 1)
    o_ref[...] = jnp.where(lane < HEAD_DIM, acc[0:blk], acc[blk:2 * blk])


def _sb_prompt(q, k, v, u, B, L, blk):
    nq = L // blk
    kern = functools.partial(_sb_prompt_kernel, blk=blk)
    return pl.pallas_call(
        kern,
        grid=(B, N_HEADS // 2, nq),
        in_specs=[pl.BlockSpec((blk, LANES), lambda b, h, i: (b * nq + i, h)),
                  pl.BlockSpec((L, LANES), lambda b, h, i: (b, h)),
                  pl.BlockSpec((L, LANES), lambda b, h, i: (b, h)),
                  _full((blk, blk))],
        out_specs=pl.BlockSpec((blk, LANES), lambda b, h, i: (b * nq + i, h)),
        out_shape=jax.ShapeDtypeStruct((B * L, MIX_HALF), F32),
        compiler_params=_params(("parallel", "parallel", "arbitrary"), 32),
        name="sb_attention_prompt",
    )(q, k, v, u)


N_PAGES = 16
DEC_SEQ = 8
BD_ROWS = DEC_SEQ * N_HEADS
SB_AHEAD = 3
SB_SLOTS = SB_AHEAD + 1


def _block_diag_q(q, hm):
    rows = [jnp.broadcast_to(q[t:t + 1, :], (N_HEADS, MIX_HALF)) * hm for t in range(DEC_SEQ)]
    return (jnp.concatenate(rows, axis=0) * ATT_SCALE).astype(BF16)


def _pad_new(x):
    return jnp.concatenate([x, jnp.zeros((PAGE - DEC_SEQ, x.shape[1]), x.dtype)], axis=0)


def _merge_heads(o, hm):
    return jnp.concatenate(
        [jnp.sum(o[t * N_HEADS:(t + 1) * N_HEADS, :] * hm, axis=0, keepdims=True) for t in range(DEC_SEQ)], axis=0)


def _new_key_mask(strict):
    row = lax.broadcasted_iota(jnp.int32, (BD_ROWS, PAGE), 0) // N_HEADS
    col = lax.broadcasted_iota(jnp.int32, (BD_ROWS, PAGE), 1)
    return (col < row) if strict else (col <= row)


def _page_scores(qbd, k_pages, kn_ref, j):
    if j < N_PAGES:
        return _dot(qbd, k_pages[j][0].reshape(MIX_HALF, PAGE).astype(BF16))
    return _dot_nt(qbd, _pad_new(kn_ref[...]).astype(BF16))


def _page_values(w, v_pages, vn_ref, j):
    if j < N_PAGES:
        return _dot_nt(w, v_pages[j][0].reshape(MIX_HALF, PAGE).astype(BF16))
    return _dot(w, _pad_new(vn_ref[...]).astype(BF16))


def _cumsum_lanes(x):
    n = x.shape[1]
    lane = lax.broadcasted_iota(jnp.int32, x.shape, 1)
    k = 1
    while k < n:
        x = x + jnp.where(lane >= k, pltpu.roll(x, k, 1), 0.0)
        k *= 2
    return x


def _fox_decode_kernel(pt_ref, *refs):
    k_pages = refs[0:N_PAGES]
    v_pages = refs[N_PAGES:2 * N_PAGES]
    lf_pages = refs[2 * N_PAGES:3 * N_PAGES]
    q_ref, kn_ref, vn_ref, lftn_ref, hm_ref, o_ref = refs[3 * N_PAGES:]
    del pt_ref
    hm = hm_ref[...]
    qbd = _block_diag_q(q_ref[...], hm)
    parts = [lf_pages[j][0] for j in range(N_PAGES)]
    parts.append(jnp.concatenate([lftn_ref[0], jnp.zeros((N_HEADS, PAGE - DEC_SEQ), F32)], axis=1))
    fk = _cumsum_lanes(jnp.concatenate(parts, axis=1))
    fnew = fk[:, N_PAGES * PAGE:]
    lane = lax.broadcasted_iota(jnp.int32, (N_HEADS, PAGE), 1)
    fq = jnp.concatenate([jnp.sum(jnp.where(lane == t, fnew, 0.0), axis=1, keepdims=True)
                          for t in range(DEC_SEQ)], axis=0)
    scores = []
    for j in range(N_PAGES + 1):
        bias = jnp.concatenate([fk[:, j * PAGE:(j + 1) * PAGE]] * DEC_SEQ, axis=0)
        s = _page_scores(qbd, k_pages, kn_ref, j) + (fq - bias)
        if j == N_PAGES:
            s = jnp.where(_new_key_mask(False), s, -jnp.inf)
        scores.append(s)
    m = scores[0].max(axis=1, keepdims=True)
    for s in scores[1:]:
        m = jnp.maximum(m, s.max(axis=1, keepdims=True))
    l = jnp.zeros((BD_ROWS, 1), F32)
    acc = jnp.zeros((BD_ROWS, MIX_HALF), F32)
    for j in range(N_PAGES + 1):
        p = jnp.exp(scores[j] - m)
        l = l + jnp.sum(p, axis=1, keepdims=True)
        acc = acc + _page_values(p.astype(BF16), v_pages, vn_ref, j)
    o_ref[...] = _merge_heads(acc / l, hm)


def _sb_decode_kernel(pt_ref, q_ref, kn_ref, vn_ref, hm_ref, u_ref, ck_hbm, cv_hbm, o_ref, kbuf, vbuf, ksem, vsem):
    b = pl.program_id(0)
    nb = pl.num_programs(0)
    page_slot = lambda j: (N_PAGES - 1 - j) % SB_SLOTS

    def page_copies(seq, j, slot):
        page = pt_ref[seq, j]
        return (pltpu.make_async_copy(ck_hbm.at[page], kbuf.at[slot], ksem.at[slot]),
                pltpu.make_async_copy(cv_hbm.at[page], vbuf.at[slot], vsem.at[slot]))

    def start(seq, j, slot):
        for cp in page_copies(seq, j, slot):
            cp.start()

    def wait(slot):
        for cp in page_copies(b, 0, slot):
            cp.wait()

    def start_newest(seq):
        for d in range(SB_AHEAD):
            start(seq, N_PAGES - 1 - d, d)

    @pl.when(b == 0)
    def _():
        start_newest(0)

    hm = hm_ref[...]
    qbd = _block_diag_q(q_ref[...], hm)
    u = u_ref[...]
    a, c = _sb_weights(_dot_nt(qbd, _pad_new(kn_ref[...]).astype(BF16)), jnp.zeros((BD_ROWS, 1), F32), u,
                       _new_key_mask(True))
    acc = _dot(a.astype(BF16), _pad_new(vn_ref[...]).astype(BF16))
    is_alive = lambda c: (jnp.max(c) > SB_DEAD).astype(jnp.int32)

    def cond(state):
        j, alive, _, _ = state
        return (j >= 0) & (alive > 0)

    def body(state):
        j, _, c, acc = state
        slot = page_slot(j)
        wait(slot)

        @pl.when(j >= SB_AHEAD)
        def _():
            start(b, j - SB_AHEAD, page_slot(j - SB_AHEAD))

        a, c = _sb_weights(_dot(qbd, kbuf[slot].reshape(MIX_HALF, PAGE).astype(BF16)), c, u, None)
        acc = acc + _dot_nt(a.astype(BF16), vbuf[slot].reshape(MIX_HALF, PAGE).astype(BF16))
        return j - 1, is_alive(c), c, acc

    j, _, _, acc = lax.while_loop(cond, body, (jnp.int32(N_PAGES - 1), is_alive(c), c, acc))

    for d in range(SB_AHEAD):
        @pl.when(j - d >= 0)
        def _():
            wait(page_slot(j - d))

    @pl.when(b + 1 < nb)
    def _():
        start_newest(b + 1)

    o_ref[...] = _merge_heads(acc, hm)


def _page_specs(shape_tail, n):
    specs = []
    for j in range(n):
        specs.append(pl.BlockSpec((1,) + shape_tail, functools.partial(
            lambda b, pt, j: (pt[b, j],) + (0,) * len(shape_tail), j=j)))
    return specs


def _fox_decode(page_table, ck, cv, clf, q, kn, vn, lftn, hm):
    nb = page_table.shape[0]
    seq = lambda n: pl.BlockSpec((DEC_SEQ, n), lambda b, pt: (b, 0))
    const = lambda shape: pl.BlockSpec(shape, lambda b, pt: (0,) * len(shape))
    kv_page = (N_HEADS, HEAD_DIM, PAGE)
    in_specs = (_page_specs(kv_page, N_PAGES) + _page_specs(kv_page, N_PAGES) + _page_specs((N_HEADS, PAGE), N_PAGES)
                + [seq(MIX_HALF), seq(MIX_HALF), seq(MIX_HALF),
                   pl.BlockSpec((1, N_HEADS, DEC_SEQ), lambda b, pt: (b, 0, 0)), const((N_HEADS, MIX_HALF))])
    return pl.pallas_call(
        _fox_decode_kernel,
        grid_spec=pltpu.PrefetchScalarGridSpec(
            num_scalar_prefetch=1, grid=(nb,), in_specs=in_specs, out_specs=seq(MIX_HALF)),
        out_shape=jax.ShapeDtypeStruct((nb * DEC_SEQ, MIX_HALF), F32),
        compiler_params=_params(("arbitrary",), 48),
        name="fox_attention_decode",
    )(page_table, *([ck] * N_PAGES), *([cv] * N_PAGES), *([clf] * N_PAGES), q, kn, vn, lftn, hm)


def _sb_decode(page_table, ck, cv, q, kn, vn, hm, u):
    nb = page_table.shape[0]
    seq = lambda n: pl.BlockSpec((DEC_SEQ, n), lambda b, pt: (b, 0))
    const = lambda shape: pl.BlockSpec(shape, lambda b, pt: (0,) * len(shape))
    kv_page = (N_HEADS, HEAD_DIM, PAGE)
    in_specs = [seq(MIX_HALF), seq(MIX_HALF), seq(MIX_HALF), const((N_HEADS, MIX_HALF)), const((PAGE, PAGE)),
                pl.BlockSpec(memory_space=pl.ANY), pl.BlockSpec(memory_space=pl.ANY)]
    return pl.pallas_call(
        _sb_decode_kernel,
        grid_spec=pltpu.PrefetchScalarGridSpec(
            num_scalar_prefetch=1, grid=(nb,), in_specs=in_specs, out_specs=seq(MIX_HALF),
            scratch_shapes=[pltpu.VMEM((SB_SLOTS,) + kv_page, F32), pltpu.VMEM((SB_SLOTS,) + kv_page, F32),
                            pltpu.SemaphoreType.DMA((SB_SLOTS,)), pltpu.SemaphoreType.DMA((SB_SLOTS,))]),
        out_shape=jax.ShapeDtypeStruct((nb * DEC_SEQ, MIX_HALF), F32),
        compiler_params=_params(("arbitrary",), 32),
        name="sb_attention_decode",
    )(page_table, q, kn, vn, hm, u, ck, cv)


def _x_head_rms(a, g):
    outs = []
    for h in range(X_HEADS):
        ah = a[:, h * X_HEAD_DIM:(h + 1) * X_HEAD_DIM]
        outs.append(ah * lax.rsqrt(jnp.mean(ah * ah, axis=-1, keepdims=True) + EPS))
    return jnp.concatenate(outs, axis=-1) * g


def _mix_out_kernel(x_ref, ya_ref, yb_ref, wo_ref, g_ref, wq_ref, gq_ref, x1_ref, qc_ref):
    y = _dot(ya_ref[...].astype(BF16), wo_ref[0:MIX_HALF, :]) + _dot(yb_ref[...].astype(BF16), wo_ref[MIX_HALF:, :])
    x1 = x_ref[...] + y
    x1_ref[...] = x1
    hc = _rms(x1, g_ref[...]).astype(BF16)
    qc_ref[...] = _x_head_rms(_dot(hc, wq_ref[...]), gq_ref[...])


def _mix_out(x, ya, yb, wo, g, wq, gq, tm):
    T = x.shape[0]
    row = lambda n: pl.BlockSpec((tm, n), lambda i: (i, 0))
    return pl.pallas_call(
        _mix_out_kernel,
        grid=(T // tm,),
        in_specs=[row(D_MODEL), row(MIX_HALF), row(MIX_HALF), _full((D_MODEL, D_MODEL)), _full((1, D_MODEL)),
                  _full((D_MODEL, X_WIDTH)), _full((1, X_WIDTH))],
        out_specs=[row(D_MODEL), row(X_WIDTH)],
        out_shape=[jax.ShapeDtypeStruct((T, D_MODEL), F32), jax.ShapeDtypeStruct((T, X_WIDTH), F32)],
        compiler_params=_params(("parallel",), 40),
        name="mixer_out_proj",
    )(x, ya, yb, wo, g, wq, gq)


def _mem_kv_kernel(m_ref, wk_ref, wv_ref, gk_ref, k_ref, v_ref, *, nb):
    m = m_ref[...].astype(BF16)
    k = _x_head_rms(_dot(m, wk_ref[0]), gk_ref[0])
    v = _dot(m, wv_ref[0])
    for b in range(nb):
        for h in range(X_HEADS):
            ls = slice(h * X_HEAD_DIM, (h + 1) * X_HEAD_DIM)
            k_ref[0, b, :, h, :] = k[b * N_MEM:(b + 1) * N_MEM, ls]
            v_ref[0, b, :, h, :] = v[b * N_MEM:(b + 1) * N_MEM, ls]


def _mem_kv(mem, wk, wv, gk, nb):
    B = mem.shape[0] // N_MEM
    depth = wk.shape[0]
    out = pl.BlockSpec((1, nb, N_MEM, X_HEADS, X_HEAD_DIM), lambda l, i: (l, i, 0, 0, 0))
    return pl.pallas_call(
        functools.partial(_mem_kv_kernel, nb=nb),
        grid=(depth, B // nb),
        in_specs=[pl.BlockSpec((nb * N_MEM, D_MODEL), lambda l, i: (i, 0)),
                  pl.BlockSpec((1, D_MODEL, X_WIDTH), lambda l, i: (l, 0, 0)),
                  pl.BlockSpec((1, D_MODEL, X_WIDTH), lambda l, i: (l, 0, 0)),
                  pl.BlockSpec((1, 1, X_WIDTH), lambda l, i: (l, 0, 0))],
        out_specs=[out, out],
        out_shape=[jax.ShapeDtypeStruct((depth, B, N_MEM, X_HEADS, X_HEAD_DIM), F32)] * 2,
        compiler_params=_params(("parallel", "parallel"), 32),
        name="memory_kv",
    )(mem, wk, wv, gk)


def _cross_kernel(x_ref, q_ref, mk_ref, mv_ref, wo_ref, g_ref, x2_ref, hf_ref, *, ns, tq):
    outs = []
    for s in range(ns):
        q = q_ref[s * tq:(s + 1) * tq, :]
        heads = []
        for h in range(X_HEADS):
            ls = slice(h * X_HEAD_DIM, (h + 1) * X_HEAD_DIM)
            qh = (q[:, ls] * (X_HEAD_DIM ** -0.5)).astype(BF16)
            sc = _dot_nt(qh, mk_ref[0, s, :, h, :].astype(BF16))
            p = jnp.exp(sc - jnp.max(sc, axis=1, keepdims=True))
            o = _dot(p.astype(BF16), mv_ref[0, s, :, h, :].astype(BF16))
            heads.append(o / jnp.sum(p, axis=1, keepdims=True))
        outs.append(jnp.concatenate(heads, axis=-1))
    o = outs[0] if ns == 1 else jnp.concatenate(outs, axis=0)
    x2 = x_ref[...] + _dot(o.astype(BF16), wo_ref[...])
    x2_ref[...] = x2
    hf_ref[...] = _rms(x2, g_ref[...])


def _cross(x, q, mk, mv, layer, wo, g, ns, tq):
    T = x.shape[0]
    L = T // mk.shape[1]
    nq = L // tq
    rows = ns * tq
    row = lambda n: pl.BlockSpec((rows, n), lambda i: (i, 0))
    mem = pl.BlockSpec((1, ns, N_MEM, X_HEADS, X_HEAD_DIM), lambda i: (layer, i // nq, 0, 0, 0))
    kern = functools.partial(_cross_kernel, ns=ns, tq=tq)
    return pl.pallas_call(
        kern,
        grid=(T // rows,),
        in_specs=[row(D_MODEL), row(X_WIDTH), mem, mem, _full((X_WIDTH, D_MODEL)), _full((1, D_MODEL))],
        out_specs=[row(D_MODEL), row(D_MODEL)],
        out_shape=[jax.ShapeDtypeStruct((T, D_MODEL), F32)] * 2,
        compiler_params=_params(("parallel",), 48),
        name="cross_attention",
    )(x, q, mk, mv, wo, g)


def _swiglu_kernel(h_ref, x_ref, wg_ref, wu_ref, wd_ref, o_ref, hb, acc):
    f = pl.program_id(1)

    @pl.when(f == 0)
    def _():
        hb[...] = h_ref[...].astype(BF16)
        acc[...] = jnp.zeros_like(acc)

    h = hb[...]
    gte = _dot(h, wg_ref[...])
    act = gte * (1.0 / (1.0 + jnp.exp(-gte))) * _dot(h, wu_ref[...])
    acc[...] += _dot(act.astype(BF16), wd_ref[...])

    @pl.when(f == pl.num_programs(1) - 1)
    def _():
        o_ref[...] = x_ref[...] + acc[...]


def _swiglu(h, x, wg, wu, wd, tm, tf):
    T = h.shape[0]
    dff = wg.shape[1]
    row = pl.BlockSpec((tm, D_MODEL), lambda i, f: (i, 0))
    return pl.pallas_call(
        _swiglu_kernel,
        grid=(T // tm, dff // tf),
        in_specs=[row, row, pl.BlockSpec((D_MODEL, tf), lambda i, f: (0, f)),
                  pl.BlockSpec((D_MODEL, tf), lambda i, f: (0, f)), pl.BlockSpec((tf, D_MODEL), lambda i, f: (f, 0))],
        out_specs=row,
        out_shape=jax.ShapeDtypeStruct((T, D_MODEL), F32),
        scratch_shapes=[pltpu.VMEM((tm, D_MODEL), BF16), pltpu.VMEM((tm, D_MODEL), F32)],
        compiler_params=_params(("parallel", "arbitrary"), 56),
        name="dense_swiglu",
    )(h, x, wg, wu, wd)


def _router_kernel(h_ref, w_ref, b_ref, r_ref):
    h_hi, h_lo = _split2(h_ref[...])
    w_hi, w_lo = _split2(w_ref[...])
    logits = _dot(h_hi, w_hi) + _dot(h_hi, w_lo) + _dot(h_lo, w_hi) + b_ref[...]
    lane = lax.broadcasted_iota(jnp.int32, logits.shape, 1)
    logits = jnp.where(lane < N_EXPERTS, logits, -jnp.inf)
    m1 = jnp.max(logits, axis=1, keepdims=True)
    i1 = jnp.min(jnp.where(logits == m1, lane, LANES), axis=1, keepdims=True)
    rest = jnp.where(lane == i1, -jnp.inf, logits)
    m2 = jnp.max(rest, axis=1, keepdims=True)
    i2 = jnp.min(jnp.where(rest == m2, lane, LANES), axis=1, keepdims=True)
    e2 = jnp.exp(m2 - m1)
    g1 = 1.0 / (1.0 + e2)
    g2 = e2 / (1.0 + e2)
    r_ref[...] = jnp.where(lane == 0, i1.astype(F32), jnp.where(lane == 1, i2.astype(F32),
                           jnp.where(lane == 2, g1, jnp.where(lane == 3, g2, 0.0))))


def _router(h, w, b, tm):
    T = h.shape[0]
    return pl.pallas_call(
        _router_kernel,
        grid=(T // tm,),
        in_specs=[pl.BlockSpec((tm, D_MODEL), lambda i: (i, 0)), _full((D_MODEL, LANES)), _full((1, LANES))],
        out_specs=pl.BlockSpec((tm, LANES), lambda i: (i, 0)),
        out_shape=jax.ShapeDtypeStruct((T, LANES), F32),
        compiler_params=_params(("parallel",), 32),
        name="moe_router",
    )(h, w, b)


MOE_TM = 512
MOE_NF = 2
COMBINE_TM = 512


def _moe_plan(r, tm):
    T = r.shape[0]
    n_pairs = 2 * T
    e = r[:, 0:2].astype(jnp.int32).T.reshape(-1)
    onehot = (e[:, None] == jnp.arange(N_EXPERTS, dtype=jnp.int32)[None, :]).astype(jnp.int32)
    csum = jnp.cumsum(onehot, axis=0)
    rank = jnp.sum(csum * onehot, axis=1) - 1
    psz = (csum[-1] + tm - 1) // tm * tm
    gend = jnp.cumsum(psz)
    pos = (gend - psz)[e] + rank
    n_rows = n_pairs + N_EXPERTS * tm
    n_tiles = n_rows // tm
    src = jnp.zeros((n_rows,), jnp.int32).at[pos].set(jnp.arange(n_pairs, dtype=jnp.int32) % T)
    t0 = jnp.arange(n_tiles, dtype=jnp.int32) * tm
    tile_e = jnp.minimum(jnp.sum((t0[:, None] >= gend[None, :]).astype(jnp.int32), axis=1), N_EXPERTS - 1)
    n_valid = (gend[-1] // tm).astype(jnp.int32).reshape(1)
    return src.reshape(n_tiles, 1, tm), pos.astype(jnp.int32), tile_e, n_valid


def _row_gather(src_hbm, idx_ref, dst, sem, r0, n, unrolled):
    def start(r):
        pltpu.make_async_copy(src_hbm.at[pl.ds(idx_ref[0, 0, r0 + r], 1)], dst.at[pl.ds(r0 + r, 1)], sem).start()

    if unrolled:
        for r in range(n):
            start(r)
    else:
        def body(r, carry):
            start(r)
            return carry
        lax.fori_loop(0, n, body, 0, unroll=8)


def _moe_kernel(te_ref, nv_ref, src_ref, srcn_ref, h_hbm, wg_ref, wu_ref, wd_ref, y_ref, xin, hb, gsem, *, tm):
    del te_ref
    i = pl.program_id(0)
    f = pl.program_id(1)
    nt = pl.num_programs(0)
    nf = pl.num_programs(1)
    slot = i % 2
    is_valid = i < nv_ref[0]
    part = tm // MOE_NF
    r0 = f * part

    def wait_tile(s):
        pltpu.make_async_copy(h_hbm.at[pl.ds(0, tm)], xin.at[s], gsem.at[s]).wait()

    @pl.when((i == 0) & (f == 0))
    def _():
        _row_gather(h_hbm, src_ref, xin.at[0], gsem.at[0], 0, tm, False)

    @pl.when(f == 0)
    def _():
        wait_tile(slot)
        hb[...] = xin[slot].astype(BF16)
        y_ref[...] = jnp.zeros_like(y_ref)

    @pl.when(is_valid)
    def _():
        _row_gather(h_hbm, srcn_ref, xin.at[1 - slot], gsem.at[1 - slot], r0, part, True)
        h = hb[...]
        gte = _dot(h, wg_ref[0])
        act = gte * (1.0 / (1.0 + jnp.exp(-gte))) * _dot(h, wu_ref[0])
        y_ref[...] += _dot(act.astype(BF16), wd_ref[0])

    @pl.when(jnp.logical_not(is_valid))
    def _():
        _row_gather(h_hbm, srcn_ref, xin.at[1 - slot], gsem.at[1 - slot], r0, part, False)

    @pl.when((i == nt - 1) & (f == nf - 1))
    def _():
        wait_tile(1 - slot)


def _moe(h, src, tile_e, n_valid, wg, wu, wd, tm):
    n_tiles = src.shape[0]
    nf = MOE_NF
    tf = wg.shape[2] // nf
    fser = lambda i, f: jnp.where(i % 2 == 0, f, nf - 1 - f)
    smem_tile = lambda imap: pl.BlockSpec((1, 1, tm), imap, memory_space=pltpu.SMEM)
    kern = functools.partial(_moe_kernel, tm=tm)
    return pl.pallas_call(
        kern,
        grid_spec=pltpu.PrefetchScalarGridSpec(
            num_scalar_prefetch=2,
            grid=(n_tiles, nf),
            in_specs=[smem_tile(lambda i, f, te, nv: (i, 0, 0)),
                      smem_tile(lambda i, f, te, nv: (jnp.minimum(i + 1, n_tiles - 1), 0, 0)),
                      pl.BlockSpec(memory_space=pl.ANY),
                      pl.BlockSpec((1, D_MODEL, tf), lambda i, f, te, nv: (te[i], 0, fser(i, f))),
                      pl.BlockSpec((1, D_MODEL, tf), lambda i, f, te, nv: (te[i], 0, fser(i, f))),
                      pl.BlockSpec((1, tf, D_MODEL), lambda i, f, te, nv: (te[i], fser(i, f), 0))],
            out_specs=pl.BlockSpec((tm, D_MODEL), lambda i, f, te, nv: (i, 0)),
            scratch_shapes=[pltpu.VMEM((2, tm, D_MODEL), F32), pltpu.VMEM((tm, D_MODEL), BF16),
                            pltpu.SemaphoreType.DMA((2,))]),
        out_shape=jax.ShapeDtypeStruct((n_tiles * tm, D_MODEL), F32),
        compiler_params=_params(("arbitrary", "arbitrary"), 56),
        name="moe_swiglu",
    )(tile_e, n_valid, src, src, h, wg, wu, wd)


def _combine_kernel(pos_ref, posn_ref, x_ref, r_ref, y_hbm, o_ref, buf, sem, *, tm):
    i = pl.program_id(0)
    nt = pl.num_programs(0)
    slot = i % 2

    @pl.when(i == 0)
    def _():
        _row_gather(y_hbm, pos_ref, buf.at[0], sem.at[0], 0, 2 * tm, False)

    @pl.when(i + 1 < nt)
    def _():
        _row_gather(y_hbm, posn_ref, buf.at[1 - slot], sem.at[1 - slot], 0, 2 * tm, False)

    pltpu.make_async_copy(y_hbm.at[pl.ds(0, 2 * tm)], buf.at[slot], sem.at[slot]).wait()
    r = r_ref[...]
    o_ref[...] = x_ref[...] + (r[:, 2:3] * buf[slot, 0:tm, :] + r[:, 3:4] * buf[slot, tm:2 * tm, :])


def _combine(x, r, pos, y, tm):
    T = x.shape[0]
    nt = T // tm
    smem_tile = lambda imap: pl.BlockSpec((1, 1, 2 * tm), imap, memory_space=pltpu.SMEM)
    return pl.pallas_call(
        functools.partial(_combine_kernel, tm=tm),
        grid=(nt,),
        in_specs=[smem_tile(lambda i: (i, 0, 0)), smem_tile(lambda i: (jnp.minimum(i + 1, nt - 1), 0, 0)),
                  pl.BlockSpec((tm, D_MODEL), lambda i: (i, 0)), pl.BlockSpec((tm, LANES), lambda i: (i, 0)),
                  pl.BlockSpec(memory_space=pl.ANY)],
        out_specs=pl.BlockSpec((tm, D_MODEL), lambda i: (i, 0)),
        out_shape=jax.ShapeDtypeStruct((T, D_MODEL), F32),
        scratch_shapes=[pltpu.VMEM((2, 2 * tm, D_MODEL), F32), pltpu.SemaphoreType.DMA((2,))],
        compiler_params=_params(("arbitrary",), 40),
        name="moe_combine",
    )(pos, pos, x, r, y)


def _tri(n, kind):
    r = lax.broadcasted_iota(jnp.int32, (n, n), 0)
    c = lax.broadcasted_iota(jnp.int32, (n, n), 1)
    m = {"lower_incl": r >= c, "upper_incl": r <= c, "lower_strict": r > c}[kind]
    return m.astype(BF16)


def _head_mask(n_heads, head_dim):
    h = lax.broadcasted_iota(jnp.int32, (n_heads, n_heads * head_dim), 0)
    c = lax.broadcasted_iota(jnp.int32, (n_heads, n_heads * head_dim), 1) // head_dim
    return (h == c).astype(F32)


def _stack(x, mem_k, mem_v, W, group):
    T = x.shape[0]
    B, L = group["B"], group["L"]
    prompt = group["kind"] == "prompt"
    tm = 1024 if prompt else 512
    q_dtype = BF16 if prompt else F32
    new = {}

    p, q, k32, kb, v32, vb, lf, lft = _in_even(x, W["g_mix0"], W["w_in_e"], W["wf"], W["wft"], W["bf"], W["bft"],
                                               W["g_fq"], W["g_fk"], W["bd64"], tm, q_dtype)
    ya, pool_state = _pool(group["pool_prefix"], p.reshape(B, L, MIX_HALF), W["w_pool"], W["pool_scale"],
                           ns=group["seq_per_step"], ch=group["chunk"], start_pos=group["start_pos"])
    if prompt:
        f, ft = _cumsum_prompt(lf, lft, W["tril256"], W["triu256"], B, L, 256)
        yb = _fox_prompt(q, kb, vb, f, ft.reshape(N_HEADS // 2, 2, T), B, L, 256)
    else:
        lftn = lft.reshape(N_HEADS, B, L).transpose(1, 0, 2)
        yb = _fox_decode(group["page_table"], group["fox_k"], group["fox_v"], group["fox_lf"], q, k32, v32, lftn,
                         W["hm64"])
    new.update(fox_k=k32, fox_v=v32, fox_logf=lf, pool=pool_state)
    x, qc = _mix_out(x, ya.reshape(T, MIX_HALF), yb, W["w_out_e"], W["g_cross0"], W["w_cq0"], W["g_cq0"], tm)
    x, hf = _cross(x, qc, mem_k, mem_v, 0, W["w_co0"], W["g_ffn0"], ns=group["x_ns"], tq=group["x_tq"])
    x = _swiglu(hf, x, W["w_gate"], W["w_up"], W["w_down"], 512, 1408)

    u, q, k32, kb, v32, vb = _in_odd(x, W["g_mix1"], W["w_in_o"], tm, q_dtype)
    yc, conv_state = _conv(group["conv_prefix"], u.reshape(B, L, MIX_HALF), W["conv_w"], W["conv_b"], W["ln_g"],
                           W["ln_b"], ns=group["seq_per_step"], ch=group["conv_chunk"])
    if prompt:
        yd = _sb_prompt(q, kb, vb, W["ustrict256"], B, L, 256)
    else:
        yd = _sb_decode(group["page_table"], group["sb_k"], group["sb_v"], q, k32, v32, W["hm64"], W["ustrict128"])
    new.update(sb_k=k32, sb_v=v32, conv=conv_state)
    x, qc = _mix_out(x, yc.reshape(T, MIX_HALF), yd, W["w_out_o"], W["g_cross1"], W["w_cq1"], W["g_cq1"], tm)
    x, hf = _cross(x, qc, mem_k, mem_v, 1, W["w_co1"], W["g_ffn1"], ns=group["x_ns"], tq=group["x_tq"])
    r = _router(hf, W["w_router"], W["b_router"], 512)
    return x, hf, r, new


def kernel(x_prompt, x_sample, cache_fox_k, cache_fox_v, cache_fox_logf, cache_sb_k, cache_sb_v, cache_mem_k, cache_mem_v, state_pool, state_conv, page_table, mem_prompt, norm_mix, norm_cross, norm_ffn, w_cq, w_ck, w_cv, w_co, g_cq, g_ck, w_in_e, b_f, w_pool_grp, pool_scale, g_fq, g_fk, w_out_e, w_gate, w_up, w_down, w_in_o, conv_w, conv_b, ln_g, ln_b, w_out_o, w_router, b_router, we_gate, we_up, we_down):
    B, L, D = x_prompt.shape
    SB, SL, _ = x_sample.shape
    n_pool = cache_fox_k.shape[1]
    bf = lambda a: a.astype(BF16)
    row = lambda a: a.reshape(1, -1)

    wf = jnp.pad(w_in_e[0][:, 4 * MIX_HALF:], ((0, 0), (0, LANES - N_HEADS)))
    W = dict(
        g_mix0=row(norm_mix[0]), g_mix1=row(norm_mix[1]), g_cross0=row(norm_cross[0]), g_cross1=row(norm_cross[1]),
        g_ffn0=row(norm_ffn[0]), g_ffn1=row(norm_ffn[1]),
        w_in_e=bf(w_in_e[0][:, :4 * MIX_HALF]), wf=bf(wf), wft=bf(w_in_e[0][:, 4 * MIX_HALF:].T),
        bf=jnp.pad(row(b_f[0]), ((0, 0), (0, LANES - N_HEADS))), bft=b_f[0].reshape(N_HEADS, 1),
        g_fq=row(jnp.tile(g_fq[0], N_HEADS)), g_fk=row(jnp.tile(g_fk[0], N_HEADS)),
        w_pool=bf(w_pool_grp[0]), pool_scale=row(pool_scale[0]), w_out_e=bf(w_out_e[0]),
        w_cq0=bf(w_cq[0]), w_cq1=bf(w_cq[1]), w_co0=bf(w_co[0]), w_co1=bf(w_co[1]),
        g_cq0=row(jnp.tile(g_cq[0], X_HEADS)), g_cq1=row(jnp.tile(g_cq[1], X_HEADS)),
        w_gate=bf(w_gate[0]), w_up=bf(w_up[0]), w_down=bf(w_down[0]),
        w_in_o=bf(w_in_o[0]), conv_w=conv_w[0], conv_b=row(conv_b[0]), ln_g=row(ln_g[0]), ln_b=row(ln_b[0]),
        w_out_o=bf(w_out_o[0]),
        w_router=jnp.pad(w_router[0], ((0, 0), (0, LANES - N_EXPERTS))),
        b_router=jnp.pad(row(b_router[0]), ((0, 0), (0, LANES - N_EXPERTS))),
        we_gate=bf(we_gate[0]), we_up=bf(we_up[0]), we_down=bf(we_down[0]),
        bd64=jnp.kron(jnp.eye(N_HEADS, dtype=F32), jnp.ones((HEAD_DIM, HEAD_DIM), F32)).astype(BF16),
        hm64=_head_mask(N_HEADS, HEAD_DIM), eye8=jnp.eye(N_HEADS, dtype=BF16),
        tril256=_tri(256, "lower_incl"), triu256=_tri(256, "upper_incl"),
        ustrict256=_tri(256, "lower_strict"), ustrict128=_tri(128, "lower_strict"),
    )

    mk, mv = _mem_kv(mem_prompt.reshape(B * N_MEM, D), bf(w_ck), bf(w_cv),
                     jnp.tile(g_ck, (1, X_HEADS)).reshape(-1, 1, X_WIDTH), 2)

    prompt = dict(kind="prompt", B=B, L=L, start_pos=0, seq_per_step=1, chunk=256, conv_chunk=128,
                  pool_prefix=jnp.zeros((B, POOL_BUF, MIX_HALF), F32),
                  conv_prefix=jnp.zeros((B, CONV_BUF, MIX_HALF), F32), x_ns=1, x_tq=1024)
    xp, hp, rp, sp = _stack(x_prompt.reshape(B * L, D), mk, mv, W, prompt)

    kv_view = lambda c: jnp.transpose(c[0], (0, 2, 3, 1))
    sample = dict(kind="sample", B=SB, L=SL, start_pos=page_table.shape[1] * PAGE, seq_per_step=16,
                  chunk=SL, conv_chunk=SL, pool_prefix=state_pool[0], conv_prefix=state_conv[0], x_ns=8, x_tq=SL,
                  page_table=page_table,
                  fox_k=kv_view(cache_fox_k), fox_v=kv_view(cache_fox_v),
                  fox_lf=jnp.transpose(cache_fox_logf[0], (0, 2, 1)),
                  sb_k=kv_view(cache_sb_k), sb_v=kv_view(cache_sb_v))
    xs, hs, rs, ss = _stack(x_sample.reshape(SB * SL, D), cache_mem_k, cache_mem_v, W, sample)

    n_tok = B * L + SB * SL
    src, pos, tile_e, n_valid = _moe_plan(jnp.concatenate([rp, rs], axis=0), MOE_TM)
    ye = _moe(jnp.concatenate([hp, hs], axis=0), src, tile_e, n_valid, W["we_gate"], W["we_up"], W["we_down"], MOE_TM)

    def pair_rows(row0, n):
        both = [pos[s * n_tok + row0:s * n_tok + row0 + n].reshape(n // COMBINE_TM, COMBINE_TM) for s in range(2)]
        return jnp.concatenate(both, axis=1).reshape(n // COMBINE_TM, 1, 2 * COMBINE_TM)

    yp = _combine(xp, rp, pair_rows(0, B * L), ye, COMBINE_TM)
    ys = _combine(xs, rs, pair_rows(B * L, SB * SL), ye, COMBINE_TM)

    def outs(new, b, l):
        kv = lambda a: a.reshape(1, b, l, N_HEADS, HEAD_DIM)
        return (kv(new["fox_k"]), kv(new["fox_v"]), new["fox_logf"].reshape(1, b, l, N_HEADS),
                kv(new["sb_k"]), kv(new["sb_v"]), new["pool"][None], new["conv"][None])

    op = outs(sp, B, L)
    os_ = outs(ss, SB, SL)
    return (yp.reshape(B, L, D), ys.reshape(SB, SL, D), *op, mk, mv, *os_)
```

```python
import functools

import jax
import jax.numpy as jnp
from jax import lax
from jax.experimental import pallas as pl
from jax.experimental.pallas import tpu as pltpu

F32 = jnp.float32
BF16 = jnp.bfloat16

D_MODEL = 1024
MIX_HALF = 512
HEAD_DIM = 64
N_HEADS = 8
POOL_WINDOWS = (2, 4, 8, 16)
POOL_GROUP = 128
POOL_BUF = 15
CONV_WIDTH = 31
CONV_BUF = 30
N_MEM = 256
X_HEADS = 4
X_HEAD_DIM = 128
X_WIDTH = 512
N_EXPERTS = 8
PAGE = 128
EPS = 1e-6
ATT_SCALE = HEAD_DIM ** -0.5
NEG_BIG = -1e30

LANES = 128
SUBLANES = 8
MIB = 1024 * 1024


def _params(sem, vmem_mib):
    return pltpu.CompilerParams(dimension_semantics=sem, vmem_limit_bytes=vmem_mib * MIB)


def _full(shape):
    return pl.BlockSpec(shape, lambda *_: (0,) * len(shape))


def _rms(x, g):
    return x * lax.rsqrt(jnp.mean(x * x, axis=-1, keepdims=True) + EPS) * g


def _split2(x):
    hi = x.astype(BF16)
    lo = (x - hi.astype(F32)).astype(BF16)
    return hi, lo


def _split3(x):
    hi = x.astype(BF16)
    r = x - hi.astype(F32)
    mid = r.astype(BF16)
    lo = (r - mid.astype(F32)).astype(BF16)
    return hi, mid, lo


def _dot(a, b):
    return jnp.dot(a, b, preferred_element_type=F32)


def _dot_nt(a, b):
    return lax.dot_general(a, b, (((1,), (1,)), ((), ())), preferred_element_type=F32)


def _softplus(z):
    return jnp.maximum(z, 0.0) + jnp.log1p(jnp.exp(-jnp.abs(z)))


def _head_rms(a, g, bd, head_dim):
    hi, lo = _split2(a * a)
    ssq = _dot(hi, bd) + _dot(lo, bd)
    return a * lax.rsqrt(ssq * (1.0 / head_dim) + EPS) * g


def _in_even_kernel(x_ref, g_ref, w_ref, wf_ref, wft_ref, bf_ref, bft_ref, gq_ref, gk_ref, bd_ref,
                    p_ref, q_ref, k32_ref, kb_ref, v32_ref, vb_ref, lf_ref, lft_ref):
    h = _rms(x_ref[...], g_ref[...]).astype(BF16)
    bd = bd_ref[...]
    p_ref[...] = _dot(h, w_ref[:, 0:MIX_HALF])
    q = _head_rms(_dot(h, w_ref[:, MIX_HALF:2 * MIX_HALF]), gq_ref[...], bd, HEAD_DIM)
    q_ref[...] = q.astype(q_ref.dtype)
    k = _head_rms(_dot(h, w_ref[:, 2 * MIX_HALF:3 * MIX_HALF]), gk_ref[...], bd, HEAD_DIM)
    k32_ref[...] = k
    kb_ref[...] = k.astype(BF16)
    v = _dot(h, w_ref[:, 3 * MIX_HALF:4 * MIX_HALF])
    v32_ref[...] = v
    vb_ref[...] = v.astype(BF16)
    fl = _dot(h, wf_ref[...]) + bf_ref[...]
    lf_ref[...] = (-_softplus(-fl))[:, 0:N_HEADS]
    flt = _dot_nt(wft_ref[...], h) + bft_ref[...]
    lft_ref[...] = -_softplus(-flt)


def _in_even(x, g, w, wf, wft, bf, bft, gq, gk, bd, tm, q_dtype):
    T = x.shape[0]
    row = lambda n: pl.BlockSpec((tm, n), lambda i: (i, 0))
    sds = jax.ShapeDtypeStruct
    return pl.pallas_call(
        _in_even_kernel,
        grid=(T // tm,),
        in_specs=[row(D_MODEL), _full((1, D_MODEL)), _full((D_MODEL, 4 * MIX_HALF)), _full((D_MODEL, LANES)),
                  _full((N_HEADS, D_MODEL)), _full((1, LANES)), _full((N_HEADS, 1)), _full((1, MIX_HALF)),
                  _full((1, MIX_HALF)), _full((MIX_HALF, MIX_HALF))],
        out_specs=[row(MIX_HALF), row(MIX_HALF), row(MIX_HALF), row(MIX_HALF), row(MIX_HALF), row(MIX_HALF),
                   row(N_HEADS), pl.BlockSpec((N_HEADS, tm), lambda i: (0, i))],
        out_shape=[sds((T, MIX_HALF), F32), sds((T, MIX_HALF), q_dtype), sds((T, MIX_HALF), F32),
                   sds((T, MIX_HALF), BF16), sds((T, MIX_HALF), F32), sds((T, MIX_HALF), BF16),
                   sds((T, N_HEADS), F32), sds((N_HEADS, T), F32)],
        compiler_params=_params(("parallel",), 48),
        name="in_even",
    )(x, g, w, wf, wft, bf, bft, gq, gk, bd)


def _in_odd_kernel(x_ref, g_ref, w_ref, u_ref, q_ref, k32_ref, kb_ref, v32_ref, vb_ref):
    h = _rms(x_ref[...], g_ref[...]).astype(BF16)
    a = _dot(h, w_ref[:, 0:MIX_HALF])
    gate = _dot(h, w_ref[:, MIX_HALF:2 * MIX_HALF])
    u_ref[...] = a * (1.0 / (1.0 + jnp.exp(-gate)))
    q_ref[...] = _dot(h, w_ref[:, 2 * MIX_HALF:3 * MIX_HALF]).astype(q_ref.dtype)
    k = _dot(h, w_ref[:, 3 * MIX_HALF:4 * MIX_HALF])
    k32_ref[...] = k
    kb_ref[...] = k.astype(BF16)
    v = _dot(h, w_ref[:, 4 * MIX_HALF:5 * MIX_HALF])
    v32_ref[...] = v
    vb_ref[...] = v.astype(BF16)


def _in_odd(x, g, w, tm, q_dtype):
    T = x.shape[0]
    row = lambda n: pl.BlockSpec((tm, n), lambda i: (i, 0))
    sds = jax.ShapeDtypeStruct
    return pl.pallas_call(
        _in_odd_kernel,
        grid=(T // tm,),
        in_specs=[row(D_MODEL), _full((1, D_MODEL)), _full((D_MODEL, 5 * MIX_HALF))],
        out_specs=[row(MIX_HALF)] * 6,
        out_shape=[sds((T, MIX_HALF), F32), sds((T, MIX_HALF), q_dtype), sds((T, MIX_HALF), F32),
                   sds((T, MIX_HALF), BF16), sds((T, MIX_HALF), F32), sds((T, MIX_HALF), BF16)],
        compiler_params=_params(("parallel",), 48),
        name="in_odd",
    )(x, g, w)


POOL_PAD = 16
CONV_PAD = 32


def _pool_kernel(pre_ref, p_ref, w_ref, sc_ref, y_ref, st_ref, buf, *, ns, L, ch, start_pos):
    for s in range(ns):
        buf[0:POOL_PAD - POOL_BUF, :] = jnp.zeros((POOL_PAD - POOL_BUF, MIX_HALF), F32)
        buf[POOL_PAD - POOL_BUF:POOL_PAD, :] = pre_ref[s]
        buf[POOL_PAD:POOL_PAD + L, :] = p_ref[s]
        st_ref[s] = buf[POOL_PAD + L - POOL_BUF:POOL_PAD + L, :]
        for c0 in range(0, L, ch):
            pos = start_pos + c0 + lax.broadcasted_iota(jnp.int32, (ch, POOL_GROUP), 0)
            outs = []
            for gi, w in enumerate(POOL_WINDOWS):
                ls = slice(gi * POOL_GROUP, (gi + 1) * POOL_GROUP)
                cur = buf[POOL_PAD + c0:POOL_PAD + c0 + ch, ls]
                acc = cur
                for i in range(1, w):
                    acc = acc + buf[POOL_PAD + c0 - i:POOL_PAD + c0 - i + ch, ls]
                cnt = jnp.minimum(pos + 1, w).astype(F32)
                d = acc / cnt - cur
                outs.append(_dot(d.astype(BF16), w_ref[gi]) * sc_ref[:, ls])
            y_ref[s, c0:c0 + ch, :] = jnp.concatenate(outs, axis=-1)


def _pool(prefix, p, w_grp, scale, ns, ch, start_pos):
    B, L, _ = p.shape
    kern = functools.partial(_pool_kernel, ns=ns, L=L, ch=ch, start_pos=start_pos)
    return pl.pallas_call(
        kern,
        grid=(B // ns,),
        in_specs=[pl.BlockSpec((ns, POOL_BUF, MIX_HALF), lambda i: (i, 0, 0)),
                  pl.BlockSpec((ns, L, MIX_HALF), lambda i: (i, 0, 0)),
                  _full((len(POOL_WINDOWS), POOL_GROUP, POOL_GROUP)), _full((1, MIX_HALF))],
        out_specs=[pl.BlockSpec((ns, L, MIX_HALF), lambda i: (i, 0, 0)),
                   pl.BlockSpec((ns, POOL_BUF, MIX_HALF), lambda i: (i, 0, 0))],
        out_shape=[jax.ShapeDtypeStruct((B, L, MIX_HALF), F32),
                   jax.ShapeDtypeStruct((B, POOL_BUF, MIX_HALF), F32)],
        scratch_shapes=[pltpu.VMEM((POOL_PAD + L, MIX_HALF), F32)],
        compiler_params=_params(("parallel",), 40),
        name="pool_mixer",
    )(prefix, p, w_grp, scale)


def _conv_kernel(pre_ref, u_ref, w_ref, b_ref, lg_ref, lb_ref, y_ref, st_ref, buf, *, ns, L, ch):
    for s in range(ns):
        buf[0:CONV_PAD - CONV_BUF, :] = jnp.zeros((CONV_PAD - CONV_BUF, MIX_HALF), F32)
        buf[CONV_PAD - CONV_BUF:CONV_PAD, :] = pre_ref[s]
        buf[CONV_PAD:CONV_PAD + L, :] = u_ref[s]
        st_ref[s] = buf[CONV_PAD + L - CONV_BUF:CONV_PAD + L, :]
        base = CONV_PAD - CONV_BUF
        for c0 in range(0, L, ch):
            acc = buf[base + c0:base + c0 + ch, :] * w_ref[0:1, :]
            for j in range(1, CONV_WIDTH):
                acc = acc + buf[base + c0 + j:base + c0 + j + ch, :] * w_ref[j:j + 1, :]
            y = acc + b_ref[...]
            mu = jnp.mean(y, axis=-1, keepdims=True)
            yc = y - mu
            var = jnp.mean(yc * yc, axis=-1, keepdims=True)
            yn = yc * lax.rsqrt(var + EPS) * lg_ref[...] + lb_ref[...]
            y_ref[s, c0:c0 + ch, :] = yn * (1.0 / (1.0 + jnp.exp(-yn)))


def _conv(prefix, u, conv_w, conv_b, ln_g, ln_b, ns, ch):
    B, L, _ = u.shape
    kern = functools.partial(_conv_kernel, ns=ns, L=L, ch=ch)
    return pl.pallas_call(
        kern,
        grid=(B // ns,),
        in_specs=[pl.BlockSpec((ns, CONV_BUF, MIX_HALF), lambda i: (i, 0, 0)),
                  pl.BlockSpec((ns, L, MIX_HALF), lambda i: (i, 0, 0)),
                  _full((CONV_WIDTH, MIX_HALF)), _full((1, MIX_HALF)), _full((1, MIX_HALF)), _full((1, MIX_HALF))],
        out_specs=[pl.BlockSpec((ns, L, MIX_HALF), lambda i: (i, 0, 0)),
                   pl.BlockSpec((ns, CONV_BUF, MIX_HALF), lambda i: (i, 0, 0))],
        out_shape=[jax.ShapeDtypeStruct((B, L, MIX_HALF), F32),
                   jax.ShapeDtypeStruct((B, CONV_BUF, MIX_HALF), F32)],
        scratch_shapes=[pltpu.VMEM((CONV_PAD + L, MIX_HALF), F32)],
        compiler_params=_params(("parallel",), 40),
        name="conv_module",
    )(prefix, u, conv_w, conv_b, ln_g, ln_b)


def _cumsum_prompt_kernel(lf_ref, lft_ref, tril_ref, triu_ref, f_ref, ft_ref, *, L, blk):
    c_col = jnp.zeros((1, N_HEADS), F32)
    c_row = jnp.zeros((N_HEADS, 1), F32)
    for b0 in range(0, L, blk):
        x = lf_ref[b0:b0 + blk, :]
        xt = lft_ref[:, b0:b0 + blk]
        f = c_col
        for part in _split3(x):
            f = f + _dot(tril_ref[...], part)
        ft = c_row
        for part in _split3(xt):
            ft = ft + _dot(part, triu_ref[...])
        f_ref[b0:b0 + blk, :] = f
        ft_ref[:, b0:b0 + blk] = ft
        c_col = f[blk - 1:blk, :]
        c_row = ft[:, blk - 1:blk]


def _cumsum_prompt(lf, lft, tril, triu, B, L, blk):
    kern = functools.partial(_cumsum_prompt_kernel, L=L, blk=blk)
    return pl.pallas_call(
        kern,
        grid=(B,),
        in_specs=[pl.BlockSpec((L, N_HEADS), lambda b: (b, 0)), pl.BlockSpec((N_HEADS, L), lambda b: (0, b)),
                  _full((blk, blk)), _full((blk, blk))],
        out_specs=[pl.BlockSpec((L, N_HEADS), lambda b: (b, 0)), pl.BlockSpec((N_HEADS, L), lambda b: (0, b))],
        out_shape=[jax.ShapeDtypeStruct((B * L, N_HEADS), F32), jax.ShapeDtypeStruct((N_HEADS, B * L), F32)],
        compiler_params=_params(("parallel",), 32),
        name="fox_cumsum_prompt",
    )(lf, lft, tril, triu)


def _head_pair(q):
    lane = lax.broadcasted_iota(jnp.int32, q.shape, 1)
    zero = jnp.zeros_like(q)
    scale = jnp.asarray(ATT_SCALE, q.dtype)
    return [jnp.where(lane < HEAD_DIM, q, zero) * scale, jnp.where(lane >= HEAD_DIM, q, zero) * scale]


def _stack_heads(q):
    return jnp.concatenate(_head_pair(q), axis=0)


SB_DEAD = -105.0


def _fox_prompt_kernel(q_ref, k_ref, v_ref, f_ref, ft_ref, o_ref, *, blk):
    hp = pl.program_id(1)
    qi = pl.program_id(2)
    qs = _stack_heads(q_ref[...])
    f = f_ref[...]
    lane8 = lax.broadcasted_iota(jnp.int32, f.shape, 1)
    fq = [jnp.sum(jnp.where(lane8 == 2 * hp + i, f, 0.0), axis=1, keepdims=True) for i in range(2)]
    row = lax.broadcasted_iota(jnp.int32, (2 * blk, blk), 0) & (blk - 1)
    col = lax.broadcasted_iota(jnp.int32, (2 * blk, blk), 1)

    def block(ki, carry, diagonal):
        m, l, acc = carry
        ks = pl.multiple_of(ki * blk, blk)
        s = _dot_nt(qs, k_ref[pl.ds(ks, blk), :])
        s = jnp.concatenate([s[i * blk:(i + 1) * blk] + (fq[i] - ft_ref[0, i:i + 1, pl.ds(ks, blk)])
                             for i in range(2)], axis=0)
        if diagonal:
            s = jnp.where(col <= row, s, -jnp.inf)
        m_new = jnp.maximum(m, jnp.max(s, axis=1, keepdims=True))
        alpha = jnp.exp(m - m_new)
        p = jnp.exp(s - m_new)
        l = alpha * l + jnp.sum(p, axis=1, keepdims=True)
        acc = alpha * acc + _dot(p.astype(BF16), v_ref[pl.ds(ks, blk), :])
        return m_new, l, acc

    init = (jnp.full((2 * blk, 1), NEG_BIG, F32), jnp.zeros((2 * blk, 1), F32), jnp.zeros((2 * blk, LANES), F32))
    pairs = qi // 2
    carry = lax.fori_loop(0, pairs, lambda p, c: block(2 * p + 1, block(2 * p, c, False), False), init)
    carry = lax.fori_loop(2 * pairs, qi, lambda ki, c: block(ki, c, False), carry)
    _, l, acc = block(qi, carry, True)
    o = acc / l
    lane = lax.broadcasted_iota(jnp.int32, (blk, LANES), 1)
    o_ref[...] = jnp.where(lane < HEAD_DIM, o[0:blk], o[blk:2 * blk])


def _fox_prompt(q, k, v, f, ft, B, L, blk):
    nq = L // blk
    kern = functools.partial(_fox_prompt_kernel, blk=blk)
    return pl.pallas_call(
        kern,
        grid=(B, N_HEADS // 2, nq),
        in_specs=[pl.BlockSpec((blk, LANES), lambda b, h, i: (b * nq + i, h)),
                  pl.BlockSpec((L, LANES), lambda b, h, i: (b, h)),
                  pl.BlockSpec((L, LANES), lambda b, h, i: (b, h)),
                  pl.BlockSpec((blk, N_HEADS), lambda b, h, i: (b * nq + i, 0)),
                  pl.BlockSpec((1, 2, L), lambda b, h, i: (h, 0, b))],
        out_specs=pl.BlockSpec((blk, LANES), lambda b, h, i: (b * nq + i, h)),
        out_shape=jax.ShapeDtypeStruct((B * L, MIX_HALF), F32),
        compiler_params=_params(("parallel", "parallel", "arbitrary"), 32),
        name="fox_attention_prompt",
    )(q, k, v, f, ft)


def _sb_weights(z, c, u, mask):
    sp = _softplus(z)
    ln = -sp
    if mask is not None:
        ln = jnp.where(mask, ln, 0.0)
    hi, lo = _split2(ln)
    e = _dot(hi, u) + _dot(lo, u)
    a = jnp.exp((z - sp) + e + c)
    if mask is not None:
        a = jnp.where(mask, a, 0.0)
    return a, c + jnp.sum(ln, axis=1, keepdims=True)


def _sb_block(z, c, v, u, mask):
    a, c = _sb_weights(z, c, u, mask)
    return _dot(a.astype(BF16), v), c


def _sb_prompt_kernel(q_ref, k_ref, v_ref, u_ref, o_ref, *, blk):
    qi = pl.program_id(2)
    qs = _stack_heads(q_ref[...])
    row = lax.broadcasted_iota(jnp.int32, (2 * blk, blk), 0) & (blk - 1)
    col = lax.broadcasted_iota(jnp.int32, (2 * blk, blk), 1)
    u = u_ref[...]

    def block(ki, c, acc, diagonal):
        ks = pl.multiple_of(ki * blk, blk)
        z = _dot_nt(qs, k_ref[pl.ds(ks, blk), :])
        pv, c = _sb_block(z, c, v_ref[pl.ds(ks, blk), :], u, (col < row) if diagonal else None)
        return c, acc + pv

    c, acc = block(qi, jnp.zeros((2 * blk, 1), F32), jnp.zeros((2 * blk, LANES), F32), True)

    def cond(state):
        j, alive, _, _ = state
        return (j < qi) & (alive > 0)

    def body(state):
        j, _, c, acc = state
        c, acc = block(qi - 1 - j, c, acc, False)
        return j + 1, (jnp.max(c) > SB_DEAD).astype(jnp.int32), c, acc

    _, _, _, acc = lax.while_loop(cond, body, (jnp.int32(0), jnp.int32(1), c, acc))
    lane = lax.broadcasted_iota(jnp.int32, (blk, LANES), 1)
    o_ref[...] = jnp.where(lane < HEAD_DIM, acc[0:blk], acc[blk:2 * blk])


def _sb_prompt(q, k, v, u, B, L, blk):
    nq = L // blk
    kern = functools.partial(_sb_prompt_kernel, blk=blk)
    return pl.pallas_call(
        kern,
        grid=(B, N_HEADS // 2, nq),
        in_specs=[pl.BlockSpec((blk, LANES), lambda b, h, i: (b * nq + i, h)),
                  pl.BlockSpec((L, LANES), lambda b, h, i: (b, h)),
                  pl.BlockSpec((L, LANES), lambda b, h, i: (b, h)),
                  _full((blk, blk))],
        out_specs=pl.BlockSpec((blk, LANES), lambda b, h, i: (b * nq + i, h)),
        out_shape=jax.ShapeDtypeStruct((B * L, MIX_HALF), F32),
        compiler_params=_params(("parallel", "parallel", "arbitrary"), 32),
        name="sb_attention_prompt",
    )(q, k, v, u)


N_PAGES = 16
DEC_SEQ = 8
BD_ROWS = DEC_SEQ * N_HEADS
SB_AHEAD = 3
SB_SLOTS = SB_AHEAD + 1


def _block_diag_q(q, hm):
    rows = [jnp.broadcast_to(q[t:t + 1, :], (N_HEADS, MIX_HALF)) * hm for t in range(DEC_SEQ)]
    return (jnp.concatenate(rows, axis=0) * ATT_SCALE).astype(BF16)


def _pad_new(x):
    return jnp.concatenate([x, jnp.zeros((PAGE - DEC_SEQ, x.shape[1]), x.dtype)], axis=0)


def _merge_heads(o, hm):
    return jnp.concatenate(
        [jnp.sum(o[t * N_HEADS:(t + 1) * N_HEADS, :] * hm, axis=0, keepdims=True) for t in range(DEC_SEQ)], axis=0)


def _new_key_mask(strict):
    row = lax.broadcasted_iota(jnp.int32, (BD_ROWS, PAGE), 0) // N_HEADS
    col = lax.broadcasted_iota(jnp.int32, (BD_ROWS, PAGE), 1)
    return (col < row) if strict else (col <= row)


def _page_scores(qbd, k_pages, kn_ref, j):
    if j < N_PAGES:
        return _dot(qbd, k_pages[j][0].reshape(MIX_HALF, PAGE).astype(BF16))
    return _dot_nt(qbd, _pad_new(kn_ref[...]).astype(BF16))


def _page_values(w, v_pages, vn_ref, j):
    if j < N_PAGES:
        return _dot_nt(w, v_pages[j][0].reshape(MIX_HALF, PAGE).astype(BF16))
    return _dot(w, _pad_new(vn_ref[...]).astype(BF16))


def _cumsum_lanes(x):
    n = x.shape[1]
    lane = lax.broadcasted_iota(jnp.int32, x.shape, 1)
    k = 1
    while k < n:
        x = x + jnp.where(lane >= k, pltpu.roll(x, k, 1), 0.0)
        k *= 2
    return x


def _fox_decode_kernel(pt_ref, *refs):
    k_pages = refs[0:N_PAGES]
    v_pages = refs[N_PAGES:2 * N_PAGES]
    lf_pages = refs[2 * N_PAGES:3 * N_PAGES]
    q_ref, kn_ref, vn_ref, lftn_ref, hm_ref, o_ref = refs[3 * N_PAGES:]
    del pt_ref
    hm = hm_ref[...]
    qbd = _block_diag_q(q_ref[...], hm)
    parts = [lf_pages[j][0] for j in range(N_PAGES)]
    parts.append(jnp.concatenate([lftn_ref[0], jnp.zeros((N_HEADS, PAGE - DEC_SEQ), F32)], axis=1))
    fk = _cumsum_lanes(jnp.concatenate(parts, axis=1))
    fnew = fk[:, N_PAGES * PAGE:]
    lane = lax.broadcasted_iota(jnp.int32, (N_HEADS, PAGE), 1)
    fq = jnp.concatenate([jnp.sum(jnp.where(lane == t, fnew, 0.0), axis=1, keepdims=True)
                          for t in range(DEC_SEQ)], axis=0)
    scores = []
    for j in range(N_PAGES + 1):
        bias = jnp.concatenate([fk[:, j * PAGE:(j + 1) * PAGE]] * DEC_SEQ, axis=0)
        s = _page_scores(qbd, k_pages, kn_ref, j) + (fq - bias)
        if j == N_PAGES:
            s = jnp.where(_new_key_mask(False), s, -jnp.inf)
        scores.append(s)
    m = scores[0].max(axis=1, keepdims=True)
    for s in scores[1:]:
        m = jnp.maximum(m, s.max(axis=1, keepdims=True))
    l = jnp.zeros((BD_ROWS, 1), F32)
    acc = jnp.zeros((BD_ROWS, MIX_HALF), F32)
    for j in range(N_PAGES + 1):
        p = jnp.exp(scores[j] - m)
        l = l + jnp.sum(p, axis=1, keepdims=True)
        acc = acc + _page_values(p.astype(BF16), v_pages, vn_ref, j)
    o_ref[...] = _merge_heads(acc / l, hm)


def _sb_decode_kernel(pt_ref, q_ref, kn_ref, vn_ref, hm_ref, u_ref, ck_hbm, cv_hbm, o_ref, kbuf, vbuf, ksem, vsem):
    b = pl.program_id(0)
    nb = pl.num_programs(0)
    page_slot = lambda j: (N_PAGES - 1 - j) % SB_SLOTS

    def page_copies(seq, j, slot):
        page = pt_ref[seq, j]
        return (pltpu.make_async_copy(ck_hbm.at[page], kbuf.at[slot], ksem.at[slot]),
                pltpu.make_async_copy(cv_hbm.at[page], vbuf.at[slot], vsem.at[slot]))

    def start(seq, j, slot):
        for cp in page_copies(seq, j, slot):
            cp.start()

    def wait(slot):
        for cp in page_copies(b, 0, slot):
            cp.wait()

    def start_newest(seq):
        for d in range(SB_AHEAD):
            start(seq, N_PAGES - 1 - d, d)

    @pl.when(b == 0)
    def _():
        start_newest(0)

    hm = hm_ref[...]
    qbd = _block_diag_q(q_ref[...], hm)
    u = u_ref[...]
    a, c = _sb_weights(_dot_nt(qbd, _pad_new(kn_ref[...]).astype(BF16)), jnp.zeros((BD_ROWS, 1), F32), u,
                       _new_key_mask(True))
    acc = _dot(a.astype(BF16), _pad_new(vn_ref[...]).astype(BF16))
    is_alive = lambda c: (jnp.max(c) > SB_DEAD).astype(jnp.int32)

    def cond(state):
        j, alive, _, _ = state
        return (j >= 0) & (alive > 0)

    def body(state):
        j, _, c, acc = state
        slot = page_slot(j)
        wait(slot)

        @pl.when(j >= SB_AHEAD)
        def _():
            start(b, j - SB_AHEAD, page_slot(j - SB_AHEAD))

        a, c = _sb_weights(_dot(qbd, kbuf[slot].reshape(MIX_HALF, PAGE).astype(BF16)), c, u, None)
        acc = acc + _dot_nt(a.astype(BF16), vbuf[slot].reshape(MIX_HALF, PAGE).astype(BF16))
        return j - 1, is_alive(c), c, acc

    j, _, _, acc = lax.while_loop(cond, body, (jnp.int32(N_PAGES - 1), is_alive(c), c, acc))

    for d in range(SB_AHEAD):
        @pl.when(j - d >= 0)
        def _():
            wait(page_slot(j - d))

    @pl.when(b + 1 < nb)
    def _():
        start_newest(b + 1)

    o_ref[...] = _merge_heads(acc, hm)


def _page_specs(shape_tail, n):
    specs = []
    for j in range(n):
        specs.append(pl.BlockSpec((1,) + shape_tail, functools.partial(
            lambda b, pt, j: (pt[b, j],) + (0,) * len(shape_tail), j=j)))
    return specs


def _fox_decode(page_table, ck, cv, clf, q, kn, vn, lftn, hm):
    nb = page_table.shape[0]
    seq = lambda n: pl.BlockSpec((DEC_SEQ, n), lambda b, pt: (b, 0))
    const = lambda shape: pl.BlockSpec(shape, lambda b, pt: (0,) * len(shape))
    kv_page = (N_HEADS, HEAD_DIM, PAGE)
    in_specs = (_page_specs(kv_page, N_PAGES) + _page_specs(kv_page, N_PAGES) + _page_specs((N_HEADS, PAGE), N_PAGES)
                + [seq(MIX_HALF), seq(MIX_HALF), seq(MIX_HALF),
                   pl.BlockSpec((1, N_HEADS, DEC_SEQ), lambda b, pt: (b, 0, 0)), const((N_HEADS, MIX_HALF))])
    return pl.pallas_call(
        _fox_decode_kernel,
        grid_spec=pltpu.PrefetchScalarGridSpec(
            num_scalar_prefetch=1, grid=(nb,), in_specs=in_specs, out_specs=seq(MIX_HALF)),
        out_shape=jax.ShapeDtypeStruct((nb * DEC_SEQ, MIX_HALF), F32),
        compiler_params=_params(("arbitrary",), 48),
        name="fox_attention_decode",
    )(page_table, *([ck] * N_PAGES), *([cv] * N_PAGES), *([clf] * N_PAGES), q, kn, vn, lftn, hm)


def _sb_decode(page_table, ck, cv, q, kn, vn, hm, u):
    nb = page_table.shape[0]
    seq = lambda n: pl.BlockSpec((DEC_SEQ, n), lambda b, pt: (b, 0))
    const = lambda shape: pl.BlockSpec(shape, lambda b, pt: (0,) * len(shape))
    kv_page = (N_HEADS, HEAD_DIM, PAGE)
    in_specs = [seq(MIX_HALF), seq(MIX_HALF), seq(MIX_HALF), const((N_HEADS, MIX_HALF)), const((PAGE, PAGE)),
                pl.BlockSpec(memory_space=pl.ANY), pl.BlockSpec(memory_space=pl.ANY)]
    return pl.pallas_call(
        _sb_decode_kernel,
        grid_spec=pltpu.PrefetchScalarGridSpec(
            num_scalar_prefetch=1, grid=(nb,), in_specs=in_specs, out_specs=seq(MIX_HALF),
            scratch_shapes=[pltpu.VMEM((SB_SLOTS,) + kv_page, F32), pltpu.VMEM((SB_SLOTS,) + kv_page, F32),
                            pltpu.SemaphoreType.DMA((SB_SLOTS,)), pltpu.SemaphoreType.DMA((SB_SLOTS,))]),
        out_shape=jax.ShapeDtypeStruct((nb * DEC_SEQ, MIX_HALF), F32),
        compiler_params=_params(("arbitrary",), 32),
        name="sb_attention_decode",
    )(page_table, q, kn, vn, hm, u, ck, cv)


def _x_head_rms(a, g):
    outs = []
    for h in range(X_HEADS):
        ah = a[:, h * X_HEAD_DIM:(h + 1) * X_HEAD_DIM]
        outs.append(ah * lax.rsqrt(jnp.mean(ah * ah, axis=-1, keepdims=True) + EPS))
    return jnp.concatenate(outs, axis=-1) * g


def _mix_out_kernel(x_ref, ya_ref, yb_ref, wo_ref, g_ref, wq_ref, gq_ref, x1_ref, qc_ref):
    y = _dot(ya_ref[...].astype(BF16), wo_ref[0:MIX_HALF, :]) + _dot(yb_ref[...].astype(BF16), wo_ref[MIX_HALF:, :])
    x1 = x_ref[...] + y
    x1_ref[...] = x1
    hc = _rms(x1, g_ref[...]).astype(BF16)
    qc_ref[...] = _x_head_rms(_dot(hc, wq_ref[...]), gq_ref[...])


def _mix_out(x, ya, yb, wo, g, wq, gq, tm):
    T = x.shape[0]
    row = lambda n: pl.BlockSpec((tm, n), lambda i: (i, 0))
    return pl.pallas_call(
        _mix_out_kernel,
        grid=(T // tm,),
        in_specs=[row(D_MODEL), row(MIX_HALF), row(MIX_HALF), _full((D_MODEL, D_MODEL)), _full((1, D_MODEL)),
                  _full((D_MODEL, X_WIDTH)), _full((1, X_WIDTH))],
        out_specs=[row(D_MODEL), row(X_WIDTH)],
        out_shape=[jax.ShapeDtypeStruct((T, D_MODEL), F32), jax.ShapeDtypeStruct((T, X_WIDTH), F32)],
        compiler_params=_params(("parallel",), 40),
        name="mixer_out_proj",
    )(x, ya, yb, wo, g, wq, gq)


def _mem_kv_kernel(m_ref, wk_ref, wv_ref, gk_ref, k_ref, v_ref, *, nb):
    m = m_ref[...].astype(BF16)
    k = _x_head_rms(_dot(m, wk_ref[0]), gk_ref[0])
    v = _dot(m, wv_ref[0])
    for b in range(nb):
        for h in range(X_HEADS):
            ls = slice(h * X_HEAD_DIM, (h + 1) * X_HEAD_DIM)
            k_ref[0, b, :, h, :] = k[b * N_MEM:(b + 1) * N_MEM, ls]
            v_ref[0, b, :, h, :] = v[b * N_MEM:(b + 1) * N_MEM, ls]


def _mem_kv(mem, wk, wv, gk, nb):
    B = mem.shape[0] // N_MEM
    depth = wk.shape[0]
    out = pl.BlockSpec((1, nb, N_MEM, X_HEADS, X_HEAD_DIM), lambda l, i: (l, i, 0, 0, 0))
    return pl.pallas_call(
        functools.partial(_mem_kv_kernel, nb=nb),
        grid=(depth, B // nb),
        in_specs=[pl.BlockSpec((nb * N_MEM, D_MODEL), lambda l, i: (i, 0)),
                  pl.BlockSpec((1, D_MODEL, X_WIDTH), lambda l, i: (l, 0, 0)),
                  pl.BlockSpec((1, D_MODEL, X_WIDTH), lambda l, i: (l, 0, 0)),
                  pl.BlockSpec((1, 1, X_WIDTH), lambda l, i: (l, 0, 0))],
        out_specs=[out, out],
        out_shape=[jax.ShapeDtypeStruct((depth, B, N_MEM, X_HEADS, X_HEAD_DIM), F32)] * 2,
        compiler_params=_params(("parallel", "parallel"), 32),
        name="memory_kv",
    )(mem, wk, wv, gk)


def _cross_kernel(x_ref, q_ref, mk_ref, mv_ref, wo_ref, g_ref, x2_ref, hf_ref, *, ns, tq):
    outs = []
    for s in range(ns):
        q = q_ref[s * tq:(s + 1) * tq, :]
        heads = []
        for h in range(X_HEADS):
            ls = slice(h * X_HEAD_DIM, (h + 1) * X_HEAD_DIM)
            qh = (q[:, ls] * (X_HEAD_DIM ** -0.5)).astype(BF16)
            sc = _dot_nt(qh, mk_ref[0, s, :, h, :].astype(BF16))
            p = jnp.exp(sc - jnp.max(sc, axis=1, keepdims=True))
            o = _dot(p.astype(BF16), mv_ref[0, s, :, h, :].astype(BF16))
            heads.append(o / jnp.sum(p, axis=1, keepdims=True))
        outs.append(jnp.concatenate(heads, axis=-1))
    o = outs[0] if ns == 1 else jnp.concatenate(outs, axis=0)
    x2 = x_ref[...] + _dot(o.astype(BF16), wo_ref[...])
    x2_ref[...] = x2
    hf_ref[...] = _rms(x2, g_ref[...])


def _cross(x, q, mk, mv, layer, wo, g, ns, tq):
    T = x.shape[0]
    L = T // mk.shape[1]
    nq = L // tq
    rows = ns * tq
    row = lambda n: pl.BlockSpec((rows, n), lambda i: (i, 0))
    mem = pl.BlockSpec((1, ns, N_MEM, X_HEADS, X_HEAD_DIM), lambda i: (layer, i // nq, 0, 0, 0))
    kern = functools.partial(_cross_kernel, ns=ns, tq=tq)
    return pl.pallas_call(
        kern,
        grid=(T // rows,),
        in_specs=[row(D_MODEL), row(X_WIDTH), mem, mem, _full((X_WIDTH, D_MODEL)), _full((1, D_MODEL))],
        out_specs=[row(D_MODEL), row(D_MODEL)],
        out_shape=[jax.ShapeDtypeStruct((T, D_MODEL), F32)] * 2,
        compiler_params=_params(("parallel",), 48),
        name="cross_attention",
    )(x, q, mk, mv, wo, g)


def _swiglu_kernel(h_ref, x_ref, wg_ref, wu_ref, wd_ref, o_ref, hb, acc):
    f = pl.program_id(1)

    @pl.when(f == 0)
    def _():
        hb[...] = h_ref[...].astype(BF16)
        acc[...] = jnp.zeros_like(acc)

    h = hb[...]
    gte = _dot(h, wg_ref[...])
    act = gte * (1.0 / (1.0 + jnp.exp(-gte))) * _dot(h, wu_ref[...])
    acc[...] += _dot(act.astype(BF16), wd_ref[...])

    @pl.when(f == pl.num_programs(1) - 1)
    def _():
        o_ref[...] = x_ref[...] + acc[...]


def _swiglu(h, x, wg, wu, wd, tm, tf):
    T = h.shape[0]
    dff = wg.shape[1]
    row = pl.BlockSpec((tm, D_MODEL), lambda i, f: (i, 0))
    return pl.pallas_call(
        _swiglu_kernel,
        grid=(T // tm, dff // tf),
        in_specs=[row, row, pl.BlockSpec((D_MODEL, tf), lambda i, f: (0, f)),
                  pl.BlockSpec((D_MODEL, tf), lambda i, f: (0, f)), pl.BlockSpec((tf, D_MODEL), lambda i, f: (f, 0))],
        out_specs=row,
        out_shape=jax.ShapeDtypeStruct((T, D_MODEL), F32),
        scratch_shapes=[pltpu.VMEM((tm, D_MODEL), BF16), pltpu.VMEM((tm, D_MODEL), F32)],
        compiler_params=_params(("parallel", "arbitrary"), 56),
        name="dense_swiglu",
    )(h, x, wg, wu, wd)


def _router_kernel(h_ref, w_ref, b_ref, r_ref):
    h_hi, h_lo = _split2(h_ref[...])
    w_hi, w_lo = _split2(w_ref[...])
    logits = _dot(h_hi, w_hi) + _dot(h_hi, w_lo) + _dot(h_lo, w_hi) + b_ref[...]
    lane = lax.broadcasted_iota(jnp.int32, logits.shape, 1)
    logits = jnp.where(lane < N_EXPERTS, logits, -jnp.inf)
    m1 = jnp.max(logits, axis=1, keepdims=True)
    i1 = jnp.min(jnp.where(logits == m1, lane, LANES), axis=1, keepdims=True)
    rest = jnp.where(lane == i1, -jnp.inf, logits)
    m2 = jnp.max(rest, axis=1, keepdims=True)
    i2 = jnp.min(jnp.where(rest == m2, lane, LANES), axis=1, keepdims=True)
    e2 = jnp.exp(m2 - m1)
    g1 = 1.0 / (1.0 + e2)
    g2 = e2 / (1.0 + e2)
    r_ref[...] = jnp.where(lane == 0, i1.astype(F32), jnp.where(lane == 1, i2.astype(F32),
                           jnp.where(lane == 2, g1, jnp.where(lane == 3, g2, 0.0))))


def _router(h, w, b, tm):
    T = h.shape[0]
    return pl.pallas_call(
        _router_kernel,
        grid=(T // tm,),
        in_specs=[pl.BlockSpec((tm, D_MODEL), lambda i: (i, 0)), _full((D_MODEL, LANES)), _full((1, LANES))],
        out_specs=pl.BlockSpec((tm, LANES), lambda i: (i, 0)),
        out_shape=jax.ShapeDtypeStruct((T, LANES), F32),
        compiler_params=_params(("parallel",), 32),
        name="moe_router",
    )(h, w, b)


MOE_TM = 512
MOE_NF = 2
COMBINE_TM = 512


def _moe_plan(r, tm):
    T = r.shape[0]
    n_pairs = 2 * T
    e = r[:, 0:2].astype(jnp.int32).T.reshape(-1)
    onehot = (e[:, None] == jnp.arange(N_EXPERTS, dtype=jnp.int32)[None, :]).astype(jnp.int32)
    csum = jnp.cumsum(onehot, axis=0)
    rank = jnp.sum(csum * onehot, axis=1) - 1
    psz = (csum[-1] + tm - 1) // tm * tm
    gend = jnp.cumsum(psz)
    pos = (gend - psz)[e] + rank
    n_rows = n_pairs + N_EXPERTS * tm
    n_tiles = n_rows // tm
    src = jnp.zeros((n_rows,), jnp.int32).at[pos].set(jnp.arange(n_pairs, dtype=jnp.int32) % T)
    t0 = jnp.arange(n_tiles, dtype=jnp.int32) * tm
    tile_e = jnp.minimum(jnp.sum((t0[:, None] >= gend[None, :]).astype(jnp.int32), axis=1), N_EXPERTS - 1)
    n_valid = (gend[-1] // tm).astype(jnp.int32).reshape(1)
    return src.reshape(n_tiles, 1, tm), pos.astype(jnp.int32), tile_e, n_valid


def _row_gather(src_hbm, idx_ref, dst, sem, r0, n, unrolled):
    def start(r):
        pltpu.make_async_copy(src_hbm.at[pl.ds(idx_ref[0, 0, r0 + r], 1)], dst.at[pl.ds(r0 + r, 1)], sem).start()

    if unrolled:
        for r in range(n):
            start(r)
    else:
        def body(r, carry):
            start(r)
            return carry
        lax.fori_loop(0, n, body, 0, unroll=8)


def _moe_kernel(te_ref, nv_ref, src_ref, srcn_ref, h_hbm, wg_ref, wu_ref, wd_ref, y_ref, xin, hb, gsem, *, tm):
    del te_ref
    i = pl.program_id(0)
    f = pl.program_id(1)
    nt = pl.num_programs(0)
    nf = pl.num_programs(1)
    slot = i % 2
    is_valid = i < nv_ref[0]
    part = tm // MOE_NF
    r0 = f * part

    def wait_tile(s):
        pltpu.make_async_copy(h_hbm.at[pl.ds(0, tm)], xin.at[s], gsem.at[s]).wait()

    @pl.when((i == 0) & (f == 0))
    def _():
        _row_gather(h_hbm, src_ref, xin.at[0], gsem.at[0], 0, tm, False)

    @pl.when(f == 0)
    def _():
        wait_tile(slot)
        hb[...] = xin[slot].astype(BF16)
        y_ref[...] = jnp.zeros_like(y_ref)

    @pl.when(is_valid)
    def _():
        _row_gather(h_hbm, srcn_ref, xin.at[1 - slot], gsem.at[1 - slot], r0, part, True)
        h = hb[...]
        gte = _dot(h, wg_ref[0])
        act = gte * (1.0 / (1.0 + jnp.exp(-gte))) * _dot(h, wu_ref[0])
        y_ref[...] += _dot(act.astype(BF16), wd_ref[0])

    @pl.when(jnp.logical_not(is_valid))
    def _():
        _row_gather(h_hbm, srcn_ref, xin.at[1 - slot], gsem.at[1 - slot], r0, part, False)

    @pl.when((i == nt - 1) & (f == nf - 1))
    def _():
        wait_tile(1 - slot)


def _moe(h, src, tile_e, n_valid, wg, wu, wd, tm):
    n_tiles = src.shape[0]
    nf = MOE_NF
    tf = wg.shape[2] // nf
    fser = lambda i, f: jnp.where(i % 2 == 0, f, nf - 1 - f)
    smem_tile = lambda imap: pl.BlockSpec((1, 1, tm), imap, memory_space=pltpu.SMEM)
    kern = functools.partial(_moe_kernel, tm=tm)
    return pl.pallas_call(
        kern,
        grid_spec=pltpu.PrefetchScalarGridSpec(
            num_scalar_prefetch=2,
            grid=(n_tiles, nf),
            in_specs=[smem_tile(lambda i, f, te, nv: (i, 0, 0)),
                      smem_tile(lambda i, f, te, nv: (jnp.minimum(i + 1, n_tiles - 1), 0, 0)),
                      pl.BlockSpec(memory_space=pl.ANY),
                      pl.BlockSpec((1, D_MODEL, tf), lambda i, f, te, nv: (te[i], 0, fser(i, f))),
                      pl.BlockSpec((1, D_MODEL, tf), lambda i, f, te, nv: (te[i], 0, fser(i, f))),
                      pl.BlockSpec((1, tf, D_MODEL), lambda i, f, te, nv: (te[i], fser(i, f), 0))],
            out_specs=pl.BlockSpec((tm, D_MODEL), lambda i, f, te, nv: (i, 0)),
            scratch_shapes=[pltpu.VMEM((2, tm, D_MODEL), F32), pltpu.VMEM((tm, D_MODEL), BF16),
                            pltpu.SemaphoreType.DMA((2,))]),
        out_shape=jax.ShapeDtypeStruct((n_tiles * tm, D_MODEL), F32),
        compiler_params=_params(("arbitrary", "arbitrary"), 56),
        name="moe_swiglu",
    )(tile_e, n_valid, src, src, h, wg, wu, wd)


def _combine_kernel(pos_ref, posn_ref, x_ref, r_ref, y_hbm, o_ref, buf, sem, *, tm):
    i = pl.program_id(0)
    nt = pl.num_programs(0)
    slot = i % 2

    @pl.when(i == 0)
    def _():
        _row_gather(y_hbm, pos_ref, buf.at[0], sem.at[0], 0, 2 * tm, False)

    @pl.when(i + 1 < nt)
    def _():
        _row_gather(y_hbm, posn_ref, buf.at[1 - slot], sem.at[1 - slot], 0, 2 * tm, False)

    pltpu.make_async_copy(y_hbm.at[pl.ds(0, 2 * tm)], buf.at[slot], sem.at[slot]).wait()
    r = r_ref[...]
    o_ref[...] = x_ref[...] + (r[:, 2:3] * buf[slot, 0:tm, :] + r[:, 3:4] * buf[slot, tm:2 * tm, :])


def _combine(x, r, pos, y, tm):
    T = x.shape[0]
    nt = T // tm
    smem_tile = lambda imap: pl.BlockSpec((1, 1, 2 * tm), imap, memory_space=pltpu.SMEM)
    return pl.pallas_call(
        functools.partial(_combine_kernel, tm=tm),
        grid=(nt,),
        in_specs=[smem_tile(lambda i: (i, 0, 0)), smem_tile(lambda i: (jnp.minimum(i + 1, nt - 1), 0, 0)),
                  pl.BlockSpec((tm, D_MODEL), lambda i: (i, 0)), pl.BlockSpec((tm, LANES), lambda i: (i, 0)),
                  pl.BlockSpec(memory_space=pl.ANY)],
        out_specs=pl.BlockSpec((tm, D_MODEL), lambda i: (i, 0)),
        out_shape=jax.ShapeDtypeStruct((T, D_MODEL), F32),
        scratch_shapes=[pltpu.VMEM((2, 2 * tm, D_MODEL), F32), pltpu.SemaphoreType.DMA((2,))],
        compiler_params=_params(("arbitrary",), 40),
        name="moe_combine",
    )(pos, pos, x, r, y)


def _tri(n, kind):
    r = lax.broadcasted_iota(jnp.int32, (n, n), 0)
    c = lax.broadcasted_iota(jnp.int32, (n, n), 1)
    m = {"lower_incl": r >= c, "upper_incl": r <= c, "lower_strict": r > c}[kind]
    return m.astype(BF16)


def _head_mask(n_heads, head_dim):
    h = lax.broadcasted_iota(jnp.int32, (n_heads, n_heads * head_dim), 0)
    c = lax.broadcasted_iota(jnp.int32, (n_heads, n_heads * head_dim), 1) // head_dim
    return (h == c).astype(F32)


def _stack(x, mem_k, mem_v, W, group):
    T = x.shape[0]
    B, L = group["B"], group["L"]
    prompt = group["kind"] == "prompt"
    tm = 1024 if prompt else 512
    q_dtype = BF16 if prompt else F32
    new = {}

    p, q, k32, kb, v32, vb, lf, lft = _in_even(x, W["g_mix0"], W["w_in_e"], W["wf"], W["wft"], W["bf"], W["bft"],
                                               W["g_fq"], W["g_fk"], W["bd64"], tm, q_dtype)
    ya, pool_state = _pool(group["pool_prefix"], p.reshape(B, L, MIX_HALF), W["w_pool"], W["pool_scale"],
                           ns=group["seq_per_step"], ch=group["chunk"], start_pos=group["start_pos"])
    if prompt:
        f, ft = _cumsum_prompt(lf, lft, W["tril256"], W["triu256"], B, L, 256)
        yb = _fox_prompt(q, kb, vb, f, ft.reshape(N_HEADS // 2, 2, T), B, L, 256)
    else:
        lftn = lft.reshape(N_HEADS, B, L).transpose(1, 0, 2)
        yb = _fox_decode(group["page_table"], group["fox_k"], group["fox_v"], group["fox_lf"], q, k32, v32, lftn,
                         W["hm64"])
    new.update(fox_k=k32, fox_v=v32, fox_logf=lf, pool=pool_state)
    x, qc = _mix_out(x, ya.reshape(T, MIX_HALF), yb, W["w_out_e"], W["g_cross0"], W["w_cq0"], W["g_cq0"], tm)
    x, hf = _cross(x, qc, mem_k, mem_v, 0, W["w_co0"], W["g_ffn0"], ns=group["x_ns"], tq=group["x_tq"])
    x = _swiglu(hf, x, W["w_gate"], W["w_up"], W["w_down"], 512, 1408)

    u, q, k32, kb, v32, vb = _in_odd(x, W["g_mix1"], W["w_in_o"], tm, q_dtype)
    yc, conv_state = _conv(group["conv_prefix"], u.reshape(B, L, MIX_HALF), W["conv_w"], W["conv_b"], W["ln_g"],
                           W["ln_b"], ns=group["seq_per_step"], ch=group["conv_chunk"])
    if prompt:
        yd = _sb_prompt(q, kb, vb, W["ustrict256"], B, L, 256)
    else:
        yd = _sb_decode(group["page_table"], group["sb_k"], group["sb_v"], q, k32, v32, W["hm64"], W["ustrict128"])
    new.update(sb_k=k32, sb_v=v32, conv=conv_state)
    x, qc = _mix_out(x, yc.reshape(T, MIX_HALF), yd, W["w_out_o"], W["g_cross1"], W["w_cq1"], W["g_cq1"], tm)
    x, hf = _cross(x, qc, mem_k, mem_v, 1, W["w_co1"], W["g_ffn1"], ns=group["x_ns"], tq=group["x_tq"])
    r = _router(hf, W["w_router"], W["b_router"], 512)
    return x, hf, r, new


def kernel(x_prompt, x_sample, cache_fox_k, cache_fox_v, cache_fox_logf, cache_sb_k, cache_sb_v, cache_mem_k, cache_mem_v, state_pool, state_conv, page_table, mem_prompt, norm_mix, norm_cross, norm_ffn, w_cq, w_ck, w_cv, w_co, g_cq, g_ck, w_in_e, b_f, w_pool_grp, pool_scale, g_fq, g_fk, w_out_e, w_gate, w_up, w_down, w_in_o, conv_w, conv_b, ln_g, ln_b, w_out_o, w_router, b_router, we_gate, we_up, we_down):
    B, L, D = x_prompt.shape
    SB, SL, _ = x_sample.shape
    n_pool = cache_fox_k.shape[1]
    bf = lambda a: a.astype(BF16)
    row = lambda a: a.reshape(1, -1)

    wf = jnp.pad(w_in_e[0][:, 4 * MIX_HALF:], ((0, 0), (0, LANES - N_HEADS)))
    W = dict(
        g_mix0=row(norm_mix[0]), g_mix1=row(norm_mix[1]), g_cross0=row(norm_cross[0]), g_cross1=row(norm_cross[1]),
        g_ffn0=row(norm_ffn[0]), g_ffn1=row(norm_ffn[1]),
        w_in_e=bf(w_in_e[0][:, :4 * MIX_HALF]), wf=bf(wf), wft=bf(w_in_e[0][:, 4 * MIX_HALF:].T),
        bf=jnp.pad(row(b_f[0]), ((0, 0), (0, LANES - N_HEADS))), bft=b_f[0].reshape(N_HEADS, 1),
        g_fq=row(jnp.tile(g_fq[0], N_HEADS)), g_fk=row(jnp.tile(g_fk[0], N_HEADS)),
        w_pool=bf(w_pool_grp[0]), pool_scale=row(pool_scale[0]), w_out_e=bf(w_out_e[0]),
        w_cq0=bf(w_cq[0]), w_cq1=bf(w_cq[1]), w_co0=bf(w_co[0]), w_co1=bf(w_co[1]),
        g_cq0=row(jnp.tile(g_cq[0], X_HEADS)), g_cq1=row(jnp.tile(g_cq[1], X_HEADS)),
        w_gate=bf(w_gate[0]), w_up=bf(w_up[0]), w_down=bf(w_down[0]),
        w_in_o=bf(w_in_o[0]), conv_w=conv_w[0], conv_b=row(conv_b[0]), ln_g=row(ln_g[0]), ln_b=row(ln_b[0]),
        w_out_o=bf(w_out_o[0]),
        w_router=jnp.pad(w_router[0], ((0, 0), (0, LANES - N_EXPERTS))),
        b_router=jnp.pad(row(b_router[0]), ((0, 0), (0, LANES - N_EXPERTS))),
        we_gate=bf(we_gate[0]), we_up=bf(we_up[0]), we_down=bf(we_down[0]),
        bd64=jnp.kron(jnp.eye(N_HEADS, dtype=F32), jnp.ones((HEAD_DIM, HEAD_DIM), F32)).astype(BF16),
        hm64=_head_mask(N_HEADS, HEAD_DIM), eye8=jnp.eye(N_HEADS, dtype=BF16),
        tril256=_tri(256, "lower_incl"), triu256=_tri(256, "upper_incl"),
        ustrict256=_tri(256, "lower_strict"), ustrict128=_tri(128, "lower_strict"),
    )

    mk, mv = _mem_kv(mem_prompt.reshape(B * N_MEM, D), bf(w_ck), bf(w_cv),
                     jnp.tile(g_ck, (1, X_HEADS)).reshape(-1, 1, X_WIDTH), 2)

    prompt = dict(kind="prompt", B=B, L=L, start_pos=0, seq_per_step=1, chunk=256, conv_chunk=128,
                  pool_prefix=jnp.zeros((B, POOL_BUF, MIX_HALF), F32),
                  conv_prefix=jnp.zeros((B, CONV_BUF, MIX_HALF), F32), x_ns=1, x_tq=1024)
    xp, hp, rp, sp = _stack(x_prompt.reshape(B * L, D), mk, mv, W, prompt)

    kv_view = lambda c: jnp.transpose(c[0], (0, 2, 3, 1))
    sample = dict(kind="sample", B=SB, L=SL, start_pos=page_table.shape[1] * PAGE, seq_per_step=16,
                  chunk=SL, conv_chunk=SL, pool_prefix=state_pool[0], conv_prefix=state_conv[0], x_ns=8, x_tq=SL,
                  page_table=page_table,
                  fox_k=kv_view(cache_fox_k), fox_v=kv_view(cache_fox_v),
                  fox_lf=jnp.transpose(cache_fox_logf[0], (0, 2, 1)),
                  sb_k=kv_view(cache_sb_k), sb_v=kv_view(cache_sb_v))
    xs, hs, rs, ss = _stack(x_sample.reshape(SB * SL, D), cache_mem_k, cache_mem_v, W, sample)

    n_tok = B * L + SB * SL
    src, pos, tile_e, n_valid = _moe_plan(jnp.concatenate([rp, rs], axis=0), MOE_TM)
    ye = _moe(jnp.concatenate([hp, hs], axis=0), src, tile_e, n_valid, W["we_gate"], W["we_up"], W["we_down"], MOE_TM)

    def pair_rows(row0, n):
        both = [pos[s * n_tok + row0:s * n_tok + row0 + n].reshape(n // COMBINE_TM, COMBINE_TM) for s in range(2)]
        return jnp.concatenate(both, axis=1).reshape(n // COMBINE_TM, 1, 2 * COMBINE_TM)

    yp = _combine(xp, rp, pair_rows(0, B * L), ye, COMBINE_TM)
    ys = _combine(xs, rs, pair_rows(B * L, SB * SL), ye, COMBINE_TM)

    def outs(new, b, l):
        kv = lambda a: a.reshape(1, b, l, N_HEADS, HEAD_DIM)
        return (kv(new["fox_k"]), kv(new["fox_v"]), new["fox_logf"].reshape(1, b, l, N_HEADS),
                kv(new["sb_k"]), kv(new["sb_v"]), new["pool"][None], new["conv"][None])

    op = outs(sp, B, L)
    os_ = outs(ss, SB, SL)
    return (yp.reshape(B, L, D), ys.reshape(SB, SL, D), *op, mk, mv, *os_)
```
